```python
import jax, jax.numpy as jnp
from jax import lax
import numpy as np

D_MODEL = 2048
BATCH = 2
SEQ = 4096
DEPTH = 1
DEC_BATCH = 8
DEC_SEQ = 4
PAST_LEN = 16384
PAGE_SIZE = 128

D_MIX = D_MODEL
W_A = D_MIX // 2
W_B = D_MIX - W_A
HEAD_DIM_A = 128
N_HEADS_A = W_A // HEAD_DIM_A
HGRN_DK = 128
N_HEADS_B = W_B // HGRN_DK
HGRN_DV = W_B // N_HEADS_B
D_IN = 4 * W_A + 4 * W_B
Q_BLOCK = 128
CHUNK = 64
SB_BIAS_MEAN = -6.0
EPS = 1e-6

kernel_name = "stickbreak_hgrn2_hybrid_step"


def rmsnorm(x, w):
    xf = x.astype(jnp.float32)
    y = xf * lax.rsqrt(jnp.mean(xf * xf, axis=-1, keepdims=True) + EPS)
    return (y * w.astype(jnp.float32)).astype(x.dtype)


def head_rmsnorm(o, gain):
    y = o * lax.rsqrt(jnp.mean(o * o, axis=-1, keepdims=True) + EPS)
    return y * gain.astype(jnp.float32)


def in_proj(x, norm_w, w_in):
    h = rmsnorm(x, norm_w)
    p = jnp.einsum('btd,de->bte', h, w_in)
    offs = [W_A, 2 * W_A, 3 * W_A, 4 * W_A,
            4 * W_A + W_B, 4 * W_A + 2 * W_B, 4 * W_A + 3 * W_B]
    qa, ka, va, za, qb, fb, ib, zb = jnp.split(p, offs, axis=-1)
    B, T = x.shape[:2]
    heads_a = lambda t: t.reshape(B, T, N_HEADS_A, HEAD_DIM_A)
    return (heads_a(qa), heads_a(ka), heads_a(va), za,
            qb.reshape(B, T, N_HEADS_B, HGRN_DK), fb.reshape(B, T, N_HEADS_B, HGRN_DK),
            ib.reshape(B, T, N_HEADS_B, HGRN_DV), zb)


def stick_breaking(q, k, v, bias, q_pos, k_pos):
    scale = HEAD_DIM_A ** -0.5
    z = jnp.einsum('bqhd,bkhd->bhqk', q.astype(jnp.float32), k.astype(jnp.float32)) * scale
    z = z + bias.astype(jnp.float32)[None, :, None, None]
    causal = (k_pos[None, :] < q_pos[:, None])[None, None]
    log_1m = jnp.where(causal, jax.nn.log_sigmoid(-z), 0.0)
    tail = lax.cumsum(log_1m, axis=3, reverse=True) - log_1m
    w = jnp.where(causal, jnp.exp(jax.nn.log_sigmoid(z) + tail), 0.0)
    return jnp.einsum('bhqk,bkhd->bqhd', w, v.astype(jnp.float32))


def stick_breaking_blocks(q, k, v, bias):
    B, T, H, D = q.shape
    nb = T // Q_BLOCK
    qb = q.reshape(B, nb, Q_BLOCK, H, D).transpose(1, 0, 2, 3, 4)
    k_pos = jnp.arange(T)

    def one_block(args):
        qi, i = args
        q_pos = i * Q_BLOCK + jnp.arange(Q_BLOCK)
        return stick_breaking(qi, k, v, bias, q_pos, k_pos)

    o = lax.map(one_block, (qb, jnp.arange(nb)))
    return o.transpose(1, 0, 2, 3, 4).reshape(B, T, H, D)


def hgrn_gates(qb, fb, ib, lb):
    q = jax.nn.silu(qb.astype(jnp.float32))
    g = lb + (1.0 - lb) * jax.nn.sigmoid(fb.astype(jnp.float32))
    return q, 1.0 - g, ib.astype(jnp.float32), jnp.log(g)


def hgrn2_chunked(q, k, v, log_f, s0):
    B, T = q.shape[:2]
    C = min(CHUNK, T)
    pad = (-T) % C
    padw = ((0, 0), (0, pad), (0, 0), (0, 0))
    q, k, v, log_f = (jnp.pad(a, padw) for a in (q, k, v, log_f))
    n = (T + pad) // C
    to_chunks = lambda a: a.reshape(B, n, C, *a.shape[2:]).transpose(1, 0, 2, 3, 4)
    tri = jnp.tril(jnp.ones((C, C), dtype=bool))[None, :, :, None, None]

    def step(S, inp):
        qc, kc, vc, gc = inp
        b = jnp.cumsum(gc, axis=1)
        o_inter = jnp.einsum('bthk,bhkv->bthv', qc * jnp.exp(b), S)
        diff = b[:, :, None] - b[:, None]
        decay = jnp.where(tri, jnp.exp(jnp.minimum(diff, 0.0)), 0.0)
        att = jnp.einsum('btshk,bshk->bhts', qc[:, :, None] * decay, kc)
        o_intra = jnp.einsum('bhts,bshv->bthv', att, vc)
        bl = b[:, -1]
        S_new = S * jnp.exp(bl)[..., None] + jnp.einsum(
            'bshk,bshv->bhkv', kc * jnp.exp(bl[:, None] - b), vc)
        return S_new, o_inter + o_intra

    S_T, o = lax.scan(step, s0, (to_chunks(q), to_chunks(k), to_chunks(v), to_chunks(log_f)))
    o = o.transpose(1, 0, 2, 3, 4).reshape(B, T + pad, N_HEADS_B, HGRN_DV)[:, :T]
    return o, S_T


def out_proj(o_a, o_b, za, zb, gain_a, gain_b, w_out, dtype):
    B, T = o_a.shape[:2]
    a = head_rmsnorm(o_a, gain_a.reshape(N_HEADS_A, HEAD_DIM_A)).reshape(B, T, W_A) \
        * jax.nn.silu(za.astype(jnp.float32))
    b = head_rmsnorm(o_b, gain_b.reshape(N_HEADS_B, HGRN_DV)).reshape(B, T, W_B) \
        * jax.nn.silu(zb.astype(jnp.float32))
    m = jnp.concatenate([a, b], axis=-1).astype(dtype)
    return jnp.einsum('bte,ed->btd', m, w_out).astype(dtype)


def setup_inputs(seed: int = 0) -> dict:
    key = jax.random.key(seed)
    ks = jax.random.split(key, 14)
    n_pages = PAST_LEN // PAGE_SIZE
    n_used = DEC_BATCH * n_pages
    n_phys = n_used + n_used // 4
    nrm = jax.random.normal
    x_prompt = nrm(ks[0], (BATCH, SEQ, D_MODEL), jnp.float32)
    x_sample = nrm(ks[1], (DEC_BATCH, DEC_SEQ, D_MODEL), jnp.float32)
    cache_k = nrm(ks[2], (DEPTH, n_phys, PAGE_SIZE, N_HEADS_A, HEAD_DIM_A), jnp.float32)
    cache_v = nrm(ks[3], (DEPTH, n_phys, PAGE_SIZE, N_HEADS_A, HEAD_DIM_A), jnp.float32)
    state_s = 0.5 * nrm(ks[4], (DEPTH, DEC_BATCH, N_HEADS_B, HGRN_DK, HGRN_DV), jnp.float32)
    page_table = jax.random.permutation(ks[5], n_phys)[:n_used].reshape(
        DEC_BATCH, n_pages).astype(jnp.int32)
    norm_w = 1.0 + 0.02 * nrm(ks[6], (DEPTH, D_MODEL), jnp.float32)
    w_in = nrm(ks[7], (DEPTH, D_MODEL, D_IN), jnp.float32) * D_MODEL ** -0.5
    gain_a = 1.0 + 0.02 * nrm(ks[8], (DEPTH, W_A), jnp.float32)
    gain_b = 1.0 + 0.02 * nrm(ks[9], (DEPTH, W_B), jnp.float32)
    sb_bias = SB_BIAS_MEAN + 0.5 * nrm(ks[13], (DEPTH, N_HEADS_A), jnp.float32)
    lb_logits = 0.5 * nrm(ks[10], (DEPTH + 1, W_B), jnp.float32)
    w_out = nrm(ks[11], (DEPTH, D_MIX, D_MODEL), jnp.float32) * D_MIX ** -0.5
    final_norm_w = 1.0 + 0.02 * nrm(ks[12], (D_MODEL,), jnp.float32)
    return {"x_prompt": x_prompt, "x_sample": x_sample, "cache_k": cache_k,
            "cache_v": cache_v, "state_s": state_s, "page_table": page_table,
            "norm_w": norm_w, "w_in": w_in, "gain_a": gain_a, "gain_b": gain_b,
            "sb_bias": sb_bias, "lb_logits": lb_logits, "w_out": w_out,
            "final_norm_w": final_norm_w}


def reference(x_prompt, x_sample, cache_k, cache_v, state_s, page_table, norm_w, w_in,
              gain_a, gain_b, sb_bias, lb_logits, w_out, final_norm_w):
    n_pages = page_table.shape[1]
    past_len = n_pages * cache_k.shape[2]
    db, ts = x_sample.shape[:2]
    lb_all = jnp.cumsum(jax.nn.softmax(lb_logits.astype(jnp.float32), axis=0), axis=0)
    xp, xs = x_prompt, x_sample
    kp_l, vp_l, sp_l, ks_l, vs_l, ss_l = [], [], [], [], [], []
    for l in range(DEPTH):
        lb = lb_all[l].reshape(N_HEADS_B, HGRN_DK)
        qa, ka, va, za, qb, fb, ib, zb = in_proj(xp, norm_w[l], w_in[l])
        o_a = stick_breaking_blocks(qa, ka, va, sb_bias[l])
        q, k, v, lf = hgrn_gates(qb, fb, ib, lb)
        s0 = jnp.zeros((xp.shape[0], N_HEADS_B, HGRN_DK, HGRN_DV), jnp.float32)
        o_b, s_p = hgrn2_chunked(q, k, v, lf, s0)
        xp = xp + out_proj(o_a, o_b, za, zb, gain_a[l], gain_b[l], w_out[l], xp.dtype)
        kp_l.append(ka)
        vp_l.append(va)
        sp_l.append(s_p.astype(state_s.dtype))
        qa, ka, va, za, qb, fb, ib, zb = in_proj(xs, norm_w[l], w_in[l])
        k_past = cache_k[l][page_table].reshape(db, past_len, N_HEADS_A, HEAD_DIM_A)
        v_past = cache_v[l][page_table].reshape(db, past_len, N_HEADS_A, HEAD_DIM_A)
        k_all = jnp.concatenate([k_past, ka.astype(k_past.dtype)], axis=1)
        v_all = jnp.concatenate([v_past, va.astype(v_past.dtype)], axis=1)
        q_pos = past_len + jnp.arange(ts)
        k_pos = jnp.arange(past_len + ts)
        o_a = stick_breaking(qa, k_all, v_all, sb_bias[l], q_pos, k_pos)
        q, k, v, lf = hgrn_gates(qb, fb, ib, lb)
        o_b, s_s = hgrn2_chunked(q, k, v, lf, state_s[l].astype(jnp.float32))
        xs = xs + out_proj(o_a, o_b, za, zb, gain_a[l], gain_b[l], w_out[l], xs.dtype)
        ks_l.append(ka.astype(cache_k.dtype))
        vs_l.append(va.astype(cache_v.dtype))
        ss_l.append(s_s.astype(state_s.dtype))
    y_prompt = rmsnorm(xp, final_norm_w)
    y_sample = rmsnorm(xs, final_norm_w)
    return (y_prompt, y_sample, jnp.stack(kp_l), jnp.stack(vp_l), jnp.stack(sp_l),
            jnp.stack(ks_l), jnp.stack(vs_l), jnp.stack(ss_l))
```

```python
import functools

import jax
import jax.numpy as jnp
from jax import lax
from jax.experimental import pallas as pl
from jax.experimental.pallas import tpu as pltpu

EPS = 1e-6
HEAD_DIM = 128
N_COMP = 8
SB_BLOCK = 128
HGRN_CHUNK = 64
HGRN_SUB = 16
SAMPLE_PAD = 8
VMEM_LIMIT = 56 * 1024 * 1024

F32 = jnp.float32
BF16 = jnp.bfloat16


def _dot(a, b, **kw):
    return jnp.dot(a, b, preferred_element_type=F32, **kw)


def _dot_nt(a, b):
    return lax.dot_general(a, b, (((1,), (1,)), ((), ())), preferred_element_type=F32)


def _dot_tn(a, b):
    return lax.dot_general(a, b, (((0,), (0,)), ((), ())), preferred_element_type=F32)


def _sigmoid(x):
    return 1.0 / (1.0 + jnp.exp(-x))


def _softplus(x):
    return jnp.maximum(x, 0.0) + jnp.log(1.0 + jnp.exp(-jnp.abs(x)))


def _split_bf16(x):
    hi = x.astype(BF16)
    lo = (x - hi.astype(F32)).astype(BF16)
    return hi, lo


def _in_proj_kernel(x_ref, nw_ref, w_ref, o_ref, h_scr):
    @pl.when(pl.program_id(1) == 0)
    def _():
        x = x_ref[...]
        ms = jnp.mean(x * x, axis=-1, keepdims=True)
        h_scr[...] = (x * lax.rsqrt(ms + EPS) * nw_ref[...]).astype(BF16)

    o_ref[0] = _dot(h_scr[...], w_ref[...])


def _in_proj(x, norm_w, w_bf16, bm):
    t, d = x.shape
    d_in = w_bf16.shape[1]
    bn = d_in // N_COMP
    return pl.pallas_call(
        _in_proj_kernel,
        grid=(t // bm, N_COMP),
        in_specs=[
            pl.BlockSpec((bm, d), lambda m, n: (m, 0)),
            pl.BlockSpec((1, d), lambda m, n: (0, 0)),
            pl.BlockSpec((d, bn), lambda m, n: (0, n)),
        ],
        out_specs=pl.BlockSpec((1, bm, bn), lambda m, n: (n, m, 0)),
        out_shape=jax.ShapeDtypeStruct((N_COMP, t, bn), F32),
        scratch_shapes=[pltpu.VMEM((bm, d), BF16)],
        compiler_params=pltpu.CompilerParams(
            dimension_semantics=("arbitrary", "arbitrary"), vmem_limit_bytes=VMEM_LIMIT),
        name="in_proj",
    )(x, norm_w, w_bf16)


def _head_norm_gate(o, gain, z):
    ms = jnp.mean(o * o, axis=-1, keepdims=True)
    return o * lax.rsqrt(ms + EPS) * gain * (z * _sigmoid(z))


def _sb_prompt_kernel(bias_ref, q_ref, k_ref, v_ref, z_ref, gain_ref, mu_ref, o_ref,
                      kb_scr, vb_scr):
    bq = SB_BLOCK
    t_len = q_ref.shape[1]
    nq = t_len // bq
    bias = bias_ref[0, pl.program_id(1)]
    scale = HEAD_DIM ** -0.5
    kb_scr[...] = k_ref[0].astype(BF16)
    vb_scr[...] = v_ref[0].astype(BF16)
    mu = mu_ref[...]
    gain = gain_ref[...]
    row = lax.broadcasted_iota(jnp.int32, (bq, bq), 0)
    col = lax.broadcasted_iota(jnp.int32, (bq, bq), 1)
    causal = col < row

    def block(q, j, carry, acc, masked):
        ks = pl.multiple_of(j * bq, bq)
        kb = kb_scr[pl.ds(ks, bq), :]
        vb = vb_scr[pl.ds(ks, bq), :]
        z = _dot_nt(q, kb) + bias
        sp = _softplus(z)
        if masked:
            sp = jnp.where(causal, sp, 0.0)
        hi, lo = _split_bf16(sp)
        r = _dot(jnp.concatenate([hi, lo], axis=1), mu)
        w = jnp.exp(z - sp - r[:, :bq] - carry)
        if masked:
            w = jnp.where(causal, w, 0.0)
        acc = acc + _dot(w.astype(BF16), vb)
        return carry + r[:, bq:], acc

    def q_block(qi, _):
        qs = pl.multiple_of(qi * bq, bq)
        q = (q_ref[0, pl.ds(qs, bq), :] * scale).astype(BF16)
        zero = jnp.zeros((bq, bq), F32)
        carry, acc = block(q, qi, zero, zero, True)

        def k_block(jj, ca):
            return block(q, qi - 1 - jj, ca[0], ca[1], False)

        carry, acc = lax.fori_loop(0, qi, k_block, (carry, acc))
        zg = z_ref[0, pl.ds(qs, bq), :]
        o_ref[pl.ds(qs, bq), :] = _head_norm_gate(acc, gain, zg).astype(o_ref.dtype)
        return 0

    lax.fori_loop(0, nq, q_block, 0)


def _sb_matrix(n):
    j = jnp.arange(2 * n)[:, None] % n
    s = jnp.arange(2 * n)[None, :]
    return jnp.where(s < n, (j > s), True).astype(BF16)


def _sb_prompt(p3, sb_bias, gain_a, batch, seq):
    n_heads = p3.shape[2] // HEAD_DIM
    mu = _sb_matrix(SB_BLOCK)
    comp = lambda c: pl.BlockSpec((1, seq, HEAD_DIM), lambda b, h: (c, b, h))
    return pl.pallas_call(
        _sb_prompt_kernel,
        grid=(batch, n_heads),
        in_specs=[
            pl.BlockSpec(memory_space=pltpu.SMEM),
            comp(0), comp(1), comp(2), comp(3),
            pl.BlockSpec((1, HEAD_DIM), lambda b, h: (0, h)),
            pl.BlockSpec(mu.shape, lambda b, h: (0, 0)),
        ],
        out_specs=pl.BlockSpec((seq, HEAD_DIM), lambda b, h: (b, h)),
        out_shape=jax.ShapeDtypeStruct((batch * seq, n_heads * HEAD_DIM), BF16),
        scratch_shapes=[pltpu.VMEM((seq, HEAD_DIM), BF16), pltpu.VMEM((seq, HEAD_DIM), BF16)],
        compiler_params=pltpu.CompilerParams(
            dimension_semantics=("arbitrary", "arbitrary"), vmem_limit_bytes=VMEM_LIMIT),
        name="sb_prompt",
    )(sb_bias, p3, p3, p3, p3, gain_a, mu)


def _lower_bound(lbl_ref):
    l = lbl_ref[...]
    e = jnp.exp(l - jnp.max(l, axis=0, keepdims=True))
    return e[0:1, :] / jnp.sum(e, axis=0, keepdims=True)


def _hgrn_chunk(qb, fb, ib, lb, st, n_valid=None):
    c_len = qb.shape[0]
    sub = min(HGRN_SUB, c_len)
    q = qb * _sigmoid(qb)
    g = lb + (1.0 - lb) * _sigmoid(fb)
    kk = 1.0 - g
    lg = jnp.log(g)
    if n_valid is not None:
        live = lax.broadcasted_iota(jnp.int32, (c_len, HEAD_DIM), 0) < n_valid
        kk = jnp.where(live, kk, 0.0)
        lg = jnp.where(live, lg, 0.0)
    ti = lax.broadcasted_iota(jnp.int32, (c_len, c_len), 0)
    si = lax.broadcasted_iota(jnp.int32, (c_len, c_len), 1)
    tri = (si <= ti).astype(F32)
    b = _dot(tri, lg, precision=lax.Precision.HIGHEST)
    bl = b[c_len - 1:c_len, :]
    vb = ib.astype(BF16)
    o_inter = _dot_nt((q * jnp.exp(b)).astype(BF16), st.astype(BF16))
    parts = []
    for i in range(c_len // sub):
        lo_r, hi_r = i * sub, (i + 1) * sub
        mid = lo_r + sub // 2
        m = b[mid:mid + 1, :]
        qi = (q[lo_r:hi_r] * jnp.exp(b[lo_r:hi_r] - m)).astype(BF16)
        ki = (kk[:hi_r] * jnp.exp(m - b[:hi_r])).astype(BF16)
        att = _dot_nt(qi, ki)
        t_pos = lax.broadcasted_iota(jnp.int32, (sub, hi_r), 0) + lo_r
        s_pos = lax.broadcasted_iota(jnp.int32, (sub, hi_r), 1)
        att = jnp.where(s_pos <= t_pos, att, 0.0)
        parts.append(_dot(att.astype(BF16), vb[:hi_r]))
    o = o_inter + jnp.concatenate(parts, axis=0)
    kd = (kk * jnp.exp(bl - b)).astype(BF16)
    st_new = st * jnp.exp(bl) + _dot_tn(vb, kd)
    return o, st_new


def _hgrn_prompt_kernel(lbl_ref, q_ref, f_ref, i_ref, z_ref, gain_ref, o_ref, s_ref):
    c_len = HGRN_CHUNK
    n_chunks = q_ref.shape[1] // c_len
    lb = _lower_bound(lbl_ref)
    gain = gain_ref[...]

    def chunk(ci, st):
        rs = pl.multiple_of(ci * c_len, c_len)
        rows = pl.ds(rs, c_len)
        o, st = _hgrn_chunk(q_ref[0, rows, :], f_ref[0, rows, :], i_ref[0, rows, :], lb, st)
        o_ref[rows, :] = _head_norm_gate(o, gain, z_ref[0, rows, :]).astype(o_ref.dtype)
        return st

    st = lax.fori_loop(0, n_chunks, chunk, jnp.zeros((HEAD_DIM, HEAD_DIM), F32))
    s_ref[0, 0] = st.T


def _hgrn_prompt(p3, lb_logits, gain_b, batch, seq):
    n_heads = p3.shape[2] // HEAD_DIM
    comp = lambda c: pl.BlockSpec((1, seq, HEAD_DIM), lambda b, h: (c, b, h))
    return pl.pallas_call(
        _hgrn_prompt_kernel,
        grid=(batch, n_heads),
        in_specs=[
            pl.BlockSpec((lb_logits.shape[0], HEAD_DIM), lambda b, h: (0, h)),
            comp(4), comp(5), comp(6), comp(7),
            pl.BlockSpec((1, HEAD_DIM), lambda b, h: (0, h)),
        ],
        out_specs=[
            pl.BlockSpec((seq, HEAD_DIM), lambda b, h: (b, h)),
            pl.BlockSpec((1, 1, HEAD_DIM, HEAD_DIM), lambda b, h: (b, h, 0, 0)),
        ],
        out_shape=[
            jax.ShapeDtypeStruct((batch * seq, n_heads * HEAD_DIM), BF16),
            jax.ShapeDtypeStruct((batch, n_heads, HEAD_DIM, HEAD_DIM), F32),
        ],
        compiler_params=pltpu.CompilerParams(
            dimension_semantics=("arbitrary", "arbitrary"), vmem_limit_bytes=VMEM_LIMIT),
        name="hgrn_prompt",
    )(lb_logits, p3, p3, p3, p3, gain_b)


def _hgrn_decode_kernel(lbl_ref, q_ref, f_ref, i_ref, z_ref, gain_ref, s0_ref, o_ref, s_ref, *, n_valid):
    lb = _lower_bound(lbl_ref)
    o, st = _hgrn_chunk(q_ref[0], f_ref[0], i_ref[0], lb, s0_ref[0, 0].T, n_valid=n_valid)
    o_ref[...] = _head_norm_gate(o, gain_ref[...], z_ref[0]).astype(o_ref.dtype)
    s_ref[0, 0] = st.T


def _hgrn_decode(p3s, lb_logits, gain_b, state, n_valid):
    dec_batch, n_heads = state.shape[:2]
    comp = lambda c: pl.BlockSpec((1, SAMPLE_PAD, HEAD_DIM), lambda b, h: (c, b, h))
    return pl.pallas_call(
        functools.partial(_hgrn_decode_kernel, n_valid=n_valid),
        grid=(dec_batch, n_heads),
        in_specs=[
            pl.BlockSpec((lb_logits.shape[0], HEAD_DIM), lambda b, h: (0, h)),
            comp(4), comp(5), comp(6), comp(7),
            pl.BlockSpec((1, HEAD_DIM), lambda b, h: (0, h)),
            pl.BlockSpec((1, 1, HEAD_DIM, HEAD_DIM), lambda b, h: (b, h, 0, 0)),
        ],
        out_specs=[
            pl.BlockSpec((SAMPLE_PAD, HEAD_DIM), lambda b, h: (b, h)),
            pl.BlockSpec((1, 1, HEAD_DIM, HEAD_DIM), lambda b, h: (b, h, 0, 0)),
        ],
        out_shape=[
            jax.ShapeDtypeStruct((dec_batch * SAMPLE_PAD, n_heads * HEAD_DIM), BF16),
            jax.ShapeDtypeStruct(state.shape, F32),
        ],
        compiler_params=pltpu.CompilerParams(dimension_semantics=("arbitrary", "arbitrary")),
        name="hgrn_decode",
    )(lb_logits, p3s, p3s, p3s, p3s, gain_b, state)


def _sb_decode_kernel(pt_ref, qbd_ref, bias_ref, kn_ref, vn_ref, kp_ref, vp_ref, su_ref, z_ref, gain_ref,
                      o_ref, acc_scr, carry_scr, *, n_valid):
    del pt_ref
    j = pl.program_id(1)
    n_lanes = qbd_ref.shape[2]
    n_heads = acc_scr.shape[0]
    n_q = n_lanes // n_heads
    page = kp_ref.shape[0] // n_heads
    scale = HEAD_DIM ** -0.5
    bias = bias_ref[...]
    head = lambda h: slice(h * HEAD_DIM, (h + 1) * HEAD_DIM)

    def process(k_heads, v_heads, su, mask):
        zt = sum(_dot(k_heads[h].astype(BF16), qbd_ref[0, head(h), :]) for h in range(n_heads))
        zt = zt * scale + bias
        sp = _softplus(zt)
        if mask is not None:
            sp = jnp.where(mask, sp, 0.0)
            later = _dot(su, sp, precision=lax.Precision.HIGHEST)
        else:
            hi, lo = _split_bf16(sp)
            later = _dot(su, hi) + _dot(su, lo)
        w = jnp.exp(zt - sp - later - carry_scr[...])
        if mask is not None:
            w = jnp.where(mask, w, 0.0)
        carry_scr[...] += jnp.sum(sp, axis=0, keepdims=True)
        wb = w.astype(BF16)
        for h in range(n_heads):
            acc_scr[h] += _dot_tn(wb, v_heads[h].astype(BF16))

    @pl.when(j == 0)
    def _():
        acc_scr[...] = jnp.zeros_like(acc_scr)
        carry_scr[...] = jnp.zeros_like(carry_scr)
        r = kn_ref.shape[1]
        s_new = lax.broadcasted_iota(jnp.int32, (r, n_lanes), 0)
        t_new = lax.broadcasted_iota(jnp.int32, (r, n_lanes), 1) % n_q
        ji = lax.broadcasted_iota(jnp.int32, (r, r), 1)
        si = lax.broadcasted_iota(jnp.int32, (r, r), 0)
        process([kn_ref[0, :, head(h)] for h in range(n_heads)],
                [vn_ref[0, :, head(h)] for h in range(n_heads)],
                (ji > si).astype(F32), (s_new < t_new) & (s_new < n_valid))

    process([kp_ref[pl.ds(h, page, stride=n_heads), :] for h in range(n_heads)],
            [vp_ref[pl.ds(h, page, stride=n_heads), :] for h in range(n_heads)],
            su_ref[...], None)

    @pl.when(j == pl.num_programs(1) - 1)
    def _():
        rh = lax.broadcasted_iota(jnp.int32, (n_lanes, HEAD_DIM), 0) // n_q
        o = jnp.zeros((n_lanes, HEAD_DIM), F32)
        for h in range(n_heads):
            o = o + jnp.where(rh == h, acc_scr[h], 0.0)
        o_ref[0] = _head_norm_gate(o, gain_ref[...], z_ref[0]).astype(o_ref.dtype)


def _sb_decode(page_table, qbd, bias_l, p3s, cache_k, cache_v, z_r, gain_r, n_valid):
    dec_batch, n_pages = page_table.shape
    page_rows = cache_k.shape[1]
    width = qbd.shape[1]
    n_heads = width // HEAD_DIM
    page = page_rows // n_heads
    n_lanes = qbd.shape[2]
    ji = jnp.arange(page)[None, :]
    si = jnp.arange(page)[:, None]
    su = (ji > si).astype(BF16)
    new = lambda c: pl.BlockSpec((1, SAMPLE_PAD, width), lambda b, j, pt: (c, b, 0))
    paged = pl.BlockSpec((None, page_rows, HEAD_DIM), lambda b, j, pt: (pt[b, n_pages - 1 - j], 0, 0))
    grid_spec = pltpu.PrefetchScalarGridSpec(
        num_scalar_prefetch=1,
        grid=(dec_batch, n_pages),
        in_specs=[
            pl.BlockSpec((1, width, n_lanes), lambda b, j, pt: (b, 0, 0)),
            pl.BlockSpec((1, n_lanes), lambda b, j, pt: (0, 0)),
            new(1), new(2), paged, paged,
            pl.BlockSpec((page, page), lambda b, j, pt: (0, 0)),
            pl.BlockSpec((1, n_lanes, HEAD_DIM), lambda b, j, pt: (b, 0, 0)),
            pl.BlockSpec((n_lanes, HEAD_DIM), lambda b, j, pt: (0, 0)),
        ],
        out_specs=pl.BlockSpec((1, n_lanes, HEAD_DIM), lambda b, j, pt: (b, 0, 0)),
        scratch_shapes=[pltpu.VMEM((n_heads, n_lanes, HEAD_DIM), F32), pltpu.VMEM((1, n_lanes), F32)],
    )
    return pl.pallas_call(
        functools.partial(_sb_decode_kernel, n_valid=n_valid),
        grid_spec=grid_spec,
        out_shape=jax.ShapeDtypeStruct((dec_batch, n_lanes, HEAD_DIM), BF16),
        compiler_params=pltpu.CompilerParams(dimension_semantics=("arbitrary", "arbitrary")),
        name="sb_decode",
    )(page_table, qbd, bias_l, p3s, p3s, cache_k, cache_v, su, z_r, gain_r)


def _out_proj_kernel(x_ref, ma_ref, mb_ref, wa_ref, wb_ref, fw_ref, o_ref):
    y = x_ref[...] + _dot(ma_ref[...], wa_ref[...]) + _dot(mb_ref[...], wb_ref[...])
    ms = jnp.mean(y * y, axis=-1, keepdims=True)
    o_ref[...] = y * lax.rsqrt(ms + EPS) * fw_ref[...]


def _out_proj(x, m_a, m_b, w_bf16, final_w, bm):
    t, d = x.shape
    wa = m_a.shape[1]
    wb = m_b.shape[1]
    assert wa == wb
    return pl.pallas_call(
        _out_proj_kernel,
        grid=(t // bm,),
        in_specs=[
            pl.BlockSpec((bm, d), lambda m: (m, 0)),
            pl.BlockSpec((bm, wa), lambda m: (m, 0)),
            pl.BlockSpec((bm, wb), lambda m: (m, 0)),
            pl.BlockSpec((wa, d), lambda m: (0, 0)),
            pl.BlockSpec((wb, d), lambda m: (1, 0)),
            pl.BlockSpec((1, d), lambda m: (0, 0)),
        ],
        out_specs=pl.BlockSpec((bm, d), lambda m: (m, 0)),
        out_shape=jax.ShapeDtypeStruct((t, d), F32),
        compiler_params=pltpu.CompilerParams(
            dimension_semantics=("arbitrary",), vmem_limit_bytes=VMEM_LIMIT),
        name="out_proj",
    )(x, m_a, m_b, w_bf16, w_bf16, final_w)


def kernel(x_prompt, x_sample, cache_k, cache_v, state_s, page_table, norm_w, w_in, gain_a, gain_b,
           sb_bias, lb_logits, w_out, final_norm_w):
    depth = norm_w.shape[0]
    assert depth == 1
    batch, seq, d_model = x_prompt.shape
    dec_batch, dec_seq, _ = x_sample.shape
    n_heads_a, head_dim = cache_k.shape[3:]
    n_heads_b = state_s.shape[2]
    w_a = n_heads_a * head_dim
    assert head_dim == HEAD_DIM and state_s.shape[3:] == (HEAD_DIM, HEAD_DIM)
    assert dec_seq <= SAMPLE_PAD and seq % SB_BLOCK == 0 and seq % HGRN_CHUNK == 0

    w_in_b = w_in[0].astype(BF16)
    w_out_b = w_out[0].astype(BF16)
    nw = norm_w[0][None, :]
    fw = final_norm_w[None, :]
    ga = gain_a[0][None, :]
    gb = gain_b[0][None, :]

    xp = x_prompt.reshape(batch * seq, d_model)
    p3 = _in_proj(xp, nw, w_in_b, bm=1024)
    m_a = _sb_prompt(p3, sb_bias, ga, batch, seq)
    m_b, s_p = _hgrn_prompt(p3, lb_logits, gb, batch, seq)
    y_prompt = _out_proj(xp, m_a, m_b, w_out_b, fw, bm=512).reshape(batch, seq, d_model)
    k_p = p3[1].reshape(1, batch, seq, n_heads_a, head_dim)
    v_p = p3[2].reshape(1, batch, seq, n_heads_a, head_dim)

    xs = jnp.pad(x_sample, ((0, 0), (0, SAMPLE_PAD - dec_seq), (0, 0))).reshape(dec_batch * SAMPLE_PAD, d_model)
    p3s = _in_proj(xs, nw, w_in_b, bm=dec_batch * SAMPLE_PAD)
    rows = lambda c: p3s[c].reshape(dec_batch, SAMPLE_PAD, -1)[:, :dec_seq]
    q_s = rows(0).reshape(dec_batch, dec_seq, n_heads_a, head_dim).transpose(0, 2, 3, 1)
    qbd = (q_s[:, :, :, None, :] * jnp.eye(n_heads_a, dtype=F32)[None, :, None, :, None]).reshape(
        dec_batch, w_a, n_heads_a * dec_seq).astype(BF16)
    bias_l = jnp.repeat(sb_bias[0], dec_seq)[None, :]
    to_rows = lambda a: a.reshape(dec_batch, dec_seq, n_heads_a, head_dim).transpose(0, 2, 1, 3).reshape(
        dec_batch, n_heads_a * dec_seq, head_dim)
    z_r = to_rows(rows(3))
    gain_r = jnp.repeat(gain_a[0].reshape(n_heads_a, head_dim), dec_seq, axis=0)
    ck = cache_k[0].reshape(cache_k.shape[1], cache_k.shape[2] * n_heads_a, head_dim)
    cv = cache_v[0].reshape(cache_v.shape[1], cache_v.shape[2] * n_heads_a, head_dim)
    o_r = _sb_decode(page_table, qbd, bias_l, p3s, ck, cv, z_r, gain_r, dec_seq)
    m_a_s = o_r.reshape(dec_batch, n_heads_a, dec_seq, head_dim).transpose(0, 2, 1, 3).reshape(
        dec_batch, dec_seq, w_a)
    m_a_s = jnp.pad(m_a_s, ((0, 0), (0, SAMPLE_PAD - dec_seq), (0, 0))).reshape(dec_batch * SAMPLE_PAD, w_a)
    m_b_s, s_s = _hgrn_decode(p3s, lb_logits, gb, state_s[0], dec_seq)
    y_s = _out_proj(xs, m_a_s, m_b_s, w_out_b, fw, bm=dec_batch * SAMPLE_PAD)
    y_sample = y_s.reshape(dec_batch, SAMPLE_PAD, d_model)[:, :dec_seq]
    k_s = rows(1).reshape(1, dec_batch, dec_seq, n_heads_a, head_dim)
    v_s = rows(2).reshape(1, dec_batch, dec_seq, n_heads_a, head_dim)

    return (y_prompt, y_sample, k_p, v_p, s_p[None], k_s, v_s, s_s[None])
```

```python
import functools

import jax
import jax.numpy as jnp
from jax import lax
from jax.experimental import pallas as pl
from jax.experimental.pallas import tpu as pltpu

EPS = 1e-6
HEAD_DIM = 128
N_COMP = 8
SB_BLOCK = 128
SB_TILE = 512
LOG2E = 1.4426950408889634
HGRN_CHUNK = 64
HGRN_CHUNKS_PER_STEP = 8
CUMSUM_GROUP = 256
HGRN_SUB = 16
SAMPLE_PAD = 8
DECODE_PAGES_PER_STEP = 8
VMEM_LIMIT = 56 * 1024 * 1024

F32 = jnp.float32
BF16 = jnp.bfloat16


def _dot(a, b, **kw):
    return jnp.dot(a, b, preferred_element_type=F32, **kw)


def _dot_nt(a, b):
    return lax.dot_general(a, b, (((1,), (1,)), ((), ())), preferred_element_type=F32)


def _dot_tn(a, b):
    return lax.dot_general(a, b, (((0,), (0,)), ((), ())), preferred_element_type=F32)


def _sigmoid(x):
    return 1.0 / (1.0 + jnp.exp(-x))


def _softplus2(z2):
    return jnp.maximum(z2, 0.0) + jnp.log2(1.0 + jnp.exp2(-jnp.abs(z2)))


def _split_bf16(x):
    hi = x.astype(BF16)
    lo = (x - hi.astype(F32)).astype(BF16)
    return hi, lo


def _in_proj_kernel(x_ref, nw_ref, w_ref, o_ref, h_scr):
    @pl.when(pl.program_id(1) == 0)
    def _():
        x = x_ref[...]
        ms = jnp.mean(x * x, axis=-1, keepdims=True)
        h_scr[...] = (x * lax.rsqrt(ms + EPS) * nw_ref[...]).astype(BF16)

    o_ref[0] = _dot(h_scr[...], w_ref[...])


def _in_proj(x, norm_w, w_bf16, bm):
    t, d = x.shape
    d_in = w_bf16.shape[1]
    bn = d_in // N_COMP
    return pl.pallas_call(
        _in_proj_kernel,
        grid=(t // bm, N_COMP),
        in_specs=[
            pl.BlockSpec((bm, d), lambda m, n: (m, 0)),
            pl.BlockSpec((1, d), lambda m, n: (0, 0)),
            pl.BlockSpec((d, bn), lambda m, n: (0, n)),
        ],
        out_specs=pl.BlockSpec((1, bm, bn), lambda m, n: (n, m, 0)),
        out_shape=jax.ShapeDtypeStruct((N_COMP, t, bn), F32),
        scratch_shapes=[pltpu.VMEM((bm, d), BF16)],
        compiler_params=pltpu.CompilerParams(
            dimension_semantics=("arbitrary", "arbitrary"), vmem_limit_bytes=VMEM_LIMIT),
        name="in_proj",
    )(x, norm_w, w_bf16)


def _head_norm_gate(o, gain, z):
    ms = jnp.mean(o * o, axis=-1, keepdims=True)
    return o * lax.rsqrt(ms + EPS) * gain * (z * _sigmoid(z))


def _sb_prompt_kernel(bias_ref, q_ref, k_ref, v_ref, z_ref, gain_ref, mu_ref, o_ref,
                      kb_scr, vb_scr, acc_scr, carry_scr):
    tq = SB_TILE
    bk = SB_BLOCK
    n_tiles = q_ref.shape[1] // tq
    bias2 = bias_ref[0, pl.program_id(1)] * LOG2E
    qscale = HEAD_DIM ** -0.5 * LOG2E
    kb_scr[...] = k_ref[0].astype(BF16)
    vb_scr[...] = v_ref[0].astype(BF16)
    gain = gain_ref[...]

    def tile(q, kt, masked):
        ks = pl.multiple_of(kt * tq, tq)
        z = _dot_nt(q, kb_scr[pl.ds(ks, tq), :]) + bias2
        sp = _softplus2(z)
        if masked:
            causal = (lax.broadcasted_iota(jnp.int32, (tq, tq), 1)
                      < lax.broadcasted_iota(jnp.int32, (tq, tq), 0))
            sp = jnp.where(causal, sp, 0.0)
        zs = z - sp
        hi, lo = _split_bf16(sp)
        carry = carry_scr[...]
        ws = [None] * (tq // bk)
        for c in reversed(range(tq // bk)):
            cols = slice(c * bk, (c + 1) * bk)
            r = _dot(jnp.concatenate([hi[:, cols], lo[:, cols]], axis=1), mu_ref[...])
            ws[c] = jnp.exp2(zs[:, cols] - r[:, :bk] - carry)
            carry = carry + r[:, bk:]
        w = jnp.concatenate(ws, axis=1)
        if masked:
            w = jnp.where(causal, w, 0.0)
        carry_scr[...] = carry
        acc_scr[...] += _dot(w.astype(BF16), vb_scr[pl.ds(ks, tq), :])

    def q_tile(qt, _):
        qs = pl.multiple_of(qt * tq, tq)
        q = (q_ref[0, pl.ds(qs, tq), :] * qscale).astype(BF16)
        acc_scr[...] = jnp.zeros_like(acc_scr)
        carry_scr[...] = jnp.zeros_like(carry_scr)
        tile(q, qt, True)

        def k_tile(jj, _):
            tile(q, qt - 1 - jj, False)
            return 0

        lax.fori_loop(0, qt, k_tile, 0)
        zg = z_ref[0, pl.ds(qs, tq), :]
        o_ref[pl.ds(qs, tq), :] = _head_norm_gate(acc_scr[...], gain, zg).astype(o_ref.dtype)
        return 0

    lax.fori_loop(0, n_tiles, q_tile, 0)


def _sb_matrix(n):
    j = jnp.arange(2 * n)[:, None] % n
    s = jnp.arange(2 * n)[None, :]
    return jnp.where(s < n, (j > s), True).astype(BF16)


def _sb_prompt(p3, sb_bias, gain_a, batch, seq):
    n_heads = p3.shape[2] // HEAD_DIM
    mu = _sb_matrix(SB_BLOCK)
    comp = lambda c: pl.BlockSpec((1, seq, HEAD_DIM), lambda b, h: (c, b, h))
    return pl.pallas_call(
        _sb_prompt_kernel,
        grid=(batch, n_heads),
        in_specs=[
            pl.BlockSpec(memory_space=pltpu.SMEM),
            comp(0), comp(1), comp(2), comp(3),
            pl.BlockSpec((1, HEAD_DIM), lambda b, h: (0, h)),
            pl.BlockSpec(mu.shape, lambda b, h: (0, 0)),
        ],
        out_specs=pl.BlockSpec((seq, HEAD_DIM), lambda b, h: (b, h)),
        out_shape=jax.ShapeDtypeStruct((batch * seq, n_heads * HEAD_DIM), BF16),
        scratch_shapes=[pltpu.VMEM((seq, HEAD_DIM), BF16), pltpu.VMEM((seq, HEAD_DIM), BF16),
                        pltpu.VMEM((SB_TILE, HEAD_DIM), F32), pltpu.VMEM((SB_TILE, SB_BLOCK), F32)],
        compiler_params=pltpu.CompilerParams(
            dimension_semantics=("arbitrary", "arbitrary"), vmem_limit_bytes=VMEM_LIMIT),
        name="sb_prompt",
    )(sb_bias, p3, p3, p3, p3, gain_a, mu)


def _lower_bound(lbl_ref):
    l = lbl_ref[...]
    e = jnp.exp(l - jnp.max(l, axis=0, keepdims=True))
    return e[0:1, :] / jnp.sum(e, axis=0, keepdims=True)


def _chunk_cumsum(x, c_len):
    rows = x.shape[0]
    group = min(rows, CUMSUM_GROUP)
    ti = lax.broadcasted_iota(jnp.int32, (group, group), 0)
    si = lax.broadcasted_iota(jnp.int32, (group, group), 1)
    tri = jnp.where((si <= ti) & (si // c_len == ti // c_len), 1.0, 0.0).astype(BF16)
    hi = x.astype(BF16)
    rest = x - hi.astype(F32)
    mid = rest.astype(BF16)
    lo = (rest - mid.astype(F32)).astype(BF16)
    pieces = jnp.concatenate([hi, mid, lo], axis=1)
    outs = []
    for g0 in range(0, rows, group):
        r = _dot(tri, pieces[g0:g0 + group])
        outs.append((r[:, :HEAD_DIM] + r[:, HEAD_DIM:2 * HEAD_DIM]) + r[:, 2 * HEAD_DIM:])
    return jnp.concatenate(outs, axis=0)


def _hgrn_chunks(qb, fb, ib, lb, st, c_len, n_valid=None):
    rows = qb.shape[0]
    sub = min(HGRN_SUB, c_len)
    q = qb * _sigmoid(qb)
    g = lb + (1.0 - lb) * _sigmoid(fb)
    kk = 1.0 - g
    lg = jnp.log(g)
    if n_valid is not None:
        live = lax.broadcasted_iota(jnp.int32, (rows, HEAD_DIM), 0) < n_valid
        kk = jnp.where(live, kk, 0.0)
        lg = jnp.where(live, lg, 0.0)
    b = _chunk_cumsum(lg, c_len)
    vb = ib.astype(BF16)
    chunks = [slice(c * c_len, (c + 1) * c_len) for c in range(rows // c_len)]
    spans, operands = [], []
    for ch in chunks:
        for lo_r in range(ch.start, ch.stop, sub):
            hi_r = lo_r + sub
            m = b[lo_r + sub // 2:lo_r + sub // 2 + 1, :]
            qi = (q[lo_r:hi_r] * jnp.exp(b[lo_r:hi_r] - m)).astype(BF16)
            ki = (kk[ch.start:hi_r] * jnp.exp(m - b[ch.start:hi_r])).astype(BF16)
            spans.append((ch.start, lo_r, hi_r))
            operands.append((qi, ki))
    atts = [_dot_nt(qi, ki) for qi, ki in operands]
    parts = []
    for att, (c0, lo_r, hi_r) in zip(atts, spans):
        t_pos = lax.broadcasted_iota(jnp.int32, att.shape, 0) + (lo_r - c0)
        s_pos = lax.broadcasted_iota(jnp.int32, att.shape, 1)
        att = jnp.where(s_pos <= t_pos, att, 0.0)
        parts.append(_dot(att.astype(BF16), vb[c0:hi_r]))
    o_intra = jnp.concatenate(parts, axis=0)
    lasts = [b[ch.stop - 1:ch.stop, :] for ch in chunks]
    incs = [_dot_tn(vb[ch], (kk[ch] * jnp.exp(bl - b[ch])).astype(BF16)) for ch, bl in zip(chunks, lasts)]
    states = [st]
    for inc, bl in zip(incs, lasts):
        states.append(states[-1] * jnp.exp(bl) + inc)
    o_inter = [_dot_nt((q[ch] * jnp.exp(b[ch])).astype(BF16), s.astype(BF16)) for ch, s in zip(chunks, states)]
    return o_intra + jnp.concatenate(o_inter, axis=0), states[-1]


def _hgrn_prompt_kernel(lbl_ref, q_ref, f_ref, i_ref, z_ref, gain_ref, o_ref, s_ref):
    r_len = HGRN_CHUNK * HGRN_CHUNKS_PER_STEP
    lb = _lower_bound(lbl_ref)
    gain = gain_ref[...]

    def step(ci, st):
        rs = pl.multiple_of(ci * r_len, r_len)
        rows = pl.ds(rs, r_len)
        o, st = _hgrn_chunks(q_ref[0, rows, :], f_ref[0, rows, :], i_ref[0, rows, :], lb, st, HGRN_CHUNK)
        o_ref[rows, :] = _head_norm_gate(o, gain, z_ref[0, rows, :]).astype(o_ref.dtype)
        return st

    st = lax.fori_loop(0, q_ref.shape[1] // r_len, step, jnp.zeros((HEAD_DIM, HEAD_DIM), F32))
    s_ref[0, 0] = st.T


def _hgrn_prompt(p3, lb_logits, gain_b, batch, seq):
    n_heads = p3.shape[2] // HEAD_DIM
    comp = lambda c: pl.BlockSpec((1, seq, HEAD_DIM), lambda b, h: (c, b, h))
    return pl.pallas_call(
        _hgrn_prompt_kernel,
        grid=(batch, n_heads),
        in_specs=[
            pl.BlockSpec((lb_logits.shape[0], HEAD_DIM), lambda b, h: (0, h)),
            comp(4), comp(5), comp(6), comp(7),
            pl.BlockSpec((1, HEAD_DIM), lambda b, h: (0, h)),
        ],
        out_specs=[
            pl.BlockSpec((seq, HEAD_DIM), lambda b, h: (b, h)),
            pl.BlockSpec((1, 1, HEAD_DIM, HEAD_DIM), lambda b, h: (b, h, 0, 0)),
        ],
        out_shape=[
            jax.ShapeDtypeStruct((batch * seq, n_heads * HEAD_DIM), BF16),
            jax.ShapeDtypeStruct((batch, n_heads, HEAD_DIM, HEAD_DIM), F32),
        ],
        compiler_params=pltpu.CompilerParams(
            dimension_semantics=("arbitrary", "arbitrary"), vmem_limit_bytes=VMEM_LIMIT),
        name="hgrn_prompt",
    )(lb_logits, p3, p3, p3, p3, gain_b)


def _hgrn_decode_kernel(lbl_ref, q_ref, f_ref, i_ref, z_ref, gain_ref, s0_ref, o_ref, s_ref, *, n_valid):
    lb = _lower_bound(lbl_ref)
    o, st = _hgrn_chunks(q_ref[0], f_ref[0], i_ref[0], lb, s0_ref[0, 0].T, SAMPLE_PAD, n_valid=n_valid)
    o_ref[...] = _head_norm_gate(o, gain_ref[...], z_ref[0]).astype(o_ref.dtype)
    s_ref[0, 0] = st.T


def _hgrn_decode(p3s, lb_logits, gain_b, state, n_valid):
    dec_batch, n_heads = state.shape[:2]
    comp = lambda c: pl.BlockSpec((1, SAMPLE_PAD, HEAD_DIM), lambda b, h: (c, b, h))
    return pl.pallas_call(
        functools.partial(_hgrn_decode_kernel, n_valid=n_valid),
        grid=(dec_batch, n_heads),
        in_specs=[
            pl.BlockSpec((lb_logits.shape[0], HEAD_DIM), lambda b, h: (0, h)),
            comp(4), comp(5), comp(6), comp(7),
            pl.BlockSpec((1, HEAD_DIM), lambda b, h: (0, h)),
            pl.BlockSpec((1, 1, HEAD_DIM, HEAD_DIM), lambda b, h: (b, h, 0, 0)),
        ],
        out_specs=[
            pl.BlockSpec((SAMPLE_PAD, HEAD_DIM), lambda b, h: (b, h)),
            pl.BlockSpec((1, 1, HEAD_DIM, HEAD_DIM), lambda b, h: (b, h, 0, 0)),
        ],
        out_shape=[
            jax.ShapeDtypeStruct((dec_batch * SAMPLE_PAD, n_heads * HEAD_DIM), BF16),
            jax.ShapeDtypeStruct(state.shape, F32),
        ],
        compiler_params=pltpu.CompilerParams(dimension_semantics=("arbitrary", "arbitrary")),
        name="hgrn_decode",
    )(lb_logits, p3s, p3s, p3s, p3s, gain_b, state)


def _sb_decode_kernel(pt_ref, qr_ref, bias_ref, kn_ref, vn_ref, *rest, n_valid, n_pg):
    del pt_ref
    kp_refs, vp_refs = rest[:n_pg], rest[n_pg:2 * n_pg]
    mu_ref, z_ref, gain_ref, o_ref, acc_scr, carry_scr = rest[2 * n_pg:]
    j = pl.program_id(1)
    n_heads, n_rows = acc_scr.shape[:2]
    n_q = n_rows // n_heads
    page = kp_refs[0].shape[0] // n_heads
    zscale = HEAD_DIM ** -0.5 * LOG2E
    bias2 = bias_ref[...] * LOG2E
    head = lambda h: slice(h * HEAD_DIM, (h + 1) * HEAD_DIM)

    def scores(k_heads):
        z = sum(_dot_nt(qr_ref[0, h], k_heads[h].astype(BF16)) for h in range(n_heads))
        return z * zscale + bias2[:, :z.shape[1]]

    @pl.when(j == 0)
    def _():
        r = kn_ref.shape[1]
        z = scores([kn_ref[0, :, head(h)] for h in range(n_heads)])
        t_row = lax.broadcasted_iota(jnp.int32, (n_rows, r), 0) % n_q
        s_col = lax.broadcasted_iota(jnp.int32, (n_rows, r), 1)
        valid = (s_col < t_row) & (s_col < n_valid)
        sp = jnp.where(valid, _softplus2(z), 0.0)
        later_mat = jnp.where(lax.broadcasted_iota(jnp.int32, (r, r), 0)
                              > lax.broadcasted_iota(jnp.int32, (r, r), 1), 1.0, 0.0)
        later = _dot(sp, later_mat, precision=lax.Precision.HIGHEST)
        wb = jnp.where(valid, jnp.exp2(z - sp - later), 0.0).astype(BF16)
        carry_scr[...] = jnp.broadcast_to(jnp.sum(sp, axis=1, keepdims=True), carry_scr.shape)
        for h in range(n_heads):
            acc_scr[h] = _dot(wb, vn_ref[0, :, head(h)].astype(BF16))

    zs = [scores([kp_ref[pl.ds(h, page, stride=n_heads), :] for h in range(n_heads)]) for kp_ref in kp_refs]
    sps = [_softplus2(z) for z in zs]
    rs = [_dot(jnp.concatenate(_split_bf16(sp), axis=1), mu_ref[...]) for sp in sps]
    carry = carry_scr[...]
    wbs = []
    for z, sp, r in zip(zs, sps, rs):
        wbs.append(jnp.exp2(z - sp - r[:, :page] - carry).astype(BF16))
        carry = carry + r[:, page:]
    carry_scr[...] = carry
    for h in range(n_heads):
        acc_scr[h] += sum(_dot(wb, vp_ref[pl.ds(h, page, stride=n_heads), :].astype(BF16))
                          for wb, vp_ref in zip(wbs, vp_refs))

    @pl.when(j == pl.num_programs(1) - 1)
    def _():
        rh = lax.broadcasted_iota(jnp.int32, (n_rows, HEAD_DIM), 0) // n_q
        o = jnp.zeros((n_rows, HEAD_DIM), F32)
        for h in range(n_heads):
            o = o + jnp.where(rh == h, acc_scr[h], 0.0)
        o_ref[0] = _head_norm_gate(o, gain_ref[...], z_ref[0]).astype(o_ref.dtype)


def _sb_decode(page_table, q_rows, bias_rows, p3s, cache_k, cache_v, z_r, gain_r, n_valid):
    dec_batch, n_pages = page_table.shape
    page_rows = cache_k.shape[1]
    n_heads, n_rows = q_rows.shape[1:3]
    width = n_heads * HEAD_DIM
    page = page_rows // n_heads
    assert page == HEAD_DIM
    mu = _sb_matrix(page)
    new = lambda c: pl.BlockSpec((1, SAMPLE_PAD, width), lambda b, j, pt: (c, b, 0))
    n_pg = DECODE_PAGES_PER_STEP
    assert n_pages % n_pg == 0
    paged = [pl.BlockSpec((None, page_rows, HEAD_DIM),
                          functools.partial(lambda b, j, pt, i: (pt[b, n_pages - 1 - (j * n_pg + i)], 0, 0), i=i))
             for i in range(n_pg)]
    grid_spec = pltpu.PrefetchScalarGridSpec(
        num_scalar_prefetch=1,
        grid=(dec_batch, n_pages // n_pg),
        in_specs=[
            pl.BlockSpec((1, n_heads, n_rows, HEAD_DIM), lambda b, j, pt: (b, 0, 0, 0)),
            pl.BlockSpec((n_rows, HEAD_DIM), lambda b, j, pt: (0, 0)),
            new(1), new(2), *paged, *paged,
            pl.BlockSpec(mu.shape, lambda b, j, pt: (0, 0)),
            pl.BlockSpec((1, n_rows, HEAD_DIM), lambda b, j, pt: (b, 0, 0)),
            pl.BlockSpec((n_rows, HEAD_DIM), lambda b, j, pt: (0, 0)),
        ],
        out_specs=pl.BlockSpec((1, n_rows, HEAD_DIM), lambda b, j, pt: (b, 0, 0)),
        scratch_shapes=[pltpu.VMEM((n_heads, n_rows, HEAD_DIM), F32), pltpu.VMEM((n_rows, HEAD_DIM), F32)],
    )
    return pl.pallas_call(
        functools.partial(_sb_decode_kernel, n_valid=n_valid, n_pg=n_pg),
        grid_spec=grid_spec,
        out_shape=jax.ShapeDtypeStruct((dec_batch, n_rows, HEAD_DIM), BF16),
        compiler_params=pltpu.CompilerParams(
            dimension_semantics=("arbitrary", "arbitrary"), vmem_limit_bytes=VMEM_LIMIT),
        name="sb_decode",
    )(page_table, q_rows, bias_rows, p3s, p3s, *([cache_k] * n_pg), *([cache_v] * n_pg), mu, z_r, gain_r)


def _out_proj_kernel(x_ref, ma_ref, mb_ref, wa_ref, wb_ref, fw_ref, o_ref):
    y = x_ref[...] + _dot(ma_ref[...], wa_ref[...]) + _dot(mb_ref[...], wb_ref[...])
    ms = jnp.mean(y * y, axis=-1, keepdims=True)
    o_ref[...] = y * lax.rsqrt(ms + EPS) * fw_ref[...]


def _out_proj(x, m_a, m_b, w_bf16, final_w, bm):
    t, d = x.shape
    wa = m_a.shape[1]
    wb = m_b.shape[1]
    assert wa == wb
    return pl.pallas_call(
        _out_proj_kernel,
        grid=(t // bm,),
        in_specs=[
            pl.BlockSpec((bm, d), lambda m: (m, 0)),
            pl.BlockSpec((bm, wa), lambda m: (m, 0)),
            pl.BlockSpec((bm, wb), lambda m: (m, 0)),
            pl.BlockSpec((wa, d), lambda m: (0, 0)),
            pl.BlockSpec((wb, d), lambda m: (1, 0)),
            pl.BlockSpec((1, d), lambda m: (0, 0)),
        ],
        out_specs=pl.BlockSpec((bm, d), lambda m: (m, 0)),
        out_shape=jax.ShapeDtypeStruct((t, d), F32),
        compiler_params=pltpu.CompilerParams(
            dimension_semantics=("arbitrary",), vmem_limit_bytes=VMEM_LIMIT),
        name="out_proj",
    )(x, m_a, m_b, w_bf16, w_bf16, final_w)


def kernel(x_prompt, x_sample, cache_k, cache_v, state_s, page_table, norm_w, w_in, gain_a, gain_b,
           sb_bias, lb_logits, w_out, final_norm_w):
    depth = norm_w.shape[0]
    assert depth == 1
    batch, seq, d_model = x_prompt.shape
    dec_batch, dec_seq, _ = x_sample.shape
    n_heads_a, head_dim = cache_k.shape[3:]
    w_a = n_heads_a * head_dim
    assert head_dim == HEAD_DIM and state_s.shape[3:] == (HEAD_DIM, HEAD_DIM)
    assert dec_seq <= SAMPLE_PAD and seq % SB_TILE == 0 and seq % (HGRN_CHUNK * HGRN_CHUNKS_PER_STEP) == 0

    w_in_b = w_in[0].astype(BF16)
    w_out_b = w_out[0].astype(BF16)
    nw = norm_w[0][None, :]
    fw = final_norm_w[None, :]
    ga = gain_a[0][None, :]
    gb = gain_b[0][None, :]

    xp = x_prompt.reshape(batch * seq, d_model)
    p3 = _in_proj(xp, nw, w_in_b, bm=1024)
    m_a = _sb_prompt(p3, sb_bias, ga, batch, seq)
    m_b, s_p = _hgrn_prompt(p3, lb_logits, gb, batch, seq)
    y_prompt = _out_proj(xp, m_a, m_b, w_out_b, fw, bm=512).reshape(batch, seq, d_model)
    k_p = p3[1].reshape(1, batch, seq, n_heads_a, head_dim)
    v_p = p3[2].reshape(1, batch, seq, n_heads_a, head_dim)

    xs = jnp.pad(x_sample, ((0, 0), (0, SAMPLE_PAD - dec_seq), (0, 0))).reshape(dec_batch * SAMPLE_PAD, d_model)
    p3s = _in_proj(xs, nw, w_in_b, bm=dec_batch * SAMPLE_PAD)
    rows = lambda c: p3s[c].reshape(dec_batch, SAMPLE_PAD, -1)[:, :dec_seq]
    q_t = rows(0).reshape(dec_batch, dec_seq, n_heads_a, head_dim).transpose(0, 2, 1, 3)
    q_rows = (q_t[:, :, None, :, :] * jnp.eye(n_heads_a, dtype=F32)[None, :, :, None, None]).reshape(
        dec_batch, n_heads_a, n_heads_a * dec_seq, head_dim).astype(BF16)
    bias_rows = jnp.broadcast_to(jnp.repeat(sb_bias[0], dec_seq)[:, None], (n_heads_a * dec_seq, head_dim))
    to_rows = lambda a: a.reshape(dec_batch, dec_seq, n_heads_a, head_dim).transpose(0, 2, 1, 3).reshape(
        dec_batch, n_heads_a * dec_seq, head_dim)
    z_r = to_rows(rows(3))
    gain_r = jnp.repeat(gain_a[0].reshape(n_heads_a, head_dim), dec_seq, axis=0)
    ck = cache_k[0].reshape(cache_k.shape[1], cache_k.shape[2] * n_heads_a, head_dim)
    cv = cache_v[0].reshape(cache_v.shape[1], cache_v.shape[2] * n_heads_a, head_dim)
    o_r = _sb_decode(page_table, q_rows, bias_rows, p3s, ck, cv, z_r, gain_r, dec_seq)
    m_a_s = o_r.reshape(dec_batch, n_heads_a, dec_seq, head_dim).transpose(0, 2, 1, 3).reshape(
        dec_batch, dec_seq, w_a)
    m_a_s = jnp.pad(m_a_s, ((0, 0), (0, SAMPLE_PAD - dec_seq), (0, 0))).reshape(dec_batch * SAMPLE_PAD, w_a)
    m_b_s, s_s = _hgrn_decode(p3s, lb_logits, gb, state_s[0], dec_seq)
    y_s = _out_proj(xs, m_a_s, m_b_s, w_out_b, fw, bm=dec_batch * SAMPLE_PAD)
    y_sample = y_s.reshape(dec_batch, SAMPLE_PAD, d_model)[:, :dec_seq]
    k_s = rows(1).reshape(1, dec_batch, dec_seq, n_heads_a, head_dim)
    v_s = rows(2).reshape(1, dec_batch, dec_seq, n_heads_a, head_dim)

    return (y_prompt, y_sample, k_p, v_p, s_p[None], k_s, v_s, s_s[None])
```

```python
import functools

import jax
import jax.numpy as jnp
from jax import lax
from jax.experimental import pallas as pl
from jax.experimental.pallas import tpu as pltpu

EPS = 1e-6
HEAD_DIM = 128
N_COMP = 8
COMP_QA, COMP_KA, COMP_VA, COMP_ZA, COMP_QB, COMP_FB, COMP_IB, COMP_ZB = range(N_COMP)
SB_BLOCK = 128
SB_TILE = 512
LOG2E = 1.4426950408889634
HGRN_CHUNK = 64
HGRN_CHUNKS_PER_STEP = 8
CUMSUM_GROUP = 256
HGRN_SUB = 16
SAMPLE_PAD = 8
DECODE_PAGES_PER_STEP = 8
VMEM_LIMIT = 56 * 1024 * 1024

F32 = jnp.float32
BF16 = jnp.bfloat16


def _dot(a, b, **kw):
    return jnp.dot(a, b, preferred_element_type=F32, **kw)


def _dot_nt(a, b):
    return lax.dot_general(a, b, (((1,), (1,)), ((), ())), preferred_element_type=F32)


def _dot_tn(a, b):
    return lax.dot_general(a, b, (((0,), (0,)), ((), ())), preferred_element_type=F32)


def _sigmoid(x):
    return 1.0 / (1.0 + jnp.exp(-x))


def _softplus2(z2):
    return jnp.maximum(z2, 0.0) + jnp.log2(1.0 + jnp.exp2(-jnp.abs(z2)))


def _split_bf16(x):
    hi = x.astype(BF16)
    lo = (x - hi.astype(F32)).astype(BF16)
    return hi, lo


def _in_proj_kernel(x_ref, nw_ref, w_ref, o_ref, k_ref, v_ref, h_scr):
    n = pl.program_id(1)

    @pl.when(n == 0)
    def _():
        x = x_ref[...]
        ms = jnp.mean(x * x, axis=-1, keepdims=True)
        h_scr[...] = (x * lax.rsqrt(ms + EPS) * nw_ref[...]).astype(BF16)

    res = _dot(h_scr[...], w_ref[...])
    o_ref[0] = res
    k_ref[...] = jnp.where(n <= COMP_KA, res, k_ref[...])
    v_ref[...] = jnp.where(n <= COMP_VA, res, v_ref[...])


def _in_proj(x, norm_w, w_bf16, bm):
    t, d = x.shape
    d_in = w_bf16.shape[1]
    bn = d_in // N_COMP
    return pl.pallas_call(
        _in_proj_kernel,
        grid=(t // bm, N_COMP),
        in_specs=[
            pl.BlockSpec((bm, d), lambda m, n: (m, 0)),
            pl.BlockSpec((1, d), lambda m, n: (0, 0)),
            pl.BlockSpec((d, bn), lambda m, n: (0, n)),
        ],
        out_specs=[
            pl.BlockSpec((1, bm, bn), lambda m, n: (n, m, 0)),
            pl.BlockSpec((bm, bn), lambda m, n: (m, 0)),
            pl.BlockSpec((bm, bn), lambda m, n: (m, 0)),
        ],
        out_shape=[
            jax.ShapeDtypeStruct((N_COMP, t, bn), F32),
            jax.ShapeDtypeStruct((t, bn), F32),
            jax.ShapeDtypeStruct((t, bn), F32),
        ],
        scratch_shapes=[pltpu.VMEM((bm, d), BF16)],
        compiler_params=pltpu.CompilerParams(
            dimension_semantics=("arbitrary", "arbitrary"), vmem_limit_bytes=VMEM_LIMIT),
        name="in_proj",
    )(x, norm_w, w_bf16)


def _head_norm_gate(o, gain, z):
    ms = jnp.mean(o * o, axis=-1, keepdims=True)
    return o * lax.rsqrt(ms + EPS) * gain * (z * _sigmoid(z))


def _sb_prompt_kernel(bias_ref, q_ref, k_ref, v_ref, z_ref, gain_ref, mu_ref, o_ref,
                      kb_scr, vb_scr, acc_scr, carry_scr):
    tq = SB_TILE
    bk = SB_BLOCK
    t_len = q_ref.shape[1]
    bias2 = bias_ref[0, pl.program_id(1)] * LOG2E
    qscale = HEAD_DIM ** -0.5 * LOG2E
    kb_scr[...] = k_ref[0].astype(BF16)
    vb_scr[...] = v_ref[0].astype(BF16)
    gain = gain_ref[...]

    def tile(q, k0, tk, masked):
        keys = pl.ds(pl.multiple_of(k0, tq), tk)
        z = _dot_nt(q, kb_scr[keys, :]) + bias2
        sp = _softplus2(z)
        if masked:
            causal = (lax.broadcasted_iota(jnp.int32, (tq, tk), 1)
                      < lax.broadcasted_iota(jnp.int32, (tq, tk), 0))
            sp = jnp.where(causal, sp, 0.0)
        zs = z - sp
        hi, lo = _split_bf16(sp)
        carry = carry_scr[...]
        ws = [None] * (tk // bk)
        for c in reversed(range(tk // bk)):
            cols = slice(c * bk, (c + 1) * bk)
            r = _dot(jnp.concatenate([hi[:, cols], lo[:, cols]], axis=1), mu_ref[...])
            ws[c] = jnp.exp2(zs[:, cols] - r[:, :bk] - carry)
            carry = carry + r[:, bk:]
        w = jnp.concatenate(ws, axis=1)
        if masked:
            w = jnp.where(causal, w, 0.0)
        carry_scr[...] = carry
        acc_scr[...] += _dot(w.astype(BF16), vb_scr[keys, :])

    def q_tile(qt, _):
        qs = pl.multiple_of(qt * tq, tq)
        q = (q_ref[0, pl.ds(qs, tq), :] * qscale).astype(BF16)
        acc_scr[...] = jnp.zeros_like(acc_scr)
        carry_scr[...] = jnp.zeros_like(carry_scr)
        tile(q, qs, tq, True)

        def k_pair(jj, _):
            tile(q, qs - (jj + 1) * 2 * tq, 2 * tq, False)
            return 0

        lax.fori_loop(0, qt // 2, k_pair, 0)

        @pl.when(qt % 2 == 1)
        def _():
            tile(q, 0, tq, False)

        zg = z_ref[0, pl.ds(qs, tq), :]
        o_ref[pl.ds(qs, tq), :] = _head_norm_gate(acc_scr[...], gain, zg).astype(o_ref.dtype)
        return 0

    lax.fori_loop(0, t_len // tq, q_tile, 0)


def _sb_matrix(n, pieces):
    j = jnp.arange(pieces * n)[:, None] % n
    s = jnp.arange(2 * n)[None, :]
    return jnp.where(s < n, (j > s), True).astype(BF16)


def _sb_prompt(p3, sb_bias, gain_a, batch, seq):
    n_heads = p3.shape[2] // HEAD_DIM
    mu = _sb_matrix(SB_BLOCK, 2)
    comp = lambda c: pl.BlockSpec((1, seq, HEAD_DIM), lambda b, h: (c, b, h))
    return pl.pallas_call(
        _sb_prompt_kernel,
        grid=(batch, n_heads),
        in_specs=[
            pl.BlockSpec(memory_space=pltpu.SMEM),
            comp(COMP_QA), comp(COMP_KA), comp(COMP_VA), comp(COMP_ZA),
            pl.BlockSpec((1, HEAD_DIM), lambda b, h: (0, h)),
            pl.BlockSpec(mu.shape, lambda b, h: (0, 0)),
        ],
        out_specs=pl.BlockSpec((seq, HEAD_DIM), lambda b, h: (b, h)),
        out_shape=jax.ShapeDtypeStruct((batch * seq, n_heads * HEAD_DIM), BF16),
        scratch_shapes=[pltpu.VMEM((seq, HEAD_DIM), BF16), pltpu.VMEM((seq, HEAD_DIM), BF16),
                        pltpu.VMEM((SB_TILE, HEAD_DIM), F32), pltpu.VMEM((SB_TILE, SB_BLOCK), F32)],
        compiler_params=pltpu.CompilerParams(
            dimension_semantics=("arbitrary", "arbitrary"), vmem_limit_bytes=VMEM_LIMIT),
        name="sb_prompt",
    )(sb_bias, p3, p3, p3, p3, gain_a, mu)


def _lower_bound(l):
    e = jnp.exp(l - jnp.max(l, axis=0, keepdims=True))
    return e[0:1, :] / jnp.sum(e, axis=0, keepdims=True)


def _chunk_cumsum(x, c_len):
    rows = x.shape[0]
    group = min(rows, CUMSUM_GROUP)
    ti = lax.broadcasted_iota(jnp.int32, (group, group), 0)
    si = lax.broadcasted_iota(jnp.int32, (group, group), 1)
    tri = jnp.where((si <= ti) & (si // c_len == ti // c_len), 1.0, 0.0).astype(BF16)
    hi = x.astype(BF16)
    rest = x - hi.astype(F32)
    mid = rest.astype(BF16)
    lo = (rest - mid.astype(F32)).astype(BF16)
    pieces = jnp.concatenate([hi, mid, lo], axis=1)
    outs = []
    for g0 in range(0, rows, group):
        r = _dot(tri, pieces[g0:g0 + group])
        outs.append((r[:, :HEAD_DIM] + r[:, HEAD_DIM:2 * HEAD_DIM]) + r[:, 2 * HEAD_DIM:])
    return jnp.concatenate(outs, axis=0)


def _hgrn_chunks(qb, fb, ib, lb, st, c_len, n_valid=None):
    rows = qb.shape[0]
    sub = min(HGRN_SUB, c_len)
    q = qb * _sigmoid(qb)
    g = lb + (1.0 - lb) * _sigmoid(fb)
    kk = 1.0 - g
    lg = jnp.log(g)
    if n_valid is not None:
        live = lax.broadcasted_iota(jnp.int32, (rows, HEAD_DIM), 0) < n_valid
        kk = jnp.where(live, kk, 0.0)
        lg = jnp.where(live, lg, 0.0)
    yield
    b = _chunk_cumsum(lg, c_len)
    vb = ib.astype(BF16)
    chunks = [slice(c * c_len, (c + 1) * c_len) for c in range(rows // c_len)]
    spans, operands = [], []
    for ch in chunks:
        for lo_r in range(ch.start, ch.stop, sub):
            hi_r = lo_r + sub
            m = b[lo_r + sub // 2:lo_r + sub // 2 + 1, :]
            qi = (q[lo_r:hi_r] * jnp.exp(b[lo_r:hi_r] - m)).astype(BF16)
            ki = (kk[ch.start:hi_r] * jnp.exp(m - b[ch.start:hi_r])).astype(BF16)
            spans.append((ch.start, lo_r, hi_r))
            operands.append((qi, ki))
    yield
    atts = [_dot_nt(qi, ki) for qi, ki in operands]
    masked = []
    for att, (c0, lo_r, hi_r) in zip(atts, spans):
        t_pos = lax.broadcasted_iota(jnp.int32, att.shape, 0) + (lo_r - c0)
        s_pos = lax.broadcasted_iota(jnp.int32, att.shape, 1)
        masked.append(jnp.where(s_pos <= t_pos, att, 0.0).astype(BF16))
    yield
    o_intra = jnp.concatenate([_dot(att, vb[c0:hi_r]) for att, (c0, _, hi_r) in zip(masked, spans)], axis=0)
    lasts = [b[ch.stop - 1:ch.stop, :] for ch in chunks]
    decayed = [(kk[ch] * jnp.exp(bl - b[ch])).astype(BF16) for ch, bl in zip(chunks, lasts)]
    yield
    incs = [_dot_tn(vb[ch], kd) for ch, kd in zip(chunks, decayed)]
    states = [st]
    for inc, bl in zip(incs, lasts):
        states.append(states[-1] * jnp.exp(bl) + inc)
    starts = [((q[ch] * jnp.exp(b[ch])).astype(BF16), s.astype(BF16)) for ch, s in zip(chunks, states)]
    yield
    o_inter = [_dot_nt(qe, s) for qe, s in starts]
    return o_intra + jnp.concatenate(o_inter, axis=0), states[-1]


def _in_lockstep(generators):
    values = [None] * len(generators)
    running = dict(enumerate(generators))
    while running:
        for i, gen in list(running.items()):
            try:
                next(gen)
            except StopIteration as done:
                values[i] = done.value
                del running[i]
    return values


def _hgrn_prompt_kernel(lbl_ref, q_ref, f_ref, i_ref, z_ref, gain_ref, o_ref, s_ref):
    r_len = HGRN_CHUNK * HGRN_CHUNKS_PER_STEP
    lb = _lower_bound(lbl_ref[...])
    gain = gain_ref[...]

    def step(ci, st):
        rs = pl.multiple_of(ci * r_len, r_len)
        rows = pl.ds(rs, r_len)
        (o, st), = _in_lockstep(
            [_hgrn_chunks(q_ref[0, rows, :], f_ref[0, rows, :], i_ref[0, rows, :], lb, st, HGRN_CHUNK)])
        o_ref[rows, :] = _head_norm_gate(o, gain, z_ref[0, rows, :]).astype(o_ref.dtype)
        return st

    st = lax.fori_loop(0, q_ref.shape[1] // r_len, step, jnp.zeros((HEAD_DIM, HEAD_DIM), F32))
    s_ref[0, 0] = st.T


def _hgrn_prompt(p3, lb_logits, gain_b, batch, seq):
    n_heads = p3.shape[2] // HEAD_DIM
    comp = lambda c: pl.BlockSpec((1, seq, HEAD_DIM), lambda b, h: (c, b, h))
    return pl.pallas_call(
        _hgrn_prompt_kernel,
        grid=(batch, n_heads),
        in_specs=[
            pl.BlockSpec((lb_logits.shape[0], HEAD_DIM), lambda b, h: (0, h)),
            comp(COMP_QB), comp(COMP_FB), comp(COMP_IB), comp(COMP_ZB),
            pl.BlockSpec((1, HEAD_DIM), lambda b, h: (0, h)),
        ],
        out_specs=[
            pl.BlockSpec((seq, HEAD_DIM), lambda b, h: (b, h)),
            pl.BlockSpec((1, 1, HEAD_DIM, HEAD_DIM), lambda b, h: (b, h, 0, 0)),
        ],
        out_shape=[
            jax.ShapeDtypeStruct((batch * seq, n_heads * HEAD_DIM), BF16),
            jax.ShapeDtypeStruct((batch, n_heads, HEAD_DIM, HEAD_DIM), F32),
        ],
        compiler_params=pltpu.CompilerParams(
            dimension_semantics=("arbitrary", "arbitrary"), vmem_limit_bytes=VMEM_LIMIT),
        name="hgrn_prompt",
    )(lb_logits, p3, p3, p3, p3, gain_b)


def _hgrn_decode_kernel(lbl_ref, q_ref, f_ref, i_ref, z_ref, gain_ref, s0_ref, o_ref, s_ref, *, n_valid):
    n_heads = s0_ref.shape[1]
    head = lambda h: slice(h * HEAD_DIM, (h + 1) * HEAD_DIM)
    lb = _lower_bound(lbl_ref[...])
    results = _in_lockstep([
        _hgrn_chunks(q_ref[0, :, head(h)], f_ref[0, :, head(h)], i_ref[0, :, head(h)], lb[:, head(h)],
                     s0_ref[0, h].T, SAMPLE_PAD, n_valid=n_valid)
        for h in range(n_heads)])
    for h, (o, st) in enumerate(results):
        o_ref[:, head(h)] = _head_norm_gate(o, gain_ref[:, head(h)], z_ref[0, :, head(h)]).astype(o_ref.dtype)
        s_ref[0, h] = st.T


def _hgrn_decode(p3s, lb_logits, gain_b, state, n_valid):
    dec_batch, n_heads = state.shape[:2]
    width = n_heads * HEAD_DIM
    comp = lambda c: pl.BlockSpec((1, SAMPLE_PAD, width), lambda b: (c, b, 0))
    state_spec = pl.BlockSpec((1, n_heads, HEAD_DIM, HEAD_DIM), lambda b: (b, 0, 0, 0))
    return pl.pallas_call(
        functools.partial(_hgrn_decode_kernel, n_valid=n_valid),
        grid=(dec_batch,),
        in_specs=[
            pl.BlockSpec(lb_logits.shape, lambda b: (0, 0)),
            comp(COMP_QB), comp(COMP_FB), comp(COMP_IB), comp(COMP_ZB),
            pl.BlockSpec((1, width), lambda b: (0, 0)),
            state_spec,
        ],
        out_specs=[pl.BlockSpec((SAMPLE_PAD, width), lambda b: (b, 0)), state_spec],
        out_shape=[
            jax.ShapeDtypeStruct((dec_batch * SAMPLE_PAD, width), BF16),
            jax.ShapeDtypeStruct(state.shape, F32),
        ],
        compiler_params=pltpu.CompilerParams(dimension_semantics=("arbitrary",)),
        name="hgrn_decode",
    )(lb_logits, p3s, p3s, p3s, p3s, gain_b, state)


def _sb_decode_kernel(pt_ref, qr_ref, bias_ref, kn_ref, vn_ref, *rest, n_valid, n_pg):
    del pt_ref
    kp_refs, vp_refs = rest[:n_pg], rest[n_pg:2 * n_pg]
    mu_ref, z_ref, gain_ref, o_ref, acc_scr, carry_scr = rest[2 * n_pg:]
    j = pl.program_id(1)
    n_heads, n_rows = acc_scr.shape[:2]
    n_q = n_rows // n_heads
    page = kp_refs[0].shape[0] // n_heads
    zscale = HEAD_DIM ** -0.5 * LOG2E
    bias2 = bias_ref[...] * LOG2E
    head = lambda h: slice(h * HEAD_DIM, (h + 1) * HEAD_DIM)

    def scores(k_heads):
        z = sum(_dot_nt(qr_ref[0, h], k_heads[h].astype(BF16)) for h in range(n_heads))
        return z * zscale + bias2[:, :z.shape[1]]

    @pl.when(j == 0)
    def _():
        r = kn_ref.shape[1]
        z = scores([kn_ref[0, :, head(h)] for h in range(n_heads)])
        t_row = lax.broadcasted_iota(jnp.int32, (n_rows, r), 0) % n_q
        s_col = lax.broadcasted_iota(jnp.int32, (n_rows, r), 1)
        valid = (s_col < t_row) & (s_col < n_valid)
        sp = jnp.where(valid, _softplus2(z), 0.0)
        later_mat = jnp.where(lax.broadcasted_iota(jnp.int32, (r, r), 0)
                              > lax.broadcasted_iota(jnp.int32, (r, r), 1), 1.0, 0.0)
        later = _dot(sp, later_mat, precision=lax.Precision.HIGHEST)
        wb = jnp.where(valid, jnp.exp2(z - sp - later), 0.0).astype(BF16)
        carry_scr[...] = jnp.broadcast_to(jnp.sum(sp, axis=1, keepdims=True), carry_scr.shape)
        for h in range(n_heads):
            acc_scr[h] = _dot(wb, vn_ref[0, :, head(h)].astype(BF16))

    zs = [scores([kp_ref[pl.ds(h, page, stride=n_heads), :] for h in range(n_heads)]) for kp_ref in kp_refs]
    sps = [_softplus2(z) for z in zs]
    rs = [_dot(jnp.concatenate(_split_bf16(sp), axis=1), mu_ref[...]) for sp in sps]
    carry = carry_scr[...]
    wbs = []
    for z, sp, r in zip(zs, sps, rs):
        wbs.append(jnp.exp2(z - sp - r[:, :page] - carry).astype(BF16))
        carry = carry + r[:, page:]
    carry_scr[...] = carry
    for h in range(n_heads):
        acc_scr[h] += sum(_dot(wb, vp_ref[pl.ds(h, page, stride=n_heads), :].astype(BF16))
                          for wb, vp_ref in zip(wbs, vp_refs))

    @pl.when(j == pl.num_programs(1) - 1)
    def _():
        rh = lax.broadcasted_iota(jnp.int32, (n_rows, HEAD_DIM), 0) // n_q
        o = jnp.zeros((n_rows, HEAD_DIM), F32)
        for h in range(n_heads):
            o = o + jnp.where(rh == h, acc_scr[h], 0.0)
        o_ref[0] = _head_norm_gate(o, gain_ref[...], z_ref[0]).astype(o_ref.dtype)


def _sb_decode(page_table, q_rows, bias_rows, p3s, cache_k, cache_v, z_r, gain_r, n_valid):
    dec_batch, n_pages = page_table.shape
    page_rows = cache_k.shape[1]
    n_heads, n_rows = q_rows.shape[1:3]
    width = n_heads * HEAD_DIM
    page = page_rows // n_heads
    assert page == HEAD_DIM
    mu = _sb_matrix(page, 2)
    new = lambda c: pl.BlockSpec((1, SAMPLE_PAD, width), lambda b, j, pt: (c, b, 0))
    n_pg = DECODE_PAGES_PER_STEP
    assert n_pages % n_pg == 0
    paged = [pl.BlockSpec((None, page_rows, HEAD_DIM),
                          functools.partial(lambda b, j, pt, i: (pt[b, n_pages - 1 - (j * n_pg + i)], 0, 0), i=i))
             for i in range(n_pg)]
    grid_spec = pltpu.PrefetchScalarGridSpec(
        num_scalar_prefetch=1,
        grid=(dec_batch, n_pages // n_pg),
        in_specs=[
            pl.BlockSpec((1, n_heads, n_rows, HEAD_DIM), lambda b, j, pt: (b, 0, 0, 0)),
            pl.BlockSpec((n_rows, HEAD_DIM), lambda b, j, pt: (0, 0)),
            new(COMP_KA), new(COMP_VA), *paged, *paged,
            pl.BlockSpec(mu.shape, lambda b, j, pt: (0, 0)),
            pl.BlockSpec((1, n_rows, HEAD_DIM), lambda b, j, pt: (b, 0, 0)),
            pl.BlockSpec((n_rows, HEAD_DIM), lambda b, j, pt: (0, 0)),
        ],
        out_specs=pl.BlockSpec((1, n_rows, HEAD_DIM), lambda b, j, pt: (b, 0, 0)),
        scratch_shapes=[pltpu.VMEM((n_heads, n_rows, HEAD_DIM), F32), pltpu.VMEM((n_rows, HEAD_DIM), F32)],
    )
    return pl.pallas_call(
        functools.partial(_sb_decode_kernel, n_valid=n_valid, n_pg=n_pg),
        grid_spec=grid_spec,
        out_shape=jax.ShapeDtypeStruct((dec_batch, n_rows, HEAD_DIM), BF16),
        compiler_params=pltpu.CompilerParams(
            dimension_semantics=("arbitrary", "arbitrary"), vmem_limit_bytes=VMEM_LIMIT),
        name="sb_decode",
    )(page_table, q_rows, bias_rows, p3s, p3s, *([cache_k] * n_pg), *([cache_v] * n_pg), mu, z_r, gain_r)


def _out_proj_kernel(x_ref, ma_ref, mb_ref, wa_ref, wb_ref, fw_ref, o_ref):
    y = x_ref[...] + _dot(ma_ref[...], wa_ref[...]) + _dot(mb_ref[...], wb_ref[...])
    ms = jnp.mean(y * y, axis=-1, keepdims=True)
    o_ref[...] = y * lax.rsqrt(ms + EPS) * fw_ref[...]


def _out_proj(x, m_a, m_b, w_bf16, final_w, bm):
    t, d = x.shape
    wa = m_a.shape[1]
    wb = m_b.shape[1]
    assert wa == wb
    return pl.pallas_call(
        _out_proj_kernel,
        grid=(t // bm,),
        in_specs=[
            pl.BlockSpec((bm, d), lambda m: (m, 0)),
            pl.BlockSpec((bm, wa), lambda m: (m, 0)),
            pl.BlockSpec((bm, wb), lambda m: (m, 0)),
            pl.BlockSpec((wa, d), lambda m: (0, 0)),
            pl.BlockSpec((wb, d), lambda m: (1, 0)),
            pl.BlockSpec((1, d), lambda m: (0, 0)),
        ],
        out_specs=pl.BlockSpec((bm, d), lambda m: (m, 0)),
        out_shape=jax.ShapeDtypeStruct((t, d), F32),
        compiler_params=pltpu.CompilerParams(
            dimension_semantics=("arbitrary",), vmem_limit_bytes=VMEM_LIMIT),
        name="out_proj",
    )(x, m_a, m_b, w_bf16, w_bf16, final_w)


def kernel(x_prompt, x_sample, cache_k, cache_v, state_s, page_table, norm_w, w_in, gain_a, gain_b,
           sb_bias, lb_logits, w_out, final_norm_w):
    depth = norm_w.shape[0]
    assert depth == 1
    batch, seq, d_model = x_prompt.shape
    dec_batch, dec_seq, _ = x_sample.shape
    n_heads_a, head_dim = cache_k.shape[3:]
    w_a = n_heads_a * head_dim
    assert head_dim == HEAD_DIM and state_s.shape[3:] == (HEAD_DIM, HEAD_DIM)
    assert dec_seq <= SAMPLE_PAD and seq % SB_TILE == 0 and seq % (HGRN_CHUNK * HGRN_CHUNKS_PER_STEP) == 0

    w_in_b = w_in[0].astype(BF16)
    w_out_b = w_out[0].astype(BF16)
    nw = norm_w[0][None, :]
    fw = final_norm_w[None, :]
    ga = gain_a[0][None, :]
    gb = gain_b[0][None, :]

    xp = x_prompt.reshape(batch * seq, d_model)
    p3, k_p, v_p = _in_proj(xp, nw, w_in_b, bm=1024)
    m_a = _sb_prompt(p3, sb_bias, ga, batch, seq)
    m_b, s_p = _hgrn_prompt(p3, lb_logits, gb, batch, seq)
    y_prompt = _out_proj(xp, m_a, m_b, w_out_b, fw, bm=512).reshape(batch, seq, d_model)
    k_p = k_p.reshape(1, batch, seq, n_heads_a, head_dim)
    v_p = v_p.reshape(1, batch, seq, n_heads_a, head_dim)

    xs = jnp.pad(x_sample, ((0, 0), (0, SAMPLE_PAD - dec_seq), (0, 0))).reshape(dec_batch * SAMPLE_PAD, d_model)
    p3s, _, _ = _in_proj(xs, nw, w_in_b, bm=dec_batch * SAMPLE_PAD)
    rows = lambda c: p3s[c].reshape(dec_batch, SAMPLE_PAD, -1)[:, :dec_seq]
    q_t = rows(COMP_QA).reshape(dec_batch, dec_seq, n_heads_a, head_dim).transpose(0, 2, 1, 3)
    q_rows = (q_t[:, :, None, :, :] * jnp.eye(n_heads_a, dtype=F32)[None, :, :, None, None]).reshape(
        dec_batch, n_heads_a, n_heads_a * dec_seq, head_dim).astype(BF16)
    bias_rows = jnp.broadcast_to(jnp.repeat(sb_bias[0], dec_seq)[:, None], (n_heads_a * dec_seq, head_dim))
    to_rows = lambda a: a.reshape(dec_batch, dec_seq, n_heads_a, head_dim).transpose(0, 2, 1, 3).reshape(
        dec_batch, n_heads_a * dec_seq, head_dim)
    z_r = to_rows(rows(COMP_ZA))
    gain_r = jnp.repeat(gain_a[0].reshape(n_heads_a, head_dim), dec_seq, axis=0)
    ck = cache_k[0].reshape(cache_k.shape[1], cache_k.shape[2] * n_heads_a, head_dim)
    cv = cache_v[0].reshape(cache_v.shape[1], cache_v.shape[2] * n_heads_a, head_dim)
    o_r = _sb_decode(page_table, q_rows, bias_rows, p3s, ck, cv, z_r, gain_r, dec_seq)
    m_a_s = o_r.reshape(dec_batch, n_heads_a, dec_seq, head_dim).transpose(0, 2, 1, 3).reshape(
        dec_batch, dec_seq, w_a)
    m_a_s = jnp.pad(m_a_s, ((0, 0), (0, SAMPLE_PAD - dec_seq), (0, 0))).reshape(dec_batch * SAMPLE_PAD, w_a)
    m_b_s, s_s = _hgrn_decode(p3s, lb_logits, gb, state_s[0], dec_seq)
    y_s = _out_proj(xs, m_a_s, m_b_s, w_out_b, fw, bm=dec_batch * SAMPLE_PAD)
    y_sample = y_s.reshape(dec_batch, SAMPLE_PAD, d_model)[:, :dec_seq]
    k_s = rows(COMP_KA).reshape(1, dec_batch, dec_seq, n_heads_a, head_dim)
    v_s = rows(COMP_VA).reshape(1, dec_batch, dec_seq, n_heads_a, head_dim)

    return (y_prompt, y_sample, k_p, v_p, s_p[None], k_s, v_s, s_s[None])
```

```python
import functools

import jax
import jax.numpy as jnp
from jax import lax
from jax.experimental import pallas as pl
from jax.experimental.pallas import tpu as pltpu

EPS = 1e-6
HEAD_DIM = 128
N_COMP = 8
COMP_QA, COMP_KA, COMP_VA, COMP_ZA, COMP_QB, COMP_FB, COMP_IB, COMP_ZB = range(N_COMP)
SB_BLOCK = 128
SB_TILE = 512
LOG2E = 1.4426950408889634
HGRN_CHUNK = 64
HGRN_CHUNKS_PER_STEP = 8
CUMSUM_GROUP = 256
HGRN_SUB = 16
SAMPLE_PAD = 8
DECODE_PAGES_PER_STEP = 8
DECODE_SLOTS = 3
VMEM_LIMIT = 60 * 1024 * 1024

F32 = jnp.float32
BF16 = jnp.bfloat16


def _dot(a, b, **kw):
    return jnp.dot(a, b, preferred_element_type=F32, **kw)


def _dot_nt(a, b):
    return lax.dot_general(a, b, (((1,), (1,)), ((), ())), preferred_element_type=F32)


def _dot_tn(a, b):
    return lax.dot_general(a, b, (((0,), (0,)), ((), ())), preferred_element_type=F32)


def _sigmoid(x):
    return 1.0 / (1.0 + jnp.exp(-x))


def _softplus2(z2):
    return jnp.maximum(z2, 0.0) + jnp.log2(1.0 + jnp.exp2(-jnp.abs(z2)))


def _split_bf16(x):
    hi = x.astype(BF16)
    lo = (x - hi.astype(F32)).astype(BF16)
    return hi, lo


def _in_proj_kernel(x_ref, xs_ref, nw_ref, w_ref, o_ref, k_ref, v_ref, os_ref, h_scr):
    m, n = pl.program_id(0), pl.program_id(1)

    def normed(x):
        ms = jnp.mean(x * x, axis=-1, keepdims=True)
        return (x * lax.rsqrt(ms + EPS) * nw_ref[...]).astype(BF16)

    @pl.when(n == 0)
    def _():
        h_scr[...] = normed(x_ref[...])

    @pl.when(m == 0)
    def _():
        os_ref[0] = _dot(normed(xs_ref[...]), w_ref[...])

    res = _dot(h_scr[...], w_ref[...])
    o_ref[0] = res
    k_ref[...] = jnp.where(n <= COMP_KA, res, k_ref[...])
    v_ref[...] = jnp.where(n <= COMP_VA, res, v_ref[...])


def _in_proj(x, xs, norm_w, w_bf16, bm):
    t, d = x.shape
    ts = xs.shape[0]
    d_in = w_bf16.shape[1]
    bn = d_in // N_COMP
    return pl.pallas_call(
        _in_proj_kernel,
        grid=(t // bm, N_COMP),
        in_specs=[
            pl.BlockSpec((bm, d), lambda m, n: (m, 0)),
            pl.BlockSpec((ts, d), lambda m, n: (0, 0)),
            pl.BlockSpec((1, d), lambda m, n: (0, 0)),
            pl.BlockSpec((d, bn), lambda m, n: (0, n)),
        ],
        out_specs=[
            pl.BlockSpec((1, bm, bn), lambda m, n: (n, m, 0)),
            pl.BlockSpec((bm, bn), lambda m, n: (m, 0)),
            pl.BlockSpec((bm, bn), lambda m, n: (m, 0)),
            pl.BlockSpec((1, ts, bn), lambda m, n: (jnp.where(m == 0, n, N_COMP - 1), 0, 0)),
        ],
        out_shape=[
            jax.ShapeDtypeStruct((N_COMP, t, bn), F32),
            jax.ShapeDtypeStruct((t, bn), F32),
            jax.ShapeDtypeStruct((t, bn), F32),
            jax.ShapeDtypeStruct((N_COMP, ts, bn), F32),
        ],
        scratch_shapes=[pltpu.VMEM((bm, d), BF16)],
        compiler_params=pltpu.CompilerParams(
            dimension_semantics=("arbitrary", "arbitrary"), vmem_limit_bytes=VMEM_LIMIT),
        name="in_proj",
    )(x, xs, norm_w, w_bf16)


def _head_norm_gate(o, gain, z):
    ms = jnp.mean(o * o, axis=-1, keepdims=True)
    return o * lax.rsqrt(ms + EPS) * gain * (z * _sigmoid(z))


def _sb_prompt_kernel(bias_ref, q_ref, k_ref, v_ref, z_ref, gain_ref, mu_ref, o_ref,
                      kb_scr, vb_scr, acc_scr, carry_scr):
    tq = SB_TILE
    bk = SB_BLOCK
    t_len = q_ref.shape[1]
    bias2 = bias_ref[0, pl.program_id(1)] * LOG2E
    qscale = HEAD_DIM ** -0.5 * LOG2E
    kb_scr[...] = k_ref[0].astype(BF16)
    vb_scr[...] = v_ref[0].astype(BF16)
    gain = gain_ref[...]

    def tile(q, k0, tk, masked):
        keys = pl.ds(pl.multiple_of(k0, tq), tk)
        z = _dot_nt(q, kb_scr[keys, :]) + bias2
        sp = _softplus2(z)
        if masked:
            causal = (lax.broadcasted_iota(jnp.int32, (tq, tk), 1)
                      < lax.broadcasted_iota(jnp.int32, (tq, tk), 0))
            sp = jnp.where(causal, sp, 0.0)
        zs = z - sp
        hi, lo = _split_bf16(sp)
        carry = carry_scr[...]
        ws = [None] * (tk // bk)
        for c in reversed(range(tk // bk)):
            cols = slice(c * bk, (c + 1) * bk)
            r = _dot(jnp.concatenate([hi[:, cols], lo[:, cols]], axis=1), mu_ref[...])
            ws[c] = jnp.exp2(zs[:, cols] - r[:, :bk] - carry)
            carry = carry + r[:, bk:]
        w = jnp.concatenate(ws, axis=1)
        if masked:
            w = jnp.where(causal, w, 0.0)
        carry_scr[...] = carry
        acc_scr[...] += _dot(w.astype(BF16), vb_scr[keys, :])

    def q_tile(qt, _):
        qs = pl.multiple_of(qt * tq, tq)
        q = (q_ref[0, pl.ds(qs, tq), :] * qscale).astype(BF16)
        acc_scr[...] = jnp.zeros_like(acc_scr)
        carry_scr[...] = jnp.zeros_like(carry_scr)
        tile(q, qs, tq, True)

        def k_pair(jj, _):
            tile(q, qs - (jj + 1) * 2 * tq, 2 * tq, False)
            return 0

        lax.fori_loop(0, qt // 2, k_pair, 0)

        @pl.when(qt % 2 == 1)
        def _():
            tile(q, 0, tq, False)

        zg = z_ref[0, pl.ds(qs, tq), :]
        o_ref[pl.ds(qs, tq), :] = _head_norm_gate(acc_scr[...], gain, zg).astype(o_ref.dtype)
        return 0

    lax.fori_loop(0, t_len // tq, q_tile, 0)


def _sb_matrix(n, pieces):
    j = jnp.arange(pieces * n)[:, None] % n
    s = jnp.arange(2 * n)[None, :]
    return jnp.where(s < n, (j > s), True).astype(BF16)


def _sb_prompt(p3, sb_bias, gain_a, batch, seq):
    n_heads = p3.shape[2] // HEAD_DIM
    mu = _sb_matrix(SB_BLOCK, 2)
    comp = lambda c: pl.BlockSpec((1, seq, HEAD_DIM), lambda b, h: (c, b, h))
    return pl.pallas_call(
        _sb_prompt_kernel,
        grid=(batch, n_heads),
        in_specs=[
            pl.BlockSpec(memory_space=pltpu.SMEM),
            comp(COMP_QA), comp(COMP_KA), comp(COMP_VA), comp(COMP_ZA),
            pl.BlockSpec((1, HEAD_DIM), lambda b, h: (0, h)),
            pl.BlockSpec(mu.shape, lambda b, h: (0, 0)),
        ],
        out_specs=pl.BlockSpec((seq, HEAD_DIM), lambda b, h: (b, h)),
        out_shape=jax.ShapeDtypeStruct((batch * seq, n_heads * HEAD_DIM), BF16),
        scratch_shapes=[pltpu.VMEM((seq, HEAD_DIM), BF16), pltpu.VMEM((seq, HEAD_DIM), BF16),
                        pltpu.VMEM((SB_TILE, HEAD_DIM), F32), pltpu.VMEM((SB_TILE, SB_BLOCK), F32)],
        compiler_params=pltpu.CompilerParams(
            dimension_semantics=("arbitrary", "arbitrary"), vmem_limit_bytes=VMEM_LIMIT),
        name="sb_prompt",
    )(sb_bias, p3, p3, p3, p3, gain_a, mu)


def _lower_bound(l):
    e = jnp.exp(l - jnp.max(l, axis=0, keepdims=True))
    return e[0:1, :] / jnp.sum(e, axis=0, keepdims=True)


def _chunk_cumsum(x, c_len):
    rows = x.shape[0]
    group = min(rows, CUMSUM_GROUP)
    ti = lax.broadcasted_iota(jnp.int32, (group, group), 0)
    si = lax.broadcasted_iota(jnp.int32, (group, group), 1)
    tri = jnp.where((si <= ti) & (si // c_len == ti // c_len), 1.0, 0.0).astype(BF16)
    hi = x.astype(BF16)
    rest = x - hi.astype(F32)
    mid = rest.astype(BF16)
    lo = (rest - mid.astype(F32)).astype(BF16)
    pieces = jnp.concatenate([hi, mid, lo], axis=1)
    outs = []
    for g0 in range(0, rows, group):
        r = _dot(tri, pieces[g0:g0 + group])
        outs.append((r[:, :HEAD_DIM] + r[:, HEAD_DIM:2 * HEAD_DIM]) + r[:, 2 * HEAD_DIM:])
    return jnp.concatenate(outs, axis=0)


def _hgrn_chunks(qb, fb, ib, lb, st, c_len, n_valid=None):
    rows = qb.shape[0]
    sub = min(HGRN_SUB, c_len)
    q = qb * _sigmoid(qb)
    g = lb + (1.0 - lb) * _sigmoid(fb)
    kk = 1.0 - g
    lg = jnp.log(g)
    if n_valid is not None:
        live = lax.broadcasted_iota(jnp.int32, (rows, HEAD_DIM), 0) < n_valid
        kk = jnp.where(live, kk, 0.0)
        lg = jnp.where(live, lg, 0.0)
    yield
    b = _chunk_cumsum(lg, c_len)
    vb = ib.astype(BF16)
    chunks = [slice(c * c_len, (c + 1) * c_len) for c in range(rows // c_len)]
    spans, operands = [], []
    for ch in chunks:
        for lo_r in range(ch.start, ch.stop, sub):
            hi_r = lo_r + sub
            m = b[lo_r + sub // 2:lo_r + sub // 2 + 1, :]
            qi = (q[lo_r:hi_r] * jnp.exp(b[lo_r:hi_r] - m)).astype(BF16)
            ki = (kk[ch.start:hi_r] * jnp.exp(m - b[ch.start:hi_r])).astype(BF16)
            spans.append((ch.start, lo_r, hi_r))
            operands.append((qi, ki))
    yield
    atts = [_dot_nt(qi, ki) for qi, ki in operands]
    masked = []
    for att, (c0, lo_r, hi_r) in zip(atts, spans):
        t_pos = lax.broadcasted_iota(jnp.int32, att.shape, 0) + (lo_r - c0)
        s_pos = lax.broadcasted_iota(jnp.int32, att.shape, 1)
        masked.append(jnp.where(s_pos <= t_pos, att, 0.0).astype(BF16))
    yield
    o_intra = jnp.concatenate([_dot(att, vb[c0:hi_r]) for att, (c0, _, hi_r) in zip(masked, spans)], axis=0)
    lasts = [b[ch.stop - 1:ch.stop, :] for ch in chunks]
    decayed = [(kk[ch] * jnp.exp(bl - b[ch])).astype(BF16) for ch, bl in zip(chunks, lasts)]
    yield
    incs = [_dot_tn(vb[ch], kd) for ch, kd in zip(chunks, decayed)]
    states = [st]
    for inc, bl in zip(incs, lasts):
        states.append(states[-1] * jnp.exp(bl) + inc)
    starts = [((q[ch] * jnp.exp(b[ch])).astype(BF16), s.astype(BF16)) for ch, s in zip(chunks, states)]
    yield
    o_inter = [_dot_nt(qe, s) for qe, s in starts]
    return o_intra + jnp.concatenate(o_inter, axis=0), states[-1]


def _in_lockstep(generators):
    values = [None] * len(generators)
    running = dict(enumerate(generators))
    while running:
        for i, gen in list(running.items()):
            try:
                next(gen)
            except StopIteration as done:
                values[i] = done.value
                del running[i]
    return values


def _hgrn_prompt_kernel(lbl_ref, q_ref, f_ref, i_ref, z_ref, gain_ref, o_ref, s_ref):
    r_len = HGRN_CHUNK * HGRN_CHUNKS_PER_STEP
    lb = _lower_bound(lbl_ref[...])
    gain = gain_ref[...]

    def step(ci, st):
        rs = pl.multiple_of(ci * r_len, r_len)
        rows = pl.ds(rs, r_len)
        (o, st), = _in_lockstep(
            [_hgrn_chunks(q_ref[0, rows, :], f_ref[0, rows, :], i_ref[0, rows, :], lb, st, HGRN_CHUNK)])
        o_ref[rows, :] = _head_norm_gate(o, gain, z_ref[0, rows, :]).astype(o_ref.dtype)
        return st

    st = lax.fori_loop(0, q_ref.shape[1] // r_len, step, jnp.zeros((HEAD_DIM, HEAD_DIM), F32))
    s_ref[0, 0] = st.T


def _hgrn_prompt(p3, lb_logits, gain_b, batch, seq):
    n_heads = p3.shape[2] // HEAD_DIM
    comp = lambda c: pl.BlockSpec((1, seq, HEAD_DIM), lambda b, h: (c, b, h))
    return pl.pallas_call(
        _hgrn_prompt_kernel,
        grid=(batch, n_heads),
        in_specs=[
            pl.BlockSpec((lb_logits.shape[0], HEAD_DIM), lambda b, h: (0, h)),
            comp(COMP_QB), comp(COMP_FB), comp(COMP_IB), comp(COMP_ZB),
            pl.BlockSpec((1, HEAD_DIM), lambda b, h: (0, h)),
        ],
        out_specs=[
            pl.BlockSpec((seq, HEAD_DIM), lambda b, h: (b, h)),
            pl.BlockSpec((1, 1, HEAD_DIM, HEAD_DIM), lambda b, h: (b, h, 0, 0)),
        ],
        out_shape=[
            jax.ShapeDtypeStruct((batch * seq, n_heads * HEAD_DIM), BF16),
            jax.ShapeDtypeStruct((batch, n_heads, HEAD_DIM, HEAD_DIM), F32),
        ],
        compiler_params=pltpu.CompilerParams(
            dimension_semantics=("arbitrary", "arbitrary"), vmem_limit_bytes=VMEM_LIMIT),
        name="hgrn_prompt",
    )(lb_logits, p3, p3, p3, p3, gain_b)


def _hgrn_decode_kernel(lbl_ref, q_ref, f_ref, i_ref, z_ref, gain_ref, s0_ref, o_ref, s_ref, *, n_valid):
    n_heads = s0_ref.shape[1]
    head = lambda h: slice(h * HEAD_DIM, (h + 1) * HEAD_DIM)
    lb = _lower_bound(lbl_ref[...])
    results = _in_lockstep([
        _hgrn_chunks(q_ref[0, :, head(h)], f_ref[0, :, head(h)], i_ref[0, :, head(h)], lb[:, head(h)],
                     s0_ref[0, h].T, SAMPLE_PAD, n_valid=n_valid)
        for h in range(n_heads)])
    for h, (o, st) in enumerate(results):
        o_ref[:, head(h)] = _head_norm_gate(o, gain_ref[:, head(h)], z_ref[0, :, head(h)]).astype(o_ref.dtype)
        s_ref[0, h] = st.T


def _hgrn_decode(p3s, lb_logits, gain_b, state, n_valid):
    dec_batch, n_heads = state.shape[:2]
    width = n_heads * HEAD_DIM
    comp = lambda c: pl.BlockSpec((1, SAMPLE_PAD, width), lambda b: (c, b, 0))
    state_spec = pl.BlockSpec((1, n_heads, HEAD_DIM, HEAD_DIM), lambda b: (b, 0, 0, 0))
    return pl.pallas_call(
        functools.partial(_hgrn_decode_kernel, n_valid=n_valid),
        grid=(dec_batch,),
        in_specs=[
            pl.BlockSpec(lb_logits.shape, lambda b: (0, 0)),
            comp(COMP_QB), comp(COMP_FB), comp(COMP_IB), comp(COMP_ZB),
            pl.BlockSpec((1, width), lambda b: (0, 0)),
            state_spec,
        ],
        out_specs=[pl.BlockSpec((SAMPLE_PAD, width), lambda b: (b, 0)), state_spec],
        out_shape=[
            jax.ShapeDtypeStruct((dec_batch * SAMPLE_PAD, width), BF16),
            jax.ShapeDtypeStruct(state.shape, F32),
        ],
        compiler_params=pltpu.CompilerParams(dimension_semantics=("arbitrary",)),
        name="hgrn_decode",
    )(lb_logits, p3s, p3s, p3s, p3s, gain_b, state)


def _sb_decode_kernel(pt_ref, qr_ref, bias_ref, kn_ref, vn_ref, *rest, n_valid, n_pg):
    ck_hbm, cv_hbm, mu_ref, z_ref, gain_ref, o_ref, acc_scr, carry_scr, kbuf, vbuf, sems = rest
    j = pl.program_id(1)
    n_steps = pl.num_programs(1)
    step = pl.program_id(0) * n_steps + j
    n_slots = kbuf.shape[0]
    n_heads, n_rows = acc_scr.shape[:2]
    n_q = n_rows // n_heads
    page = kbuf.shape[2] // n_heads

    def page_copies(t):
        seq, js, slot = t // n_steps, t % n_steps, t % n_slots
        copies = []
        for i in range(n_pg):
            src = pt_ref[seq, n_steps * n_pg - 1 - (js * n_pg + i)]
            copies.append(pltpu.make_async_copy(ck_hbm.at[src], kbuf.at[slot, i], sems.at[slot, 0, i]))
            copies.append(pltpu.make_async_copy(cv_hbm.at[src], vbuf.at[slot, i], sems.at[slot, 1, i]))
        return copies

    @pl.when(step == 0)
    def _():
        for t in range(n_slots - 1):
            for copy in page_copies(t):
                copy.start()

    @pl.when(step + n_slots - 1 < pl.num_programs(0) * n_steps)
    def _():
        for copy in page_copies(step + n_slots - 1):
            copy.start()

    zscale = HEAD_DIM ** -0.5 * LOG2E
    bias2 = bias_ref[...] * LOG2E
    head = lambda h: slice(h * HEAD_DIM, (h + 1) * HEAD_DIM)

    def scores(k_heads):
        z = sum(_dot_nt(qr_ref[0, h], k_heads[h].astype(BF16)) for h in range(n_heads))
        return z * zscale + bias2[:, :z.shape[1]]

    @pl.when(j == 0)
    def _():
        r = kn_ref.shape[1]
        z = scores([kn_ref[0, :, head(h)] for h in range(n_heads)])
        t_row = lax.broadcasted_iota(jnp.int32, (n_rows, r), 0) % n_q
        s_col = lax.broadcasted_iota(jnp.int32, (n_rows, r), 1)
        valid = (s_col < t_row) & (s_col < n_valid)
        sp = jnp.where(valid, _softplus2(z), 0.0)
        later_mat = jnp.where(lax.broadcasted_iota(jnp.int32, (r, r), 0)
                              > lax.broadcasted_iota(jnp.int32, (r, r), 1), 1.0, 0.0)
        later = _dot(sp, later_mat, precision=lax.Precision.HIGHEST)
        wb = jnp.where(valid, jnp.exp2(z - sp - later), 0.0).astype(BF16)
        carry_scr[...] = jnp.broadcast_to(jnp.sum(sp, axis=1, keepdims=True), carry_scr.shape)
        for h in range(n_heads):
            acc_scr[h] = _dot(wb, vn_ref[0, :, head(h)].astype(BF16))

    for copy in page_copies(step):
        copy.wait()
    slot = step % n_slots
    kp_refs = [kbuf.at[slot, i] for i in range(n_pg)]
    vp_refs = [vbuf.at[slot, i] for i in range(n_pg)]
    zs = [scores([kp_ref[pl.ds(h, page, stride=n_heads), :] for h in range(n_heads)]) for kp_ref in kp_refs]
    sps = [_softplus2(z) for z in zs]
    rs = [_dot(jnp.concatenate(_split_bf16(sp), axis=1), mu_ref[...]) for sp in sps]
    carry = carry_scr[...]
    wbs = []
    for z, sp, r in zip(zs, sps, rs):
        wbs.append(jnp.exp2(z - sp - r[:, :page] - carry).astype(BF16))
        carry = carry + r[:, page:]
    carry_scr[...] = carry
    for h in range(n_heads):
        acc_scr[h] += sum(_dot(wb, vp_ref[pl.ds(h, page, stride=n_heads), :].astype(BF16))
                          for wb, vp_ref in zip(wbs, vp_refs))

    @pl.when(j == pl.num_programs(1) - 1)
    def _():
        rh = lax.broadcasted_iota(jnp.int32, (n_rows, HEAD_DIM), 0) // n_q
        o = jnp.zeros((n_rows, HEAD_DIM), F32)
        for h in range(n_heads):
            o = o + jnp.where(rh == h, acc_scr[h], 0.0)
        o_ref[0] = _head_norm_gate(o, gain_ref[...], z_ref[0]).astype(o_ref.dtype)


def _sb_decode(page_table, q_rows, bias_rows, p3s, cache_k, cache_v, z_r, gain_r, n_valid):
    dec_batch, n_pages = page_table.shape
    page_rows = cache_k.shape[1]
    n_heads, n_rows = q_rows.shape[1:3]
    width = n_heads * HEAD_DIM
    page = page_rows // n_heads
    assert page == HEAD_DIM
    mu = _sb_matrix(page, 2)
    new = lambda c: pl.BlockSpec((1, SAMPLE_PAD, width), lambda b, j, pt: (c, b, 0))
    n_pg = DECODE_PAGES_PER_STEP
    assert n_pages % n_pg == 0 and dec_batch * (n_pages // n_pg) >= DECODE_SLOTS - 1
    page_buf = pltpu.VMEM((DECODE_SLOTS, n_pg, page_rows, HEAD_DIM), cache_k.dtype)
    grid_spec = pltpu.PrefetchScalarGridSpec(
        num_scalar_prefetch=1,
        grid=(dec_batch, n_pages // n_pg),
        in_specs=[
            pl.BlockSpec((1, n_heads, n_rows, HEAD_DIM), lambda b, j, pt: (b, 0, 0, 0)),
            pl.BlockSpec((n_rows, HEAD_DIM), lambda b, j, pt: (0, 0)),
            new(COMP_KA), new(COMP_VA),
            pl.BlockSpec(memory_space=pl.ANY), pl.BlockSpec(memory_space=pl.ANY),
            pl.BlockSpec(mu.shape, lambda b, j, pt: (0, 0)),
            pl.BlockSpec((1, n_rows, HEAD_DIM), lambda b, j, pt: (b, 0, 0)),
            pl.BlockSpec((n_rows, HEAD_DIM), lambda b, j, pt: (0, 0)),
        ],
        out_specs=pl.BlockSpec((1, n_rows, HEAD_DIM), lambda b, j, pt: (b, 0, 0)),
        scratch_shapes=[pltpu.VMEM((n_heads, n_rows, HEAD_DIM), F32), pltpu.VMEM((n_rows, HEAD_DIM), F32),
                        page_buf, page_buf, pltpu.SemaphoreType.DMA((DECODE_SLOTS, 2, n_pg))],
    )
    return pl.pallas_call(
        functools.partial(_sb_decode_kernel, n_valid=n_valid, n_pg=n_pg),
        grid_spec=grid_spec,
        out_shape=jax.ShapeDtypeStruct((dec_batch, n_rows, HEAD_DIM), BF16),
        compiler_params=pltpu.CompilerParams(
            dimension_semantics=("arbitrary", "arbitrary"), vmem_limit_bytes=VMEM_LIMIT),
        name="sb_decode",
    )(page_table, q_rows, bias_rows, p3s, p3s, cache_k, cache_v, mu, z_r, gain_r)


def _out_proj_kernel(x_ref, ma_ref, mb_ref, wa_ref, wb_ref, fw_ref, o_ref):
    y = x_ref[...] + _dot(ma_ref[...], wa_ref[...]) + _dot(mb_ref[...], wb_ref[...])
    ms = jnp.mean(y * y, axis=-1, keepdims=True)
    o_ref[...] = y * lax.rsqrt(ms + EPS) * fw_ref[...]


def _out_proj(x, m_a, m_b, w_bf16, final_w, bm):
    t, d = x.shape
    wa = m_a.shape[1]
    wb = m_b.shape[1]
    assert wa == wb
    return pl.pallas_call(
        _out_proj_kernel,
        grid=(t // bm,),
        in_specs=[
            pl.BlockSpec((bm, d), lambda m: (m, 0)),
            pl.BlockSpec((bm, wa), lambda m: (m, 0)),
            pl.BlockSpec((bm, wb), lambda m: (m, 0)),
            pl.BlockSpec((wa, d), lambda m: (0, 0)),
            pl.BlockSpec((wb, d), lambda m: (1, 0)),
            pl.BlockSpec((1, d), lambda m: (0, 0)),
        ],
        out_specs=pl.BlockSpec((bm, d), lambda m: (m, 0)),
        out_shape=jax.ShapeDtypeStruct((t, d), F32),
        compiler_params=pltpu.CompilerParams(
            dimension_semantics=("arbitrary",), vmem_limit_bytes=VMEM_LIMIT),
        name="out_proj",
    )(x, m_a, m_b, w_bf16, w_bf16, final_w)


def kernel(x_prompt, x_sample, cache_k, cache_v, state_s, page_table, norm_w, w_in, gain_a, gain_b,
           sb_bias, lb_logits, w_out, final_norm_w):
    depth = norm_w.shape[0]
    assert depth == 1
    batch, seq, d_model = x_prompt.shape
    dec_batch, dec_seq, _ = x_sample.shape
    n_heads_a, head_dim = cache_k.shape[3:]
    w_a = n_heads_a * head_dim
    assert head_dim == HEAD_DIM and state_s.shape[3:] == (HEAD_DIM, HEAD_DIM)
    assert dec_seq <= SAMPLE_PAD and seq % SB_TILE == 0 and seq % (HGRN_CHUNK * HGRN_CHUNKS_PER_STEP) == 0

    w_in_b = w_in[0].astype(BF16)
    w_out_b = w_out[0].astype(BF16)
    nw = norm_w[0][None, :]
    fw = final_norm_w[None, :]
    ga = gain_a[0][None, :]
    gb = gain_b[0][None, :]

    xp = x_prompt.reshape(batch * seq, d_model)
    xs = jnp.pad(x_sample, ((0, 0), (0, SAMPLE_PAD - dec_seq), (0, 0))).reshape(dec_batch * SAMPLE_PAD, d_model)
    p3, k_p, v_p, p3s = _in_proj(xp, xs, nw, w_in_b, bm=1024)
    m_a = _sb_prompt(p3, sb_bias, ga, batch, seq)
    m_b, s_p = _hgrn_prompt(p3, lb_logits, gb, batch, seq)
    y_prompt = _out_proj(xp, m_a, m_b, w_out_b, fw, bm=512).reshape(batch, seq, d_model)
    k_p = k_p.reshape(1, batch, seq, n_heads_a, head_dim)
    v_p = v_p.reshape(1, batch, seq, n_heads_a, head_dim)

    rows = lambda c: p3s[c].reshape(dec_batch, SAMPLE_PAD, -1)[:, :dec_seq]
    q_t = rows(COMP_QA).reshape(dec_batch, dec_seq, n_heads_a, head_dim).transpose(0, 2, 1, 3)
    q_rows = (q_t[:, :, None, :, :] * jnp.eye(n_heads_a, dtype=F32)[None, :, :, None, None]).reshape(
        dec_batch, n_heads_a, n_heads_a * dec_seq, head_dim).astype(BF16)
    bias_rows = jnp.broadcast_to(jnp.repeat(sb_bias[0], dec_seq)[:, None], (n_heads_a * dec_seq, head_dim))
    to_rows = lambda a: a.reshape(dec_batch, dec_seq, n_heads_a, head_dim).transpose(0, 2, 1, 3).reshape(
        dec_batch, n_heads_a * dec_seq, head_dim)
    z_r = to_rows(rows(COMP_ZA))
    gain_r = jnp.repeat(gain_a[0].reshape(n_heads_a, head_dim), dec_seq, axis=0)
    ck = cache_k[0].reshape(cache_k.shape[1], cache_k.shape[2] * n_heads_a, head_dim)
    cv = cache_v[0].reshape(cache_v.shape[1], cache_v.shape[2] * n_heads_a, head_dim)
    o_r = _sb_decode(page_table, q_rows, bias_rows, p3s, ck, cv, z_r, gain_r, dec_seq)
    m_a_s = o_r.reshape(dec_batch, n_heads_a, dec_seq, head_dim).transpose(0, 2, 1, 3).reshape(
        dec_batch, dec_seq, w_a)
    m_a_s = jnp.pad(m_a_s, ((0, 0), (0, SAMPLE_PAD - dec_seq), (0, 0))).reshape(dec_batch * SAMPLE_PAD, w_a)
    m_b_s, s_s = _hgrn_decode(p3s, lb_logits, gb, state_s[0], dec_seq)
    y_s = _out_proj(xs, m_a_s, m_b_s, w_out_b, fw, bm=dec_batch * SAMPLE_PAD)
    y_sample = y_s.reshape(dec_batch, SAMPLE_PAD, d_model)[:, :dec_seq]
    k_s = rows(COMP_KA).reshape(1, dec_batch, dec_seq, n_heads_a, head_dim)
    v_s = rows(COMP_VA).reshape(1, dec_batch, dec_seq, n_heads_a, head_dim)

    return (y_prompt, y_sample, k_p, v_p, s_p[None], k_s, v_s, s_s[None])
```

```python
import functools

import jax
import jax.numpy as jnp
from jax import lax
from jax.experimental import pallas as pl
from jax.experimental.pallas import tpu as pltpu

EPS = 1e-6
HEAD_DIM = 128
N_COMP = 8
COMP_QA, COMP_KA, COMP_VA, COMP_ZA, COMP_QB, COMP_FB, COMP_IB, COMP_ZB = range(N_COMP)
SB_BLOCK = 128
SB_TILE = 512
LOG2E = 1.4426950408889634
HGRN_CHUNK = 64
HGRN_CHUNKS_PER_STEP = 8
CUMSUM_GROUP = 256
HGRN_SUB = 16
SAMPLE_PAD = 8
DECODE_PAGES_PER_STEP = 8
DECODE_SLOTS = 3
VMEM_LIMIT = 60 * 1024 * 1024

F32 = jnp.float32
BF16 = jnp.bfloat16


def _dot(a, b, **kw):
    return jnp.dot(a, b, preferred_element_type=F32, **kw)


def _dot_nt(a, b):
    return lax.dot_general(a, b, (((1,), (1,)), ((), ())), preferred_element_type=F32)


def _dot_tn(a, b):
    return lax.dot_general(a, b, (((0,), (0,)), ((), ())), preferred_element_type=F32)


def _sigmoid(x):
    return 1.0 / (1.0 + jnp.exp(-x))


def _softplus2(z2):
    return jnp.maximum(z2, 0.0) + jnp.log2(1.0 + jnp.exp2(-jnp.abs(z2)))


def _split_bf16(x):
    hi = x.astype(BF16)
    lo = (x - hi.astype(F32)).astype(BF16)
    return hi, lo


def _in_proj_kernel(x_ref, xs_ref, nw_ref, w_ref, o_ref, k_ref, v_ref, os_ref, h_scr):
    m, n = pl.program_id(0), pl.program_id(1)

    def normed(x):
        ms = jnp.mean(x * x, axis=-1, keepdims=True)
        return (x * lax.rsqrt(ms + EPS) * nw_ref[...]).astype(BF16)

    @pl.when(n == 0)
    def _():
        h_scr[...] = normed(x_ref[...])

    @pl.when(m == 0)
    def _():
        os_ref[0] = _dot(normed(xs_ref[...]), w_ref[...])

    res = _dot(h_scr[...], w_ref[...])
    o_ref[0] = res
    k_ref[...] = jnp.where(n <= COMP_KA, res, k_ref[...])
    v_ref[...] = jnp.where(n <= COMP_VA, res, v_ref[...])


def _in_proj(x, xs, norm_w, w_bf16, bm):
    t, d = x.shape
    ts = xs.shape[0]
    d_in = w_bf16.shape[1]
    bn = d_in // N_COMP
    return pl.pallas_call(
        _in_proj_kernel,
        grid=(t // bm, N_COMP),
        in_specs=[
            pl.BlockSpec((bm, d), lambda m, n: (m, 0)),
            pl.BlockSpec((ts, d), lambda m, n: (0, 0)),
            pl.BlockSpec((1, d), lambda m, n: (0, 0)),
            pl.BlockSpec((d, bn), lambda m, n: (0, n)),
        ],
        out_specs=[
            pl.BlockSpec((1, bm, bn), lambda m, n: (n, m, 0)),
            pl.BlockSpec((bm, bn), lambda m, n: (m, 0)),
            pl.BlockSpec((bm, bn), lambda m, n: (m, 0)),
            pl.BlockSpec((1, ts, bn), lambda m, n: (jnp.where(m == 0, n, N_COMP - 1), 0, 0)),
        ],
        out_shape=[
            jax.ShapeDtypeStruct((N_COMP, t, bn), F32),
            jax.ShapeDtypeStruct((t, bn), F32),
            jax.ShapeDtypeStruct((t, bn), F32),
            jax.ShapeDtypeStruct((N_COMP, ts, bn), F32),
        ],
        scratch_shapes=[pltpu.VMEM((bm, d), BF16)],
        compiler_params=pltpu.CompilerParams(
            dimension_semantics=("arbitrary", "arbitrary"), vmem_limit_bytes=VMEM_LIMIT),
        name="in_proj",
    )(x, xs, norm_w, w_bf16)


def _head_norm_gate(o, gain, z):
    ms = jnp.mean(o * o, axis=-1, keepdims=True)
    return o * lax.rsqrt(ms + EPS) * gain * (z * _sigmoid(z))


def _sb_prompt_kernel(bias_ref, q_ref, k_ref, v_ref, z_ref, gain_ref, mu_ref, o_ref,
                      kb_scr, vb_scr, acc_scr, carry_scr):
    tq = SB_TILE
    bk = SB_BLOCK
    t_len = q_ref.shape[1]
    bias2 = bias_ref[0, pl.program_id(1)] * LOG2E
    qscale = HEAD_DIM ** -0.5 * LOG2E
    kb_scr[...] = k_ref[0].astype(BF16)
    vb_scr[...] = v_ref[0].astype(BF16)
    gain = gain_ref[...]

    def tile(q, k0, tk, masked):
        keys = pl.ds(pl.multiple_of(k0, tq), tk)
        z = _dot_nt(q, kb_scr[keys, :]) + bias2
        sp = _softplus2(z)
        if masked:
            causal = (lax.broadcasted_iota(jnp.int32, (tq, tk), 1) - (tk - tq)
                      < lax.broadcasted_iota(jnp.int32, (tq, tk), 0))
            sp = jnp.where(causal, sp, 0.0)
        zs = z - sp
        hi, lo = _split_bf16(sp)
        carry = carry_scr[...]
        ws = [None] * (tk // bk)
        for c in reversed(range(tk // bk)):
            cols = slice(c * bk, (c + 1) * bk)
            r = _dot(jnp.concatenate([hi[:, cols], lo[:, cols]], axis=1), mu_ref[...])
            ws[c] = jnp.exp2(zs[:, cols] - r[:, :bk] - carry)
            carry = carry + r[:, bk:]
        w = jnp.concatenate(ws, axis=1)
        if masked:
            w = jnp.where(causal, w, 0.0)
        carry_scr[...] = carry
        acc_scr[...] += _dot(w.astype(BF16), vb_scr[keys, :])

    def q_tile(qt, _):
        qs = pl.multiple_of(qt * tq, tq)
        q = (q_ref[0, pl.ds(qs, tq), :] * qscale).astype(BF16)
        acc_scr[...] = jnp.zeros_like(acc_scr)
        carry_scr[...] = jnp.zeros_like(carry_scr)
        @pl.when(qt == 0)
        def _():
            tile(q, 0, tq, True)

        @pl.when(qt > 0)
        def _():
            tile(q, qs - tq, 2 * tq, True)

        def k_pair(jj, _):
            tile(q, qs - tq - (jj + 1) * 2 * tq, 2 * tq, False)
            return 0

        lax.fori_loop(0, jnp.maximum(qt - 1, 0) // 2, k_pair, 0)

        @pl.when((qt > 0) & (qt % 2 == 0))
        def _():
            tile(q, 0, tq, False)

        zg = z_ref[0, pl.ds(qs, tq), :]
        o_ref[pl.ds(qs, tq), :] = _head_norm_gate(acc_scr[...], gain, zg).astype(o_ref.dtype)
        return 0

    lax.fori_loop(0, t_len // tq, q_tile, 0)


def _sb_matrix(n, pieces):
    j = jnp.arange(pieces * n)[:, None] % n
    s = jnp.arange(2 * n)[None, :]
    return jnp.where(s < n, (j > s), True).astype(BF16)


def _sb_prompt(p3, sb_bias, gain_a, batch, seq):
    n_heads = p3.shape[2] // HEAD_DIM
    mu = _sb_matrix(SB_BLOCK, 2)
    comp = lambda c: pl.BlockSpec((1, seq, HEAD_DIM), lambda b, h: (c, b, h))
    return pl.pallas_call(
        _sb_prompt_kernel,
        grid=(batch, n_heads),
        in_specs=[
            pl.BlockSpec(memory_space=pltpu.SMEM),
            comp(COMP_QA), comp(COMP_KA), comp(COMP_VA), comp(COMP_ZA),
            pl.BlockSpec((1, HEAD_DIM), lambda b, h: (0, h)),
            pl.BlockSpec(mu.shape, lambda b, h: (0, 0)),
        ],
        out_specs=pl.BlockSpec((seq, HEAD_DIM), lambda b, h: (b, h)),
        out_shape=jax.ShapeDtypeStruct((batch * seq, n_heads * HEAD_DIM), BF16),
        scratch_shapes=[pltpu.VMEM((seq, HEAD_DIM), BF16), pltpu.VMEM((seq, HEAD_DIM), BF16),
                        pltpu.VMEM((SB_TILE, HEAD_DIM), F32), pltpu.VMEM((SB_TILE, SB_BLOCK), F32)],
        compiler_params=pltpu.CompilerParams(
            dimension_semantics=("arbitrary", "arbitrary"), vmem_limit_bytes=VMEM_LIMIT),
        name="sb_prompt",
    )(sb_bias, p3, p3, p3, p3, gain_a, mu)


def _lower_bound(l):
    e = jnp.exp(l - jnp.max(l, axis=0, keepdims=True))
    return e[0:1, :] / jnp.sum(e, axis=0, keepdims=True)


def _chunk_cumsum(x, c_len):
    rows = x.shape[0]
    group = min(rows, CUMSUM_GROUP)
    ti = lax.broadcasted_iota(jnp.int32, (group, group), 0)
    si = lax.broadcasted_iota(jnp.int32, (group, group), 1)
    tri = jnp.where((si <= ti) & (si // c_len == ti // c_len), 1.0, 0.0).astype(BF16)
    hi = x.astype(BF16)
    rest = x - hi.astype(F32)
    mid = rest.astype(BF16)
    lo = (rest - mid.astype(F32)).astype(BF16)
    pieces = jnp.concatenate([hi, mid, lo], axis=1)
    outs = []
    for g0 in range(0, rows, group):
        r = _dot(tri, pieces[g0:g0 + group])
        outs.append((r[:, :HEAD_DIM] + r[:, HEAD_DIM:2 * HEAD_DIM]) + r[:, 2 * HEAD_DIM:])
    return jnp.concatenate(outs, axis=0)


def _hgrn_chunks(qb, fb, ib, lb, st, c_len, n_valid=None):
    rows = qb.shape[0]
    sub = min(HGRN_SUB, c_len)
    q = qb * _sigmoid(qb)
    g = lb + (1.0 - lb) * _sigmoid(fb)
    kk = 1.0 - g
    lg = jnp.log(g)
    if n_valid is not None:
        live = lax.broadcasted_iota(jnp.int32, (rows, HEAD_DIM), 0) < n_valid
        kk = jnp.where(live, kk, 0.0)
        lg = jnp.where(live, lg, 0.0)
    yield
    b = _chunk_cumsum(lg, c_len)
    vb = ib.astype(BF16)
    chunks = [slice(c * c_len, (c + 1) * c_len) for c in range(rows // c_len)]
    spans, operands = [], []
    for ch in chunks:
        for lo_r in range(ch.start, ch.stop, sub):
            hi_r = lo_r + sub
            m = b[lo_r + sub // 2:lo_r + sub // 2 + 1, :]
            qi = (q[lo_r:hi_r] * jnp.exp(b[lo_r:hi_r] - m)).astype(BF16)
            ki = (kk[ch.start:hi_r] * jnp.exp(m - b[ch.start:hi_r])).astype(BF16)
            spans.append((ch.start, lo_r, hi_r))
            operands.append((qi, ki))
    yield
    atts = [_dot_nt(qi, ki) for qi, ki in operands]
    masked = []
    for att, (c0, lo_r, hi_r) in zip(atts, spans):
        t_pos = lax.broadcasted_iota(jnp.int32, att.shape, 0) + (lo_r - c0)
        s_pos = lax.broadcasted_iota(jnp.int32, att.shape, 1)
        masked.append(jnp.where(s_pos <= t_pos, att, 0.0).astype(BF16))
    yield
    o_intra = jnp.concatenate([_dot(att, vb[c0:hi_r]) for att, (c0, _, hi_r) in zip(masked, spans)], axis=0)
    lasts = [b[ch.stop - 1:ch.stop, :] for ch in chunks]
    decayed = [(kk[ch] * jnp.exp(bl - b[ch])).astype(BF16) for ch, bl in zip(chunks, lasts)]
    yield
    incs = [_dot_tn(vb[ch], kd) for ch, kd in zip(chunks, decayed)]
    states = [st]
    for inc, bl in zip(incs, lasts):
        states.append(states[-1] * jnp.exp(bl) + inc)
    starts = [((q[ch] * jnp.exp(b[ch])).astype(BF16), s.astype(BF16)) for ch, s in zip(chunks, states)]
    yield
    o_inter = [_dot_nt(qe, s) for qe, s in starts]
    return o_intra + jnp.concatenate(o_inter, axis=0), states[-1]


def _in_lockstep(generators):
    values = [None] * len(generators)
    running = dict(enumerate(generators))
    while running:
        for i, gen in list(running.items()):
            try:
                next(gen)
            except StopIteration as done:
                values[i] = done.value
                del running[i]
    return values


def _hgrn_prompt_kernel(lbl_ref, q_ref, f_ref, i_ref, z_ref, gain_ref, o_ref, s_ref):
    r_len = HGRN_CHUNK * HGRN_CHUNKS_PER_STEP
    lb = _lower_bound(lbl_ref[...])
    gain = gain_ref[...]

    def step(ci, st):
        rs = pl.multiple_of(ci * r_len, r_len)
        rows = pl.ds(rs, r_len)
        (o, st), = _in_lockstep(
            [_hgrn_chunks(q_ref[0, rows, :], f_ref[0, rows, :], i_ref[0, rows, :], lb, st, HGRN_CHUNK)])
        o_ref[rows, :] = _head_norm_gate(o, gain, z_ref[0, rows, :]).astype(o_ref.dtype)
        return st

    st = lax.fori_loop(0, q_ref.shape[1] // r_len, step, jnp.zeros((HEAD_DIM, HEAD_DIM), F32))
    s_ref[0, 0] = st.T


def _hgrn_prompt(p3, lb_logits, gain_b, batch, seq):
    n_heads = p3.shape[2] // HEAD_DIM
    comp = lambda c: pl.BlockSpec((1, seq, HEAD_DIM), lambda b, h: (c, b, h))
    return pl.pallas_call(
        _hgrn_prompt_kernel,
        grid=(batch, n_heads),
        in_specs=[
            pl.BlockSpec((lb_logits.shape[0], HEAD_DIM), lambda b, h: (0, h)),
            comp(COMP_QB), comp(COMP_FB), comp(COMP_IB), comp(COMP_ZB),
            pl.BlockSpec((1, HEAD_DIM), lambda b, h: (0, h)),
        ],
        out_specs=[
            pl.BlockSpec((seq, HEAD_DIM), lambda b, h: (b, h)),
            pl.BlockSpec((1, 1, HEAD_DIM, HEAD_DIM), lambda b, h: (b, h, 0, 0)),
        ],
        out_shape=[
            jax.ShapeDtypeStruct((batch * seq, n_heads * HEAD_DIM), BF16),
            jax.ShapeDtypeStruct((batch, n_heads, HEAD_DIM, HEAD_DIM), F32),
        ],
        compiler_params=pltpu.CompilerParams(
            dimension_semantics=("arbitrary", "arbitrary"), vmem_limit_bytes=VMEM_LIMIT),
        name="hgrn_prompt",
    )(lb_logits, p3, p3, p3, p3, gain_b)


def _hgrn_decode_kernel(lbl_ref, q_ref, f_ref, i_ref, z_ref, gain_ref, s0_ref, o_ref, s_ref, *, n_valid):
    n_heads = s0_ref.shape[1]
    head = lambda h: slice(h * HEAD_DIM, (h + 1) * HEAD_DIM)
    lb = _lower_bound(lbl_ref[...])
    results = _in_lockstep([
        _hgrn_chunks(q_ref[0, :, head(h)], f_ref[0, :, head(h)], i_ref[0, :, head(h)], lb[:, head(h)],
                     s0_ref[0, h].T, SAMPLE_PAD, n_valid=n_valid)
        for h in range(n_heads)])
    for h, (o, st) in enumerate(results):
        o_ref[:, head(h)] = _head_norm_gate(o, gain_ref[:, head(h)], z_ref[0, :, head(h)]).astype(o_ref.dtype)
        s_ref[0, h] = st.T


def _hgrn_decode(p3s, lb_logits, gain_b, state, n_valid):
    dec_batch, n_heads = state.shape[:2]
    width = n_heads * HEAD_DIM
    comp = lambda c: pl.BlockSpec((1, SAMPLE_PAD, width), lambda b: (c, b, 0))
    state_spec = pl.BlockSpec((1, n_heads, HEAD_DIM, HEAD_DIM), lambda b: (b, 0, 0, 0))
    return pl.pallas_call(
        functools.partial(_hgrn_decode_kernel, n_valid=n_valid),
        grid=(dec_batch,),
        in_specs=[
            pl.BlockSpec(lb_logits.shape, lambda b: (0, 0)),
            comp(COMP_QB), comp(COMP_FB), comp(COMP_IB), comp(COMP_ZB),
            pl.BlockSpec((1, width), lambda b: (0, 0)),
            state_spec,
        ],
        out_specs=[pl.BlockSpec((SAMPLE_PAD, width), lambda b: (b, 0)), state_spec],
        out_shape=[
            jax.ShapeDtypeStruct((dec_batch * SAMPLE_PAD, width), BF16),
            jax.ShapeDtypeStruct(state.shape, F32),
        ],
        compiler_params=pltpu.CompilerParams(dimension_semantics=("arbitrary",)),
        name="hgrn_decode",
    )(lb_logits, p3s, p3s, p3s, p3s, gain_b, state)


def _sb_decode_kernel(pt_ref, qr_ref, bias_ref, kn_ref, vn_ref, *rest, n_valid, n_pg):
    ck_hbm, cv_hbm, mu_ref, z_ref, gain_ref, o_ref, acc_scr, carry_scr, kbuf, vbuf, sems = rest
    j = pl.program_id(1)
    n_steps = pl.num_programs(1)
    step = pl.program_id(0) * n_steps + j
    n_slots = kbuf.shape[0]
    n_heads, n_rows = acc_scr.shape[:2]
    n_q = n_rows // n_heads
    page = kbuf.shape[2] // n_heads

    def page_copies(t):
        seq, js, slot = t // n_steps, t % n_steps, t % n_slots
        copies = []
        for i in range(n_pg):
            src = pt_ref[seq, n_steps * n_pg - 1 - (js * n_pg + i)]
            copies.append(pltpu.make_async_copy(ck_hbm.at[src], kbuf.at[slot, i], sems.at[slot, 0, i]))
            copies.append(pltpu.make_async_copy(cv_hbm.at[src], vbuf.at[slot, i], sems.at[slot, 1, i]))
        return copies

    @pl.when(step == 0)
    def _():
        for t in range(n_slots - 1):
            for copy in page_copies(t):
                copy.start()

    @pl.when(step + n_slots - 1 < pl.num_programs(0) * n_steps)
    def _():
        for copy in page_copies(step + n_slots - 1):
            copy.start()

    zscale = HEAD_DIM ** -0.5 * LOG2E
    bias2 = bias_ref[...] * LOG2E
    head = lambda h: slice(h * HEAD_DIM, (h + 1) * HEAD_DIM)

    def scores(k_heads):
        z = sum(_dot_nt(qr_ref[0, h], k_heads[h].astype(BF16)) for h in range(n_heads))
        return z * zscale + bias2[:, :z.shape[1]]

    @pl.when(j == 0)
    def _():
        r = kn_ref.shape[1]
        z = scores([kn_ref[0, :, head(h)] for h in range(n_heads)])
        t_row = lax.broadcasted_iota(jnp.int32, (n_rows, r), 0) % n_q
        s_col = lax.broadcasted_iota(jnp.int32, (n_rows, r), 1)
        valid = (s_col < t_row) & (s_col < n_valid)
        sp = jnp.where(valid, _softplus2(z), 0.0)
        later_mat = jnp.where(lax.broadcasted_iota(jnp.int32, (r, r), 0)
                              > lax.broadcasted_iota(jnp.int32, (r, r), 1), 1.0, 0.0)
        later = _dot(sp, later_mat, precision=lax.Precision.HIGHEST)
        wb = jnp.where(valid, jnp.exp2(z - sp - later), 0.0).astype(BF16)
        carry_scr[...] = jnp.broadcast_to(jnp.sum(sp, axis=1, keepdims=True), carry_scr.shape)
        for h in range(n_heads):
            acc_scr[h] = _dot(wb, vn_ref[0, :, head(h)].astype(BF16))

    for copy in page_copies(step):
        copy.wait()
    slot = step % n_slots
    kp_refs = [kbuf.at[slot, i] for i in range(n_pg)]
    vp_refs = [vbuf.at[slot, i] for i in range(n_pg)]
    zs = [scores([kp_ref[pl.ds(h, page, stride=n_heads), :] for h in range(n_heads)]) for kp_ref in kp_refs]
    sps = [_softplus2(z) for z in zs]
    rs = [_dot(jnp.concatenate(_split_bf16(sp), axis=1), mu_ref[...]) for sp in sps]
    carry = carry_scr[...]
    wbs = []
    for z, sp, r in zip(zs, sps, rs):
        wbs.append(jnp.exp2(z - sp - r[:, :page] - carry).astype(BF16))
        carry = carry + r[:, page:]
    carry_scr[...] = carry
    for h in range(n_heads):
        acc_scr[h] += sum(_dot(wb, vp_ref[pl.ds(h, page, stride=n_heads), :].astype(BF16))
                          for wb, vp_ref in zip(wbs, vp_refs))

    @pl.when(j == pl.num_programs(1) - 1)
    def _():
        rh = lax.broadcasted_iota(jnp.int32, (n_rows, HEAD_DIM), 0) // n_q
        o = jnp.zeros((n_rows, HEAD_DIM), F32)
        for h in range(n_heads):
            o = o + jnp.where(rh == h, acc_scr[h], 0.0)
        o_ref[0] = _head_norm_gate(o, gain_ref[...], z_ref[0]).astype(o_ref.dtype)


def _sb_decode(page_table, q_rows, bias_rows, p3s, cache_k, cache_v, z_r, gain_r, n_valid):
    dec_batch, n_pages = page_table.shape
    page_rows = cache_k.shape[1]
    n_heads, n_rows = q_rows.shape[1:3]
    width = n_heads * HEAD_DIM
    page = page_rows // n_heads
    assert page == HEAD_DIM
    mu = _sb_matrix(page, 2)
    new = lambda c: pl.BlockSpec((1, SAMPLE_PAD, width), lambda b, j, pt: (c, b, 0))
    n_pg = DECODE_PAGES_PER_STEP
    assert n_pages % n_pg == 0 and dec_batch * (n_pages // n_pg) >= DECODE_SLOTS - 1
    page_buf = pltpu.VMEM((DECODE_SLOTS, n_pg, page_rows, HEAD_DIM), cache_k.dtype)
    grid_spec = pltpu.PrefetchScalarGridSpec(
        num_scalar_prefetch=1,
        grid=(dec_batch, n_pages // n_pg),
        in_specs=[
            pl.BlockSpec((1, n_heads, n_rows, HEAD_DIM), lambda b, j, pt: (b, 0, 0, 0)),
            pl.BlockSpec((n_rows, HEAD_DIM), lambda b, j, pt: (0, 0)),
            new(COMP_KA), new(COMP_VA),
            pl.BlockSpec(memory_space=pl.ANY), pl.BlockSpec(memory_space=pl.ANY),
            pl.BlockSpec(mu.shape, lambda b, j, pt: (0, 0)),
            pl.BlockSpec((1, n_rows, HEAD_DIM), lambda b, j, pt: (b, 0, 0)),
            pl.BlockSpec((n_rows, HEAD_DIM), lambda b, j, pt: (0, 0)),
        ],
        out_specs=pl.BlockSpec((1, n_rows, HEAD_DIM), lambda b, j, pt: (b, 0, 0)),
        scratch_shapes=[pltpu.VMEM((n_heads, n_rows, HEAD_DIM), F32), pltpu.VMEM((n_rows, HEAD_DIM), F32),
                        page_buf, page_buf, pltpu.SemaphoreType.DMA((DECODE_SLOTS, 2, n_pg))],
    )
    return pl.pallas_call(
        functools.partial(_sb_decode_kernel, n_valid=n_valid, n_pg=n_pg),
        grid_spec=grid_spec,
        out_shape=jax.ShapeDtypeStruct((dec_batch, n_rows, HEAD_DIM), BF16),
        compiler_params=pltpu.CompilerParams(
            dimension_semantics=("arbitrary", "arbitrary"), vmem_limit_bytes=VMEM_LIMIT),
        name="sb_decode",
    )(page_table, q_rows, bias_rows, p3s, p3s, cache_k, cache_v, mu, z_r, gain_r)


def _out_proj_kernel(x_ref, ma_ref, mb_ref, wa_ref, wb_ref, fw_ref, o_ref):
    y = x_ref[...] + _dot(ma_ref[...], wa_ref[...]) + _dot(mb_ref[...], wb_ref[...])
    ms = jnp.mean(y * y, axis=-1, keepdims=True)
    o_ref[...] = y * lax.rsqrt(ms + EPS) * fw_ref[...]


def _out_proj(x, m_a, m_b, w_bf16, final_w, bm):
    t, d = x.shape
    wa = m_a.shape[1]
    wb = m_b.shape[1]
    assert wa == wb
    return pl.pallas_call(
        _out_proj_kernel,
        grid=(t // bm,),
        in_specs=[
            pl.BlockSpec((bm, d), lambda m: (m, 0)),
            pl.BlockSpec((bm, wa), lambda m: (m, 0)),
            pl.BlockSpec((bm, wb), lambda m: (m, 0)),
            pl.BlockSpec((wa, d), lambda m: (0, 0)),
            pl.BlockSpec((wb, d), lambda m: (1, 0)),
            pl.BlockSpec((1, d), lambda m: (0, 0)),
        ],
        out_specs=pl.BlockSpec((bm, d), lambda m: (m, 0)),
        out_shape=jax.ShapeDtypeStruct((t, d), F32),
        compiler_params=pltpu.CompilerParams(
            dimension_semantics=("arbitrary",), vmem_limit_bytes=VMEM_LIMIT),
        name="out_proj",
    )(x, m_a, m_b, w_bf16, w_bf16, final_w)


def kernel(x_prompt, x_sample, cache_k, cache_v, state_s, page_table, norm_w, w_in, gain_a, gain_b,
           sb_bias, lb_logits, w_out, final_norm_w):
    depth = norm_w.shape[0]
    assert depth == 1
    batch, seq, d_model = x_prompt.shape
    dec_batch, dec_seq, _ = x_sample.shape
    n_heads_a, head_dim = cache_k.shape[3:]
    w_a = n_heads_a * head_dim
    assert head_dim == HEAD_DIM and state_s.shape[3:] == (HEAD_DIM, HEAD_DIM)
    assert dec_seq <= SAMPLE_PAD and seq % SB_TILE == 0 and seq % (HGRN_CHUNK * HGRN_CHUNKS_PER_STEP) == 0

    w_in_b = w_in[0].astype(BF16)
    w_out_b = w_out[0].astype(BF16)
    nw = norm_w[0][None, :]
    fw = final_norm_w[None, :]
    ga = gain_a[0][None, :]
    gb = gain_b[0][None, :]

    xp = x_prompt.reshape(batch * seq, d_model)
    xs = jnp.pad(x_sample, ((0, 0), (0, SAMPLE_PAD - dec_seq), (0, 0))).reshape(dec_batch * SAMPLE_PAD, d_model)
    p3, k_p, v_p, p3s = _in_proj(xp, xs, nw, w_in_b, bm=1024)
    m_a = _sb_prompt(p3, sb_bias, ga, batch, seq)
    m_b, s_p = _hgrn_prompt(p3, lb_logits, gb, batch, seq)
    y_prompt = _out_proj(xp, m_a, m_b, w_out_b, fw, bm=512).reshape(batch, seq, d_model)
    k_p = k_p.reshape(1, batch, seq, n_heads_a, head_dim)
    v_p = v_p.reshape(1, batch, seq, n_heads_a, head_dim)

    rows = lambda c: p3s[c].reshape(dec_batch, SAMPLE_PAD, -1)[:, :dec_seq]
    q_t = rows(COMP_QA).reshape(dec_batch, dec_seq, n_heads_a, head_dim).transpose(0, 2, 1, 3)
    q_rows = (q_t[:, :, None, :, :] * jnp.eye(n_heads_a, dtype=F32)[None, :, :, None, None]).reshape(
        dec_batch, n_heads_a, n_heads_a * dec_seq, head_dim).astype(BF16)
    bias_rows = jnp.broadcast_to(jnp.repeat(sb_bias[0], dec_seq)[:, None], (n_heads_a * dec_seq, head_dim))
    to_rows = lambda a: a.reshape(dec_batch, dec_seq, n_heads_a, head_dim).transpose(0, 2, 1, 3).reshape(
        dec_batch, n_heads_a * dec_seq, head_dim)
    z_r = to_rows(rows(COMP_ZA))
    gain_r = jnp.repeat(gain_a[0].reshape(n_heads_a, head_dim), dec_seq, axis=0)
    ck = cache_k[0].reshape(cache_k.shape[1], cache_k.shape[2] * n_heads_a, head_dim)
    cv = cache_v[0].reshape(cache_v.shape[1], cache_v.shape[2] * n_heads_a, head_dim)
    o_r = _sb_decode(page_table, q_rows, bias_rows, p3s, ck, cv, z_r, gain_r, dec_seq)
    m_a_s = o_r.reshape(dec_batch, n_heads_a, dec_seq, head_dim).transpose(0, 2, 1, 3).reshape(
        dec_batch, dec_seq, w_a)
    m_a_s = jnp.pad(m_a_s, ((0, 0), (0, SAMPLE_PAD - dec_seq), (0, 0))).reshape(dec_batch * SAMPLE_PAD, w_a)
    m_b_s, s_s = _hgrn_decode(p3s, lb_logits, gb, state_s[0], dec_seq)
    y_s = _out_proj(xs, m_a_s, m_b_s, w_out_b, fw, bm=dec_batch * SAMPLE_PAD)
    y_sample = y_s.reshape(dec_batch, SAMPLE_PAD, d_model)[:, :dec_seq]
    k_s = rows(COMP_KA).reshape(1, dec_batch, dec_seq, n_heads_a, head_dim)
    v_s = rows(COMP_VA).reshape(1, dec_batch, dec_seq, n_heads_a, head_dim)

    return (y_prompt, y_sample, k_p, v_p, s_p[None], k_s, v_s, s_s[None])
```

```python
import functools

import jax
import jax.numpy as jnp
from jax import lax
from jax.experimental import pallas as pl
from jax.experimental.pallas import tpu as pltpu

EPS = 1e-6
HEAD_DIM = 128
N_COMP = 8
COMP_QA, COMP_KA, COMP_VA, COMP_ZA, COMP_QB, COMP_FB, COMP_IB, COMP_ZB = range(N_COMP)
SB_BLOCK = 128
SB_TILE = 512
LOG2E = 1.4426950408889634
HGRN_CHUNK = 64
HGRN_CHUNKS_PER_STEP = 8
CUMSUM_GROUP = 256
HGRN_SUB = 16
SAMPLE_PAD = 8
DECODE_PAGES_PER_STEP = 8
DECODE_SLOTS = 3
VMEM_LIMIT = 60 * 1024 * 1024

F32 = jnp.float32
BF16 = jnp.bfloat16


def _dot(a, b, **kw):
    return jnp.dot(a, b, preferred_element_type=F32, **kw)


def _dot_nt(a, b):
    return lax.dot_general(a, b, (((1,), (1,)), ((), ())), preferred_element_type=F32)


def _dot_tn(a, b):
    return lax.dot_general(a, b, (((0,), (0,)), ((), ())), preferred_element_type=F32)


def _sigmoid(x):
    return 1.0 / (1.0 + jnp.exp(-x))


def _softplus2(z2):
    return jnp.maximum(z2, 0.0) + jnp.log2(1.0 + jnp.exp2(-jnp.abs(z2)))


def _split_bf16(x):
    hi = x.astype(BF16)
    lo = (x - hi.astype(F32)).astype(BF16)
    return hi, lo


def _in_proj_kernel(x_ref, xs_ref, nw_ref, w_ref, o_ref, k_ref, v_ref, os_ref, h_scr):
    m, n = pl.program_id(0), pl.program_id(1)

    def normed(x):
        ms = jnp.mean(x * x, axis=-1, keepdims=True)
        return (x * lax.rsqrt(ms + EPS) * nw_ref[...]).astype(BF16)

    @pl.when(n == 0)
    def _():
        h_scr[...] = normed(x_ref[...])

    @pl.when(m == 0)
    def _():
        os_ref[0] = _dot(normed(xs_ref[...]), w_ref[...])

    res = _dot(h_scr[...], w_ref[...])
    o_ref[0] = res
    k_ref[...] = jnp.where(n <= COMP_KA, res, k_ref[...])
    v_ref[...] = jnp.where(n <= COMP_VA, res, v_ref[...])


def _in_proj(x, xs, norm_w, w_bf16, bm):
    t, d = x.shape
    ts = xs.shape[0]
    d_in = w_bf16.shape[1]
    bn = d_in // N_COMP
    return pl.pallas_call(
        _in_proj_kernel,
        grid=(t // bm, N_COMP),
        in_specs=[
            pl.BlockSpec((bm, d), lambda m, n: (m, 0)),
            pl.BlockSpec((ts, d), lambda m, n: (0, 0)),
            pl.BlockSpec((1, d), lambda m, n: (0, 0)),
            pl.BlockSpec((d, bn), lambda m, n: (0, n)),
        ],
        out_specs=[
            pl.BlockSpec((1, bm, bn), lambda m, n: (n, m, 0)),
            pl.BlockSpec((bm, bn), lambda m, n: (m, 0)),
            pl.BlockSpec((bm, bn), lambda m, n: (m, 0)),
            pl.BlockSpec((1, ts, bn), lambda m, n: (jnp.where(m == 0, n, N_COMP - 1), 0, 0)),
        ],
        out_shape=[
            jax.ShapeDtypeStruct((N_COMP, t, bn), F32),
            jax.ShapeDtypeStruct((t, bn), F32),
            jax.ShapeDtypeStruct((t, bn), F32),
            jax.ShapeDtypeStruct((N_COMP, ts, bn), F32),
        ],
        scratch_shapes=[pltpu.VMEM((bm, d), BF16)],
        compiler_params=pltpu.CompilerParams(
            dimension_semantics=("arbitrary", "arbitrary"), vmem_limit_bytes=VMEM_LIMIT),
        name="in_proj",
    )(x, xs, norm_w, w_bf16)


def _head_norm_gate(o, gain, z):
    ms = jnp.mean(o * o, axis=-1, keepdims=True)
    return o * lax.rsqrt(ms + EPS) * gain * (z * _sigmoid(z))


def _sb_prompt_kernel(bias_ref, q_ref, k_ref, v_ref, z_ref, gain_ref, mu_ref, o_ref,
                      kb_scr, vb_scr, acc_scr, carry_scr):
    tq = SB_TILE
    bk = SB_BLOCK
    t_len = q_ref.shape[1]
    bias2 = bias_ref[0, pl.program_id(1)] * LOG2E
    qscale = HEAD_DIM ** -0.5 * LOG2E
    kb_scr[...] = k_ref[0].astype(BF16)
    vb_scr[...] = v_ref[0].astype(BF16)
    gain = gain_ref[...]

    def tile(q, k0, tk, masked):
        keys = pl.ds(pl.multiple_of(k0, tq), tk)
        z = _dot_nt(q, kb_scr[keys, :]) + bias2
        sp = _softplus2(z)
        if masked:
            causal = (lax.broadcasted_iota(jnp.int32, (tq, tk), 1) - (tk - tq)
                      < lax.broadcasted_iota(jnp.int32, (tq, tk), 0))
            sp = jnp.where(causal, sp, 0.0)
        zs = z - sp
        spb = sp.astype(BF16)
        carry = carry_scr[...]
        ws = [None] * (tk // bk)
        for c in reversed(range(1, tk // bk, 2)):
            later = _dot(spb[:, (c - 1) * bk:(c + 1) * bk], mu_ref[...])
            for blk, lo in ((c, bk), (c - 1, 0)):
                cols = slice(blk * bk, (blk + 1) * bk)
                inside = later[:, lo:lo + bk]
                ws[blk] = jnp.exp2(zs[:, cols] - inside - carry)
                total = inside[:, :1] + sp[:, blk * bk:blk * bk + 1]
                carry = carry + jnp.broadcast_to(total, carry.shape)
        w = jnp.concatenate(ws, axis=1)
        if masked:
            w = jnp.where(causal, w, 0.0)
        carry_scr[...] = carry
        acc_scr[...] += _dot(w.astype(BF16), vb_scr[keys, :])

    def q_tile(qt, _):
        qs = pl.multiple_of(qt * tq, tq)
        q = (q_ref[0, pl.ds(qs, tq), :] * qscale).astype(BF16)
        acc_scr[...] = jnp.zeros_like(acc_scr)
        carry_scr[...] = jnp.zeros_like(carry_scr)
        @pl.when(qt == 0)
        def _():
            tile(q, 0, tq, True)

        @pl.when(qt > 0)
        def _():
            tile(q, qs - tq, 2 * tq, True)

        def k_pair(jj, _):
            tile(q, qs - tq - (jj + 1) * 2 * tq, 2 * tq, False)
            return 0

        lax.fori_loop(0, jnp.maximum(qt - 1, 0) // 2, k_pair, 0)

        @pl.when((qt > 0) & (qt % 2 == 0))
        def _():
            tile(q, 0, tq, False)

        zg = z_ref[0, pl.ds(qs, tq), :]
        o_ref[pl.ds(qs, tq), :] = _head_norm_gate(acc_scr[...], gain, zg).astype(o_ref.dtype)
        return 0

    lax.fori_loop(0, t_len // tq, q_tile, 0)


def _sb_matrix(n, pieces):
    j = jnp.arange(pieces * n)[:, None] % n
    s = jnp.arange(2 * n)[None, :]
    return jnp.where(s < n, (j > s), True).astype(BF16)


def _later_matrix(n, blocks):
    j = jnp.arange(blocks * n)[:, None]
    s = jnp.arange(blocks * n)[None, :]
    return ((j // n == s // n) & (j > s)).astype(BF16)


def _sb_prompt(p3, sb_bias, gain_a, batch, seq):
    n_heads = p3.shape[2] // HEAD_DIM
    mu = _later_matrix(SB_BLOCK, 2)
    comp = lambda c: pl.BlockSpec((1, seq, HEAD_DIM), lambda b, h: (c, b, h))
    return pl.pallas_call(
        _sb_prompt_kernel,
        grid=(batch, n_heads),
        in_specs=[
            pl.BlockSpec(memory_space=pltpu.SMEM),
            comp(COMP_QA), comp(COMP_KA), comp(COMP_VA), comp(COMP_ZA),
            pl.BlockSpec((1, HEAD_DIM), lambda b, h: (0, h)),
            pl.BlockSpec(mu.shape, lambda b, h: (0, 0)),
        ],
        out_specs=pl.BlockSpec((seq, HEAD_DIM), lambda b, h: (b, h)),
        out_shape=jax.ShapeDtypeStruct((batch * seq, n_heads * HEAD_DIM), BF16),
        scratch_shapes=[pltpu.VMEM((seq, HEAD_DIM), BF16), pltpu.VMEM((seq, HEAD_DIM), BF16),
                        pltpu.VMEM((SB_TILE, HEAD_DIM), F32), pltpu.VMEM((SB_TILE, SB_BLOCK), F32)],
        compiler_params=pltpu.CompilerParams(
            dimension_semantics=("arbitrary", "arbitrary"), vmem_limit_bytes=VMEM_LIMIT),
        name="sb_prompt",
    )(sb_bias, p3, p3, p3, p3, gain_a, mu)


def _lower_bound(l):
    e = jnp.exp(l - jnp.max(l, axis=0, keepdims=True))
    return e[0:1, :] / jnp.sum(e, axis=0, keepdims=True)


def _chunk_cumsum(x, c_len):
    rows = x.shape[0]
    group = min(rows, CUMSUM_GROUP)
    ti = lax.broadcasted_iota(jnp.int32, (group, group), 0)
    si = lax.broadcasted_iota(jnp.int32, (group, group), 1)
    tri = jnp.where((si <= ti) & (si // c_len == ti // c_len), 1.0, 0.0).astype(BF16)
    hi = x.astype(BF16)
    rest = x - hi.astype(F32)
    mid = rest.astype(BF16)
    lo = (rest - mid.astype(F32)).astype(BF16)
    pieces = jnp.concatenate([hi, mid, lo], axis=1)
    outs = []
    for g0 in range(0, rows, group):
        r = _dot(tri, pieces[g0:g0 + group])
        outs.append((r[:, :HEAD_DIM] + r[:, HEAD_DIM:2 * HEAD_DIM]) + r[:, 2 * HEAD_DIM:])
    return jnp.concatenate(outs, axis=0)


def _hgrn_chunks(qb, fb, ib, lb, st, c_len, n_valid=None):
    rows = qb.shape[0]
    sub = min(HGRN_SUB, c_len)
    q = qb * _sigmoid(qb)
    g = lb + (1.0 - lb) * _sigmoid(fb)
    kk = 1.0 - g
    lg = jnp.log(g)
    if n_valid is not None:
        live = lax.broadcasted_iota(jnp.int32, (rows, HEAD_DIM), 0) < n_valid
        kk = jnp.where(live, kk, 0.0)
        lg = jnp.where(live, lg, 0.0)
    yield
    b = _chunk_cumsum(lg, c_len)
    vb = ib.astype(BF16)
    chunks = [slice(c * c_len, (c + 1) * c_len) for c in range(rows // c_len)]
    spans, operands = [], []
    for ch in chunks:
        for lo_r in range(ch.start, ch.stop, sub):
            hi_r = lo_r + sub
            m = b[lo_r + sub // 2:lo_r + sub // 2 + 1, :]
            qi = (q[lo_r:hi_r] * jnp.exp(b[lo_r:hi_r] - m)).astype(BF16)
            ki = (kk[ch.start:hi_r] * jnp.exp(m - b[ch.start:hi_r])).astype(BF16)
            spans.append((ch.start, lo_r, hi_r))
            operands.append((qi, ki))
    yield
    atts = [_dot_nt(qi, ki) for qi, ki in operands]
    masked = []
    for att, (c0, lo_r, hi_r) in zip(atts, spans):
        t_pos = lax.broadcasted_iota(jnp.int32, att.shape, 0) + (lo_r - c0)
        s_pos = lax.broadcasted_iota(jnp.int32, att.shape, 1)
        masked.append(jnp.where(s_pos <= t_pos, att, 0.0).astype(BF16))
    yield
    o_intra = jnp.concatenate([_dot(att, vb[c0:hi_r]) for att, (c0, _, hi_r) in zip(masked, spans)], axis=0)
    lasts = [b[ch.stop - 1:ch.stop, :] for ch in chunks]
    decayed = [(kk[ch] * jnp.exp(bl - b[ch])).astype(BF16) for ch, bl in zip(chunks, lasts)]
    yield
    incs = [_dot_tn(vb[ch], kd) for ch, kd in zip(chunks, decayed)]
    states = [st]
    for inc, bl in zip(incs, lasts):
        states.append(states[-1] * jnp.exp(bl) + inc)
    starts = [((q[ch] * jnp.exp(b[ch])).astype(BF16), s.astype(BF16)) for ch, s in zip(chunks, states)]
    yield
    o_inter = [_dot_nt(qe, s) for qe, s in starts]
    return o_intra + jnp.concatenate(o_inter, axis=0), states[-1]


def _in_lockstep(generators):
    values = [None] * len(generators)
    running = dict(enumerate(generators))
    while running:
        for i, gen in list(running.items()):
            try:
                next(gen)
            except StopIteration as done:
                values[i] = done.value
                del running[i]
    return values


def _hgrn_prompt_kernel(lbl_ref, q_ref, f_ref, i_ref, z_ref, gain_ref, o_ref, s_ref):
    r_len = HGRN_CHUNK * HGRN_CHUNKS_PER_STEP
    lb = _lower_bound(lbl_ref[...])
    gain = gain_ref[...]

    def step(ci, st):
        rs = pl.multiple_of(ci * r_len, r_len)
        rows = pl.ds(rs, r_len)
        (o, st), = _in_lockstep(
            [_hgrn_chunks(q_ref[0, rows, :], f_ref[0, rows, :], i_ref[0, rows, :], lb, st, HGRN_CHUNK)])
        o_ref[rows, :] = _head_norm_gate(o, gain, z_ref[0, rows, :]).astype(o_ref.dtype)
        return st

    st = lax.fori_loop(0, q_ref.shape[1] // r_len, step, jnp.zeros((HEAD_DIM, HEAD_DIM), F32))
    s_ref[0, 0] = st.T


def _hgrn_prompt(p3, lb_logits, gain_b, batch, seq):
    n_heads = p3.shape[2] // HEAD_DIM
    comp = lambda c: pl.BlockSpec((1, seq, HEAD_DIM), lambda b, h: (c, b, h))
    return pl.pallas_call(
        _hgrn_prompt_kernel,
        grid=(batch, n_heads),
        in_specs=[
            pl.BlockSpec((lb_logits.shape[0], HEAD_DIM), lambda b, h: (0, h)),
            comp(COMP_QB), comp(COMP_FB), comp(COMP_IB), comp(COMP_ZB),
            pl.BlockSpec((1, HEAD_DIM), lambda b, h: (0, h)),
        ],
        out_specs=[
            pl.BlockSpec((seq, HEAD_DIM), lambda b, h: (b, h)),
            pl.BlockSpec((1, 1, HEAD_DIM, HEAD_DIM), lambda b, h: (b, h, 0, 0)),
        ],
        out_shape=[
            jax.ShapeDtypeStruct((batch * seq, n_heads * HEAD_DIM), BF16),
            jax.ShapeDtypeStruct((batch, n_heads, HEAD_DIM, HEAD_DIM), F32),
        ],
        compiler_params=pltpu.CompilerParams(
            dimension_semantics=("arbitrary", "arbitrary"), vmem_limit_bytes=VMEM_LIMIT),
        name="hgrn_prompt",
    )(lb_logits, p3, p3, p3, p3, gain_b)


def _hgrn_decode_kernel(lbl_ref, q_ref, f_ref, i_ref, z_ref, gain_ref, s0_ref, o_ref, s_ref, *, n_valid):
    n_heads = s0_ref.shape[1]
    head = lambda h: slice(h * HEAD_DIM, (h + 1) * HEAD_DIM)
    lb = _lower_bound(lbl_ref[...])
    results = _in_lockstep([
        _hgrn_chunks(q_ref[0, :, head(h)], f_ref[0, :, head(h)], i_ref[0, :, head(h)], lb[:, head(h)],
                     s0_ref[0, h].T, SAMPLE_PAD, n_valid=n_valid)
        for h in range(n_heads)])
    for h, (o, st) in enumerate(results):
        o_ref[:, head(h)] = _head_norm_gate(o, gain_ref[:, head(h)], z_ref[0, :, head(h)]).astype(o_ref.dtype)
        s_ref[0, h] = st.T


def _hgrn_decode(p3s, lb_logits, gain_b, state, n_valid):
    dec_batch, n_heads = state.shape[:2]
    width = n_heads * HEAD_DIM
    comp = lambda c: pl.BlockSpec((1, SAMPLE_PAD, width), lambda b: (c, b, 0))
    state_spec = pl.BlockSpec((1, n_heads, HEAD_DIM, HEAD_DIM), lambda b: (b, 0, 0, 0))
    return pl.pallas_call(
        functools.partial(_hgrn_decode_kernel, n_valid=n_valid),
        grid=(dec_batch,),
        in_specs=[
            pl.BlockSpec(lb_logits.shape, lambda b: (0, 0)),
            comp(COMP_QB), comp(COMP_FB), comp(COMP_IB), comp(COMP_ZB),
            pl.BlockSpec((1, width), lambda b: (0, 0)),
            state_spec,
        ],
        out_specs=[pl.BlockSpec((SAMPLE_PAD, width), lambda b: (b, 0)), state_spec],
        out_shape=[
            jax.ShapeDtypeStruct((dec_batch * SAMPLE_PAD, width), BF16),
            jax.ShapeDtypeStruct(state.shape, F32),
        ],
        compiler_params=pltpu.CompilerParams(dimension_semantics=("arbitrary",)),
        name="hgrn_decode",
    )(lb_logits, p3s, p3s, p3s, p3s, gain_b, state)


def _sb_decode_kernel(pt_ref, qr_ref, bias_ref, kn_ref, vn_ref, *rest, n_valid, n_pg):
    ck_hbm, cv_hbm, mu_ref, z_ref, gain_ref, o_ref, acc_scr, carry_scr, kbuf, vbuf, sems = rest
    j = pl.program_id(1)
    n_steps = pl.num_programs(1)
    step = pl.program_id(0) * n_steps + j
    n_slots = kbuf.shape[0]
    n_heads, n_rows = acc_scr.shape[:2]
    n_q = n_rows // n_heads
    page = kbuf.shape[2] // n_heads

    def page_copies(t):
        seq, js, slot = t // n_steps, t % n_steps, t % n_slots
        copies = []
        for i in range(n_pg):
            src = pt_ref[seq, n_steps * n_pg - 1 - (js * n_pg + i)]
            copies.append(pltpu.make_async_copy(ck_hbm.at[src], kbuf.at[slot, i], sems.at[slot, 0, i]))
            copies.append(pltpu.make_async_copy(cv_hbm.at[src], vbuf.at[slot, i], sems.at[slot, 1, i]))
        return copies

    @pl.when(step == 0)
    def _():
        for t in range(n_slots - 1):
            for copy in page_copies(t):
                copy.start()

    @pl.when(step + n_slots - 1 < pl.num_programs(0) * n_steps)
    def _():
        for copy in page_copies(step + n_slots - 1):
            copy.start()

    zscale = HEAD_DIM ** -0.5 * LOG2E
    bias2 = bias_ref[...] * LOG2E
    head = lambda h: slice(h * HEAD_DIM, (h + 1) * HEAD_DIM)

    def scores(k_heads):
        z = sum(_dot_nt(qr_ref[0, h], k_heads[h].astype(BF16)) for h in range(n_heads))
        return z * zscale + bias2[:, :z.shape[1]]

    @pl.when(j == 0)
    def _():
        r = kn_ref.shape[1]
        z = scores([kn_ref[0, :, head(h)] for h in range(n_heads)])
        t_row = lax.broadcasted_iota(jnp.int32, (n_rows, r), 0) % n_q
        s_col = lax.broadcasted_iota(jnp.int32, (n_rows, r), 1)
        valid = (s_col < t_row) & (s_col < n_valid)
        sp = jnp.where(valid, _softplus2(z), 0.0)
        later_mat = jnp.where(lax.broadcasted_iota(jnp.int32, (r, r), 0)
                              > lax.broadcasted_iota(jnp.int32, (r, r), 1), 1.0, 0.0)
        later = _dot(sp, later_mat, precision=lax.Precision.HIGHEST)
        wb = jnp.where(valid, jnp.exp2(z - sp - later), 0.0).astype(BF16)
        carry_scr[...] = jnp.broadcast_to(jnp.sum(sp, axis=1, keepdims=True), carry_scr.shape)
        for h in range(n_heads):
            acc_scr[h] = _dot(wb, vn_ref[0, :, head(h)].astype(BF16))

    for copy in page_copies(step):
        copy.wait()
    slot = step % n_slots
    kp_refs = [kbuf.at[slot, i] for i in range(n_pg)]
    vp_refs = [vbuf.at[slot, i] for i in range(n_pg)]
    zs = [scores([kp_ref[pl.ds(h, page, stride=n_heads), :] for h in range(n_heads)]) for kp_ref in kp_refs]
    sps = [_softplus2(z) for z in zs]
    rs = [_dot(jnp.concatenate(_split_bf16(sp), axis=1), mu_ref[...]) for sp in sps]
    carry = carry_scr[...]
    wbs = []
    for z, sp, r in zip(zs, sps, rs):
        wbs.append(jnp.exp2(z - sp - r[:, :page] - carry).astype(BF16))
        carry = carry + r[:, page:]
    carry_scr[...] = carry
    for h in range(n_heads):
        acc_scr[h] += sum(_dot(wb, vp_ref[pl.ds(h, page, stride=n_heads), :].astype(BF16))
                          for wb, vp_ref in zip(wbs, vp_refs))

    @pl.when(j == pl.num_programs(1) - 1)
    def _():
        rh = lax.broadcasted_iota(jnp.int32, (n_rows, HEAD_DIM), 0) // n_q
        o = jnp.zeros((n_rows, HEAD_DIM), F32)
        for h in range(n_heads):
            o = o + jnp.where(rh == h, acc_scr[h], 0.0)
        o_ref[0] = _head_norm_gate(o, gain_ref[...], z_ref[0]).astype(o_ref.dtype)


def _sb_decode(page_table, q_rows, bias_rows, p3s, cache_k, cache_v, z_r, gain_r, n_valid):
    dec_batch, n_pages = page_table.shape
    page_rows = cache_k.shape[1]
    n_heads, n_rows = q_rows.shape[1:3]
    width = n_heads * HEAD_DIM
    page = page_rows // n_heads
    assert page == HEAD_DIM
    mu = _sb_matrix(page, 2)
    new = lambda c: pl.BlockSpec((1, SAMPLE_PAD, width), lambda b, j, pt: (c, b, 0))
    n_pg = DECODE_PAGES_PER_STEP
    assert n_pages % n_pg == 0 and dec_batch * (n_pages // n_pg) >= DECODE_SLOTS - 1
    page_buf = pltpu.VMEM((DECODE_SLOTS, n_pg, page_rows, HEAD_DIM), cache_k.dtype)
    grid_spec = pltpu.PrefetchScalarGridSpec(
        num_scalar_prefetch=1,
        grid=(dec_batch, n_pages // n_pg),
        in_specs=[
            pl.BlockSpec((1, n_heads, n_rows, HEAD_DIM), lambda b, j, pt: (b, 0, 0, 0)),
            pl.BlockSpec((n_rows, HEAD_DIM), lambda b, j, pt: (0, 0)),
            new(COMP_KA), new(COMP_VA),
            pl.BlockSpec(memory_space=pl.ANY), pl.BlockSpec(memory_space=pl.ANY),
            pl.BlockSpec(mu.shape, lambda b, j, pt: (0, 0)),
            pl.BlockSpec((1, n_rows, HEAD_DIM), lambda b, j, pt: (b, 0, 0)),
            pl.BlockSpec((n_rows, HEAD_DIM), lambda b, j, pt: (0, 0)),
        ],
        out_specs=pl.BlockSpec((1, n_rows, HEAD_DIM), lambda b, j, pt: (b, 0, 0)),
        scratch_shapes=[pltpu.VMEM((n_heads, n_rows, HEAD_DIM), F32), pltpu.VMEM((n_rows, HEAD_DIM), F32),
                        page_buf, page_buf, pltpu.SemaphoreType.DMA((DECODE_SLOTS, 2, n_pg))],
    )
    return pl.pallas_call(
        functools.partial(_sb_decode_kernel, n_valid=n_valid, n_pg=n_pg),
        grid_spec=grid_spec,
        out_shape=jax.ShapeDtypeStruct((dec_batch, n_rows, HEAD_DIM), BF16),
        compiler_params=pltpu.CompilerParams(
            dimension_semantics=("arbitrary", "arbitrary"), vmem_limit_bytes=VMEM_LIMIT),
        name="sb_decode",
    )(page_table, q_rows, bias_rows, p3s, p3s, cache_k, cache_v, mu, z_r, gain_r)


def _out_proj_kernel(x_ref, ma_ref, mb_ref, wa_ref, wb_ref, fw_ref, o_ref):
    y = x_ref[...] + _dot(ma_ref[...], wa_ref[...]) + _dot(mb_ref[...], wb_ref[...])
    ms = jnp.mean(y * y, axis=-1, keepdims=True)
    o_ref[...] = y * lax.rsqrt(ms + EPS) * fw_ref[...]


def _out_proj(x, m_a, m_b, w_bf16, final_w, bm):
    t, d = x.shape
    wa = m_a.shape[1]
    wb = m_b.shape[1]
    assert wa == wb
    return pl.pallas_call(
        _out_proj_kernel,
        grid=(t // bm,),
        in_specs=[
            pl.BlockSpec((bm, d), lambda m: (m, 0)),
            pl.BlockSpec((bm, wa), lambda m: (m, 0)),
            pl.BlockSpec((bm, wb), lambda m: (m, 0)),
            pl.BlockSpec((wa, d), lambda m: (0, 0)),
            pl.BlockSpec((wb, d), lambda m: (1, 0)),
            pl.BlockSpec((1, d), lambda m: (0, 0)),
        ],
        out_specs=pl.BlockSpec((bm, d), lambda m: (m, 0)),
        out_shape=jax.ShapeDtypeStruct((t, d), F32),
        compiler_params=pltpu.CompilerParams(
            dimension_semantics=("arbitrary",), vmem_limit_bytes=VMEM_LIMIT),
        name="out_proj",
    )(x, m_a, m_b, w_bf16, w_bf16, final_w)


def kernel(x_prompt, x_sample, cache_k, cache_v, state_s, page_table, norm_w, w_in, gain_a, gain_b,
           sb_bias, lb_logits, w_out, final_norm_w):
    depth = norm_w.shape[0]
    assert depth == 1
    batch, seq, d_model = x_prompt.shape
    dec_batch, dec_seq, _ = x_sample.shape
    n_heads_a, head_dim = cache_k.shape[3:]
    w_a = n_heads_a * head_dim
    assert head_dim == HEAD_DIM and state_s.shape[3:] == (HEAD_DIM, HEAD_DIM)
    assert dec_seq <= SAMPLE_PAD and seq % SB_TILE == 0 and seq % (HGRN_CHUNK * HGRN_CHUNKS_PER_STEP) == 0

    w_in_b = w_in[0].astype(BF16)
    w_out_b = w_out[0].astype(BF16)
    nw = norm_w[0][None, :]
    fw = final_norm_w[None, :]
    ga = gain_a[0][None, :]
    gb = gain_b[0][None, :]

    xp = x_prompt.reshape(batch * seq, d_model)
    xs = jnp.pad(x_sample, ((0, 0), (0, SAMPLE_PAD - dec_seq), (0, 0))).reshape(dec_batch * SAMPLE_PAD, d_model)
    p3, k_p, v_p, p3s = _in_proj(xp, xs, nw, w_in_b, bm=1024)
    m_a = _sb_prompt(p3, sb_bias, ga, batch, seq)
    m_b, s_p = _hgrn_prompt(p3, lb_logits, gb, batch, seq)
    y_prompt = _out_proj(xp, m_a, m_b, w_out_b, fw, bm=512).reshape(batch, seq, d_model)
    k_p = k_p.reshape(1, batch, seq, n_heads_a, head_dim)
    v_p = v_p.reshape(1, batch, seq, n_heads_a, head_dim)

    rows = lambda c: p3s[c].reshape(dec_batch, SAMPLE_PAD, -1)[:, :dec_seq]
    q_t = rows(COMP_QA).reshape(dec_batch, dec_seq, n_heads_a, head_dim).transpose(0, 2, 1, 3)
    q_rows = (q_t[:, :, None, :, :] * jnp.eye(n_heads_a, dtype=F32)[None, :, :, None, None]).reshape(
        dec_batch, n_heads_a, n_heads_a * dec_seq, head_dim).astype(BF16)
    bias_rows = jnp.broadcast_to(jnp.repeat(sb_bias[0], dec_seq)[:, None], (n_heads_a * dec_seq, head_dim))
    to_rows = lambda a: a.reshape(dec_batch, dec_seq, n_heads_a, head_dim).transpose(0, 2, 1, 3).reshape(
        dec_batch, n_heads_a * dec_seq, head_dim)
    z_r = to_rows(rows(COMP_ZA))
    gain_r = jnp.repeat(gain_a[0].reshape(n_heads_a, head_dim), dec_seq, axis=0)
    ck = cache_k[0].reshape(cache_k.shape[1], cache_k.shape[2] * n_heads_a, head_dim)
    cv = cache_v[0].reshape(cache_v.shape[1], cache_v.shape[2] * n_heads_a, head_dim)
    o_r = _sb_decode(page_table, q_rows, bias_rows, p3s, ck, cv, z_r, gain_r, dec_seq)
    m_a_s = o_r.reshape(dec_batch, n_heads_a, dec_seq, head_dim).transpose(0, 2, 1, 3).reshape(
        dec_batch, dec_seq, w_a)
    m_a_s = jnp.pad(m_a_s, ((0, 0), (0, SAMPLE_PAD - dec_seq), (0, 0))).reshape(dec_batch * SAMPLE_PAD, w_a)
    m_b_s, s_s = _hgrn_decode(p3s, lb_logits, gb, state_s[0], dec_seq)
    y_s = _out_proj(xs, m_a_s, m_b_s, w_out_b, fw, bm=dec_batch * SAMPLE_PAD)
    y_sample = y_s.reshape(dec_batch, SAMPLE_PAD, d_model)[:, :dec_seq]
    k_s = rows(COMP_KA).reshape(1, dec_batch, dec_seq, n_heads_a, head_dim)
    v_s = rows(COMP_VA).reshape(1, dec_batch, dec_seq, n_heads_a, head_dim)

    return (y_prompt, y_sample, k_p, v_p, s_p[None], k_s, v_s, s_s[None])
```

```python
import functools

import jax
import jax.numpy as jnp
from jax import lax
from jax.experimental import pallas as pl
from jax.experimental.pallas import tpu as pltpu

EPS = 1e-6
HEAD_DIM = 128
N_COMP = 8
COMP_QA, COMP_KA, COMP_VA, COMP_ZA, COMP_QB, COMP_FB, COMP_IB, COMP_ZB = range(N_COMP)
SB_BLOCK = 128
SB_TILE = 512
LOG2E = 1.4426950408889634
HGRN_CHUNK = 64
HGRN_CHUNKS_PER_STEP = 8
CUMSUM_GROUP = 256
HGRN_SUB = 16
SAMPLE_PAD = 8
DECODE_PAGES_PER_STEP = 8
DECODE_SLOTS = 3
VMEM_LIMIT = 60 * 1024 * 1024

F32 = jnp.float32
BF16 = jnp.bfloat16


def _dot(a, b, **kw):
    return jnp.dot(a, b, preferred_element_type=F32, **kw)


def _dot_nt(a, b):
    return lax.dot_general(a, b, (((1,), (1,)), ((), ())), preferred_element_type=F32)


def _dot_tn(a, b):
    return lax.dot_general(a, b, (((0,), (0,)), ((), ())), preferred_element_type=F32)


def _sigmoid(x):
    return 1.0 / (1.0 + jnp.exp(-x))


def _softplus2(z2):
    return jnp.maximum(z2, 0.0) + jnp.log2(1.0 + jnp.exp2(-jnp.abs(z2)))


def _split_bf16(x):
    hi = x.astype(BF16)
    lo = (x - hi.astype(F32)).astype(BF16)
    return hi, lo


def _in_proj_kernel(x_ref, xs_ref, nw_ref, w_ref, o_ref, os_ref, h_scr):
    m, n = pl.program_id(0), pl.program_id(1)

    def normed(x):
        ms = jnp.mean(x * x, axis=-1, keepdims=True)
        return (x * lax.rsqrt(ms + EPS) * nw_ref[...]).astype(BF16)

    @pl.when(n == 0)
    def _():
        h_scr[...] = normed(x_ref[...])

    @pl.when(m == 0)
    def _():
        os_ref[0] = _dot(normed(xs_ref[...]), w_ref[...].astype(BF16))

    o_ref[0] = _dot(h_scr[...], w_ref[...].astype(BF16))


def _in_proj(x, xs, norm_w, w, bm):
    t, d = x.shape
    ts = xs.shape[0]
    d_in = w.shape[1]
    bn = d_in // N_COMP
    return pl.pallas_call(
        _in_proj_kernel,
        grid=(t // bm, N_COMP),
        in_specs=[
            pl.BlockSpec((bm, d), lambda m, n: (m, 0)),
            pl.BlockSpec((ts, d), lambda m, n: (0, 0)),
            pl.BlockSpec((1, d), lambda m, n: (0, 0)),
            pl.BlockSpec((d, bn), lambda m, n: (0, n)),
        ],
        out_specs=[
            pl.BlockSpec((1, bm, bn), lambda m, n: (n, m, 0)),
            pl.BlockSpec((1, ts, bn), lambda m, n: (jnp.where(m == 0, n, N_COMP - 1), 0, 0)),
        ],
        out_shape=[
            jax.ShapeDtypeStruct((N_COMP, t, bn), F32),
            jax.ShapeDtypeStruct((N_COMP, ts, bn), F32),
        ],
        scratch_shapes=[pltpu.VMEM((bm, d), BF16)],
        compiler_params=pltpu.CompilerParams(
            dimension_semantics=("arbitrary", "arbitrary"), vmem_limit_bytes=VMEM_LIMIT),
        name="in_proj",
    )(x, xs, norm_w, w)


def _head_norm_gate(o, gain, z):
    ms = jnp.mean(o * o, axis=-1, keepdims=True)
    return o * lax.rsqrt(ms + EPS) * gain * (z * _sigmoid(z))


def _sb_prompt_kernel(bias_ref, q_ref, k_ref, v_ref, z_ref, gain_ref, mu_ref, o_ref, ko_ref, vo_ref,
                      kb_scr, vb_scr, acc_scr, carry_scr):
    tq = SB_TILE
    bk = SB_BLOCK
    t_len = q_ref.shape[1]
    bias2 = bias_ref[0, pl.program_id(1)] * LOG2E
    qscale = HEAD_DIM ** -0.5 * LOG2E
    kb_scr[...] = k_ref[0].astype(BF16)
    vb_scr[...] = v_ref[0].astype(BF16)
    ko_ref[...] = k_ref[0]
    vo_ref[...] = v_ref[0]
    gain = gain_ref[...]

    def tile(q, k0, tk, masked):
        keys = pl.ds(pl.multiple_of(k0, tq), tk)
        z = _dot_nt(q, kb_scr[keys, :]) + bias2
        sp = _softplus2(z)
        if masked:
            causal = (lax.broadcasted_iota(jnp.int32, (tq, tk), 1) - (tk - tq)
                      < lax.broadcasted_iota(jnp.int32, (tq, tk), 0))
            sp = jnp.where(causal, sp, 0.0)
        zs = z - sp
        hi, lo = _split_bf16(sp)
        carry = carry_scr[...]
        ws = [None] * (tk // bk)
        for c in reversed(range(tk // bk)):
            cols = slice(c * bk, (c + 1) * bk)
            r = _dot(jnp.concatenate([hi[:, cols], lo[:, cols]], axis=1), mu_ref[...])
            ws[c] = jnp.exp2(zs[:, cols] - r[:, :bk] - carry)
            carry = carry + r[:, bk:]
        w = jnp.concatenate(ws, axis=1)
        if masked:
            w = jnp.where(causal, w, 0.0)
        carry_scr[...] = carry
        acc_scr[...] += _dot(w.astype(BF16), vb_scr[keys, :])

    def q_tile(qt, _):
        qs = pl.multiple_of(qt * tq, tq)
        q = (q_ref[0, pl.ds(qs, tq), :] * qscale).astype(BF16)
        acc_scr[...] = jnp.zeros_like(acc_scr)
        carry_scr[...] = jnp.zeros_like(carry_scr)
        @pl.when(qt == 0)
        def _():
            tile(q, 0, tq, True)

        @pl.when(qt > 0)
        def _():
            tile(q, qs - tq, 2 * tq, True)

        def k_pair(jj, _):
            tile(q, qs - tq - (jj + 1) * 2 * tq, 2 * tq, False)
            return 0

        lax.fori_loop(0, jnp.maximum(qt - 1, 0) // 2, k_pair, 0)

        @pl.when((qt > 0) & (qt % 2 == 0))
        def _():
            tile(q, 0, tq, False)

        zg = z_ref[0, pl.ds(qs, tq), :]
        o_ref[pl.ds(qs, tq), :] = _head_norm_gate(acc_scr[...], gain, zg).astype(o_ref.dtype)
        return 0

    lax.fori_loop(0, t_len // tq, q_tile, 0)


def _sb_matrix(n, pieces):
    j = jnp.arange(pieces * n)[:, None] % n
    s = jnp.arange(2 * n)[None, :]
    return jnp.where(s < n, (j > s), True).astype(BF16)


def _sb_prompt(p3, sb_bias, gain_a, batch, seq):
    n_heads = p3.shape[2] // HEAD_DIM
    mu = _sb_matrix(SB_BLOCK, 2)
    comp = lambda c: pl.BlockSpec((1, seq, HEAD_DIM), lambda b, h: (c, b, h))
    return pl.pallas_call(
        _sb_prompt_kernel,
        grid=(batch, n_heads),
        in_specs=[
            pl.BlockSpec(memory_space=pltpu.SMEM),
            comp(COMP_QA), comp(COMP_KA), comp(COMP_VA), comp(COMP_ZA),
            pl.BlockSpec((1, HEAD_DIM), lambda b, h: (0, h)),
            pl.BlockSpec(mu.shape, lambda b, h: (0, 0)),
        ],
        out_specs=[pl.BlockSpec((seq, HEAD_DIM), lambda b, h: (b, h))] * 3,
        out_shape=[
            jax.ShapeDtypeStruct((batch * seq, n_heads * HEAD_DIM), BF16),
            jax.ShapeDtypeStruct((batch * seq, n_heads * HEAD_DIM), p3.dtype),
            jax.ShapeDtypeStruct((batch * seq, n_heads * HEAD_DIM), p3.dtype),
        ],
        scratch_shapes=[pltpu.VMEM((seq, HEAD_DIM), BF16), pltpu.VMEM((seq, HEAD_DIM), BF16),
                        pltpu.VMEM((SB_TILE, HEAD_DIM), F32), pltpu.VMEM((SB_TILE, SB_BLOCK), F32)],
        compiler_params=pltpu.CompilerParams(
            dimension_semantics=("arbitrary", "arbitrary"), vmem_limit_bytes=VMEM_LIMIT),
        name="sb_prompt",
    )(sb_bias, p3, p3, p3, p3, gain_a, mu)


def _lower_bound(l):
    e = jnp.exp(l - jnp.max(l, axis=0, keepdims=True))
    return e[0:1, :] / jnp.sum(e, axis=0, keepdims=True)


def _chunk_cumsum(x, c_len):
    rows = x.shape[0]
    group = min(rows, CUMSUM_GROUP)
    ti = lax.broadcasted_iota(jnp.int32, (group, group), 0)
    si = lax.broadcasted_iota(jnp.int32, (group, group), 1)
    tri = jnp.where((si <= ti) & (si // c_len == ti // c_len), 1.0, 0.0).astype(BF16)
    hi = x.astype(BF16)
    rest = x - hi.astype(F32)
    mid = rest.astype(BF16)
    lo = (rest - mid.astype(F32)).astype(BF16)
    pieces = jnp.concatenate([hi, mid, lo], axis=1)
    outs = []
    for g0 in range(0, rows, group):
        r = _dot(tri, pieces[g0:g0 + group])
        outs.append((r[:, :HEAD_DIM] + r[:, HEAD_DIM:2 * HEAD_DIM]) + r[:, 2 * HEAD_DIM:])
    return jnp.concatenate(outs, axis=0)


def _hgrn_chunks(qb, fb, ib, lb, st, c_len, n_valid=None):
    rows = qb.shape[0]
    sub = min(HGRN_SUB, c_len)
    q = qb * _sigmoid(qb)
    g = lb + (1.0 - lb) * _sigmoid(fb)
    kk = 1.0 - g
    lg = jnp.log(g)
    if n_valid is not None:
        live = lax.broadcasted_iota(jnp.int32, (rows, HEAD_DIM), 0) < n_valid
        kk = jnp.where(live, kk, 0.0)
        lg = jnp.where(live, lg, 0.0)
    yield
    b = _chunk_cumsum(lg, c_len)
    vb = ib.astype(BF16)
    chunks = [slice(c * c_len, (c + 1) * c_len) for c in range(rows // c_len)]
    spans, operands = [], []
    for ch in chunks:
        for lo_r in range(ch.start, ch.stop, sub):
            hi_r = lo_r + sub
            m = b[lo_r + sub // 2:lo_r + sub // 2 + 1, :]
            qi = (q[lo_r:hi_r] * jnp.exp(b[lo_r:hi_r] - m)).astype(BF16)
            ki = (kk[ch.start:hi_r] * jnp.exp(m - b[ch.start:hi_r])).astype(BF16)
            spans.append((ch.start, lo_r, hi_r))
            operands.append((qi, ki))
    yield
    atts = [_dot_nt(qi, ki) for qi, ki in operands]
    masked = []
    for att, (c0, lo_r, hi_r) in zip(atts, spans):
        t_pos = lax.broadcasted_iota(jnp.int32, att.shape, 0) + (lo_r - c0)
        s_pos = lax.broadcasted_iota(jnp.int32, att.shape, 1)
        masked.append(jnp.where(s_pos <= t_pos, att, 0.0).astype(BF16))
    yield
    o_intra = jnp.concatenate([_dot(att, vb[c0:hi_r]) for att, (c0, _, hi_r) in zip(masked, spans)], axis=0)
    lasts = [b[ch.stop - 1:ch.stop, :] for ch in chunks]
    decayed = [(kk[ch] * jnp.exp(bl - b[ch])).astype(BF16) for ch, bl in zip(chunks, lasts)]
    yield
    incs = [_dot_tn(vb[ch], kd) for ch, kd in zip(chunks, decayed)]
    states = [st]
    for inc, bl in zip(incs, lasts):
        states.append(states[-1] * jnp.exp(bl) + inc)
    starts = [((q[ch] * jnp.exp(b[ch])).astype(BF16), s.astype(BF16)) for ch, s in zip(chunks, states)]
    yield
    o_inter = [_dot_nt(qe, s) for qe, s in starts]
    return o_intra + jnp.concatenate(o_inter, axis=0), states[-1]


def _in_lockstep(generators):
    values = [None] * len(generators)
    running = dict(enumerate(generators))
    while running:
        for i, gen in list(running.items()):
            try:
                next(gen)
            except StopIteration as done:
                values[i] = done.value
                del running[i]
    return values


def _hgrn_prompt_kernel(lbl_ref, q_ref, f_ref, i_ref, z_ref, gain_ref, o_ref, s_ref):
    r_len = HGRN_CHUNK * HGRN_CHUNKS_PER_STEP
    lb = _lower_bound(lbl_ref[...])
    gain = gain_ref[...]

    def step(ci, st):
        rs = pl.multiple_of(ci * r_len, r_len)
        rows = pl.ds(rs, r_len)
        (o, st), = _in_lockstep(
            [_hgrn_chunks(q_ref[0, rows, :], f_ref[0, rows, :], i_ref[0, rows, :], lb, st, HGRN_CHUNK)])
        o_ref[rows, :] = _head_norm_gate(o, gain, z_ref[0, rows, :]).astype(o_ref.dtype)
        return st

    st = lax.fori_loop(0, q_ref.shape[1] // r_len, step, jnp.zeros((HEAD_DIM, HEAD_DIM), F32))
    s_ref[0, 0] = st.T


def _hgrn_prompt(p3, lb_logits, gain_b, batch, seq):
    n_heads = p3.shape[2] // HEAD_DIM
    comp = lambda c: pl.BlockSpec((1, seq, HEAD_DIM), lambda b, h: (c, b, h))
    return pl.pallas_call(
        _hgrn_prompt_kernel,
        grid=(batch, n_heads),
        in_specs=[
            pl.BlockSpec((lb_logits.shape[0], HEAD_DIM), lambda b, h: (0, h)),
            comp(COMP_QB), comp(COMP_FB), comp(COMP_IB), comp(COMP_ZB),
            pl.BlockSpec((1, HEAD_DIM), lambda b, h: (0, h)),
        ],
        out_specs=[
            pl.BlockSpec((seq, HEAD_DIM), lambda b, h: (b, h)),
            pl.BlockSpec((1, 1, HEAD_DIM, HEAD_DIM), lambda b, h: (b, h, 0, 0)),
        ],
        out_shape=[
            jax.ShapeDtypeStruct((batch * seq, n_heads * HEAD_DIM), BF16),
            jax.ShapeDtypeStruct((batch, n_heads, HEAD_DIM, HEAD_DIM), F32),
        ],
        compiler_params=pltpu.CompilerParams(
            dimension_semantics=("arbitrary", "arbitrary"), vmem_limit_bytes=VMEM_LIMIT),
        name="hgrn_prompt",
    )(lb_logits, p3, p3, p3, p3, gain_b)


def _hgrn_decode_kernel(lbl_ref, q_ref, f_ref, i_ref, z_ref, gain_ref, s0_ref, o_ref, s_ref, *, n_valid):
    n_heads = s0_ref.shape[1]
    head = lambda h: slice(h * HEAD_DIM, (h + 1) * HEAD_DIM)
    lb = _lower_bound(lbl_ref[...])
    results = _in_lockstep([
        _hgrn_chunks(q_ref[0, :, head(h)], f_ref[0, :, head(h)], i_ref[0, :, head(h)], lb[:, head(h)],
                     s0_ref[0, h].T, SAMPLE_PAD, n_valid=n_valid)
        for h in range(n_heads)])
    for h, (o, st) in enumerate(results):
        o_ref[:, head(h)] = _head_norm_gate(o, gain_ref[:, head(h)], z_ref[0, :, head(h)]).astype(o_ref.dtype)
        s_ref[0, h] = st.T


def _hgrn_decode(p3s, lb_logits, gain_b, state, n_valid):
    dec_batch, n_heads = state.shape[:2]
    width = n_heads * HEAD_DIM
    comp = lambda c: pl.BlockSpec((1, SAMPLE_PAD, width), lambda b: (c, b, 0))
    state_spec = pl.BlockSpec((1, n_heads, HEAD_DIM, HEAD_DIM), lambda b: (b, 0, 0, 0))
    return pl.pallas_call(
        functools.partial(_hgrn_decode_kernel, n_valid=n_valid),
        grid=(dec_batch,),
        in_specs=[
            pl.BlockSpec(lb_logits.shape, lambda b: (0, 0)),
            comp(COMP_QB), comp(COMP_FB), comp(COMP_IB), comp(COMP_ZB),
            pl.BlockSpec((1, width), lambda b: (0, 0)),
            state_spec,
        ],
        out_specs=[pl.BlockSpec((SAMPLE_PAD, width), lambda b: (b, 0)), state_spec],
        out_shape=[
            jax.ShapeDtypeStruct((dec_batch * SAMPLE_PAD, width), BF16),
            jax.ShapeDtypeStruct(state.shape, F32),
        ],
        compiler_params=pltpu.CompilerParams(dimension_semantics=("arbitrary",)),
        name="hgrn_decode",
    )(lb_logits, p3s, p3s, p3s, p3s, gain_b, state)


def _sb_decode_kernel(pt_ref, qr_ref, bias_ref, kn_ref, vn_ref, *rest, n_valid, n_pg):
    ck_hbm, cv_hbm, mu_ref, z_ref, gain_ref, o_ref, acc_scr, carry_scr, kbuf, vbuf, sems = rest
    j = pl.program_id(1)
    n_steps = pl.num_programs(1)
    step = pl.program_id(0) * n_steps + j
    n_slots = kbuf.shape[0]
    n_heads, n_rows = acc_scr.shape[:2]
    n_q = n_rows // n_heads
    page = kbuf.shape[2] // n_heads

    def page_copies(t):
        seq, js, slot = t // n_steps, t % n_steps, t % n_slots
        copies = []
        for i in range(n_pg):
            src = pt_ref[seq, n_steps * n_pg - 1 - (js * n_pg + i)]
            copies.append(pltpu.make_async_copy(ck_hbm.at[src], kbuf.at[slot, i], sems.at[slot, 0, i]))
            copies.append(pltpu.make_async_copy(cv_hbm.at[src], vbuf.at[slot, i], sems.at[slot, 1, i]))
        return copies

    @pl.when(step == 0)
    def _():
        for t in range(n_slots - 1):
            for copy in page_copies(t):
                copy.start()

    @pl.when(step + n_slots - 1 < pl.num_programs(0) * n_steps)
    def _():
        for copy in page_copies(step + n_slots - 1):
            copy.start()

    zscale = HEAD_DIM ** -0.5 * LOG2E
    bias2 = bias_ref[...] * LOG2E
    head = lambda h: slice(h * HEAD_DIM, (h + 1) * HEAD_DIM)

    def scores(k_heads):
        z = sum(_dot_nt(qr_ref[0, h], k_heads[h].astype(BF16)) for h in range(n_heads))
        return z * zscale + bias2[:, :z.shape[1]]

    @pl.when(j == 0)
    def _():
        r = kn_ref.shape[1]
        z = scores([kn_ref[0, :, head(h)] for h in range(n_heads)])
        t_row = lax.broadcasted_iota(jnp.int32, (n_rows, r), 0) % n_q
        s_col = lax.broadcasted_iota(jnp.int32, (n_rows, r), 1)
        valid = (s_col < t_row) & (s_col < n_valid)
        sp = jnp.where(valid, _softplus2(z), 0.0)
        later_mat = jnp.where(lax.broadcasted_iota(jnp.int32, (r, r), 0)
                              > lax.broadcasted_iota(jnp.int32, (r, r), 1), 1.0, 0.0)
        later = _dot(sp, later_mat, precision=lax.Precision.HIGHEST)
        wb = jnp.where(valid, jnp.exp2(z - sp - later), 0.0).astype(BF16)
        carry_scr[...] = jnp.broadcast_to(jnp.sum(sp, axis=1, keepdims=True), carry_scr.shape)
        for h in range(n_heads):
            acc_scr[h] = _dot(wb, vn_ref[0, :, head(h)].astype(BF16))

    for copy in page_copies(step):
        copy.wait()
    slot = step % n_slots
    kp_refs = [kbuf.at[slot, i] for i in range(n_pg)]
    vp_refs = [vbuf.at[slot, i] for i in range(n_pg)]
    zs = [scores([kp_ref[pl.ds(h, page, stride=n_heads), :] for h in range(n_heads)]) for kp_ref in kp_refs]
    sps = [_softplus2(z) for z in zs]
    rs = [_dot(jnp.concatenate(_split_bf16(sp), axis=1), mu_ref[...]) for sp in sps]
    carry = carry_scr[...]
    wbs = []
    for z, sp, r in zip(zs, sps, rs):
        wbs.append(jnp.exp2(z - sp - r[:, :page] - carry).astype(BF16))
        carry = carry + r[:, page:]
    carry_scr[...] = carry
    for h in range(n_heads):
        acc_scr[h] += sum(_dot(wb, vp_ref[pl.ds(h, page, stride=n_heads), :].astype(BF16))
                          for wb, vp_ref in zip(wbs, vp_refs))

    @pl.when(j == pl.num_programs(1) - 1)
    def _():
        rh = lax.broadcasted_iota(jnp.int32, (n_rows, HEAD_DIM), 0) // n_q
        o = jnp.zeros((n_rows, HEAD_DIM), F32)
        for h in range(n_heads):
            o = o + jnp.where(rh == h, acc_scr[h], 0.0)
        o_ref[0] = _head_norm_gate(o, gain_ref[...], z_ref[0]).astype(o_ref.dtype)


def _sb_decode(page_table, q_rows, bias_rows, p3s, cache_k, cache_v, z_r, gain_r, n_valid):
    dec_batch, n_pages = page_table.shape
    page_rows = cache_k.shape[1]
    n_heads, n_rows = q_rows.shape[1:3]
    width = n_heads * HEAD_DIM
    page = page_rows // n_heads
    assert page == HEAD_DIM
    mu = _sb_matrix(page, 2)
    new = lambda c: pl.BlockSpec((1, SAMPLE_PAD, width), lambda b, j, pt: (c, b, 0))
    n_pg = DECODE_PAGES_PER_STEP
    assert n_pages % n_pg == 0 and dec_batch * (n_pages // n_pg) >= DECODE_SLOTS - 1
    page_buf = pltpu.VMEM((DECODE_SLOTS, n_pg, page_rows, HEAD_DIM), cache_k.dtype)
    grid_spec = pltpu.PrefetchScalarGridSpec(
        num_scalar_prefetch=1,
        grid=(dec_batch, n_pages // n_pg),
        in_specs=[
            pl.BlockSpec((1, n_heads, n_rows, HEAD_DIM), lambda b, j, pt: (b, 0, 0, 0)),
            pl.BlockSpec((n_rows, HEAD_DIM), lambda b, j, pt: (0, 0)),
            new(COMP_KA), new(COMP_VA),
            pl.BlockSpec(memory_space=pl.ANY), pl.BlockSpec(memory_space=pl.ANY),
            pl.BlockSpec(mu.shape, lambda b, j, pt: (0, 0)),
            pl.BlockSpec((1, n_rows, HEAD_DIM), lambda b, j, pt: (b, 0, 0)),
            pl.BlockSpec((n_rows, HEAD_DIM), lambda b, j, pt: (0, 0)),
        ],
        out_specs=pl.BlockSpec((1, n_rows, HEAD_DIM), lambda b, j, pt: (b, 0, 0)),
        scratch_shapes=[pltpu.VMEM((n_heads, n_rows, HEAD_DIM), F32), pltpu.VMEM((n_rows, HEAD_DIM), F32),
                        page_buf, page_buf, pltpu.SemaphoreType.DMA((DECODE_SLOTS, 2, n_pg))],
    )
    return pl.pallas_call(
        functools.partial(_sb_decode_kernel, n_valid=n_valid, n_pg=n_pg),
        grid_spec=grid_spec,
        out_shape=jax.ShapeDtypeStruct((dec_batch, n_rows, HEAD_DIM), BF16),
        compiler_params=pltpu.CompilerParams(
            dimension_semantics=("arbitrary", "arbitrary"), vmem_limit_bytes=VMEM_LIMIT),
        name="sb_decode",
    )(page_table, q_rows, bias_rows, p3s, p3s, cache_k, cache_v, mu, z_r, gain_r)


def _out_proj_kernel(x_ref, ma_ref, mb_ref, wa_ref, wb_ref, fw_ref, o_ref):
    y = x_ref[...] + _dot(ma_ref[...], wa_ref[...].astype(BF16)) + _dot(mb_ref[...], wb_ref[...].astype(BF16))
    ms = jnp.mean(y * y, axis=-1, keepdims=True)
    o_ref[...] = y * lax.rsqrt(ms + EPS) * fw_ref[...]


def _out_proj(x, m_a, m_b, w, final_w, bm):
    t, d = x.shape
    wa = m_a.shape[1]
    wb = m_b.shape[1]
    assert wa == wb
    return pl.pallas_call(
        _out_proj_kernel,
        grid=(t // bm,),
        in_specs=[
            pl.BlockSpec((bm, d), lambda m: (m, 0)),
            pl.BlockSpec((bm, wa), lambda m: (m, 0)),
            pl.BlockSpec((bm, wb), lambda m: (m, 0)),
            pl.BlockSpec((wa, d), lambda m: (0, 0)),
            pl.BlockSpec((wb, d), lambda m: (1, 0)),
            pl.BlockSpec((1, d), lambda m: (0, 0)),
        ],
        out_specs=pl.BlockSpec((bm, d), lambda m: (m, 0)),
        out_shape=jax.ShapeDtypeStruct((t, d), F32),
        compiler_params=pltpu.CompilerParams(
            dimension_semantics=("arbitrary",), vmem_limit_bytes=VMEM_LIMIT),
        name="out_proj",
    )(x, m_a, m_b, w, w, final_w)


def kernel(x_prompt, x_sample, cache_k, cache_v, state_s, page_table, norm_w, w_in, gain_a, gain_b,
           sb_bias, lb_logits, w_out, final_norm_w):
    depth = norm_w.shape[0]
    assert depth == 1
    batch, seq, d_model = x_prompt.shape
    dec_batch, dec_seq, _ = x_sample.shape
    n_heads_a, head_dim = cache_k.shape[3:]
    w_a = n_heads_a * head_dim
    assert head_dim == HEAD_DIM and state_s.shape[3:] == (HEAD_DIM, HEAD_DIM)
    assert dec_seq <= SAMPLE_PAD and seq % SB_TILE == 0 and seq % (HGRN_CHUNK * HGRN_CHUNKS_PER_STEP) == 0

    nw = norm_w[0][None, :]
    fw = final_norm_w[None, :]
    ga = gain_a[0][None, :]
    gb = gain_b[0][None, :]

    xp = x_prompt.reshape(batch * seq, d_model)
    xs = jnp.pad(x_sample, ((0, 0), (0, SAMPLE_PAD - dec_seq), (0, 0))).reshape(dec_batch * SAMPLE_PAD, d_model)
    p3, p3s = _in_proj(xp, xs, nw, w_in[0], bm=1024)
    m_a, k_p, v_p = _sb_prompt(p3, sb_bias, ga, batch, seq)
    m_b, s_p = _hgrn_prompt(p3, lb_logits, gb, batch, seq)
    y_prompt = _out_proj(xp, m_a, m_b, w_out[0], fw, bm=512).reshape(batch, seq, d_model)
    k_p = k_p.reshape(1, batch, seq, n_heads_a, head_dim)
    v_p = v_p.reshape(1, batch, seq, n_heads_a, head_dim)

    rows = lambda c: p3s[c].reshape(dec_batch, SAMPLE_PAD, -1)[:, :dec_seq]
    q_t = rows(COMP_QA).reshape(dec_batch, dec_seq, n_heads_a, head_dim).transpose(0, 2, 1, 3)
    q_rows = (q_t[:, :, None, :, :] * jnp.eye(n_heads_a, dtype=F32)[None, :, :, None, None]).reshape(
        dec_batch, n_heads_a, n_heads_a * dec_seq, head_dim).astype(BF16)
    bias_rows = jnp.broadcast_to(jnp.repeat(sb_bias[0], dec_seq)[:, None], (n_heads_a * dec_seq, head_dim))
    to_rows = lambda a: a.reshape(dec_batch, dec_seq, n_heads_a, head_dim).transpose(0, 2, 1, 3).reshape(
        dec_batch, n_heads_a * dec_seq, head_dim)
    z_r = to_rows(rows(COMP_ZA))
    gain_r = jnp.repeat(gain_a[0].reshape(n_heads_a, head_dim), dec_seq, axis=0)
    ck = cache_k[0].reshape(cache_k.shape[1], cache_k.shape[2] * n_heads_a, head_dim)
    cv = cache_v[0].reshape(cache_v.shape[1], cache_v.shape[2] * n_heads_a, head_dim)
    o_r = _sb_decode(page_table, q_rows, bias_rows, p3s, ck, cv, z_r, gain_r, dec_seq)
    m_a_s = o_r.reshape(dec_batch, n_heads_a, dec_seq, head_dim).transpose(0, 2, 1, 3).reshape(
        dec_batch, dec_seq, w_a)
    m_a_s = jnp.pad(m_a_s, ((0, 0), (0, SAMPLE_PAD - dec_seq), (0, 0))).reshape(dec_batch * SAMPLE_PAD, w_a)
    m_b_s, s_s = _hgrn_decode(p3s, lb_logits, gb, state_s[0], dec_seq)
    y_s = _out_proj(xs, m_a_s, m_b_s, w_out[0], fw, bm=dec_batch * SAMPLE_PAD)
    y_sample = y_s.reshape(dec_batch, SAMPLE_PAD, d_model)[:, :dec_seq]
    k_s = rows(COMP_KA).reshape(1, dec_batch, dec_seq, n_heads_a, head_dim)
    v_s = rows(COMP_VA).reshape(1, dec_batch, dec_seq, n_heads_a, head_dim)

    return (y_prompt, y_sample, k_p, v_p, s_p[None], k_s, v_s, s_s[None])
```

```python
import functools

import jax
import jax.numpy as jnp
from jax import lax
from jax.experimental import pallas as pl
from jax.experimental.pallas import tpu as pltpu

EPS = 1e-6
HEAD_DIM = 128
N_COMP = 8
COMP_QA, COMP_KA, COMP_VA, COMP_ZA, COMP_QB, COMP_FB, COMP_IB, COMP_ZB = range(N_COMP)
IN_PROJ_ROWS = 1024
OUT_PROJ_ROWS = 512
SB_BLOCK = 128
SB_TILE = 512
LOG2E = 1.4426950408889634
HGRN_CHUNK = 64
HGRN_CHUNKS_PER_STEP = 16
CUMSUM_GROUP = 256
HGRN_SUB = 16
SAMPLE_PAD = 8
DECODE_PAGES_PER_STEP = 8
DECODE_SLOTS = 3
VMEM_LIMIT = 60 * 1024 * 1024

F32 = jnp.float32
BF16 = jnp.bfloat16


def _dot(a, b, **kw):
    return jnp.dot(a, b, preferred_element_type=F32, **kw)


def _dot_nt(a, b):
    return lax.dot_general(a, b, (((1,), (1,)), ((), ())), preferred_element_type=F32)


def _dot_tn(a, b):
    return lax.dot_general(a, b, (((0,), (0,)), ((), ())), preferred_element_type=F32)


def _sigmoid(x):
    return 1.0 / (1.0 + jnp.exp(-x))


def _softplus2(z2):
    return jnp.maximum(z2, 0.0) + jnp.log2(1.0 + jnp.exp2(-jnp.abs(z2)))


def _split_bf16(x):
    hi = x.astype(BF16)
    lo = (x - hi.astype(F32)).astype(BF16)
    return hi, lo


def _in_proj_kernel(x_ref, xs_ref, nw_ref, w_ref, o_ref, os_ref, h_scr):
    m, n = pl.program_id(0), pl.program_id(1)

    def normed(x):
        ms = jnp.mean(x * x, axis=-1, keepdims=True)
        return (x * lax.rsqrt(ms + EPS) * nw_ref[...]).astype(BF16)

    @pl.when(n == 0)
    def _():
        h_scr[...] = normed(x_ref[...])

    @pl.when(m == 0)
    def _():
        os_ref[0] = _dot(normed(xs_ref[...]), w_ref[...].astype(BF16))

    o_ref[0] = _dot(h_scr[...], w_ref[...].astype(BF16))


def _in_proj(x, xs, norm_w, w, bm):
    t, d = x.shape
    ts = xs.shape[0]
    d_in = w.shape[1]
    bn = d_in // N_COMP
    return pl.pallas_call(
        _in_proj_kernel,
        grid=(t // bm, N_COMP),
        in_specs=[
            pl.BlockSpec((bm, d), lambda m, n: (m, 0)),
            pl.BlockSpec((ts, d), lambda m, n: (0, 0)),
            pl.BlockSpec((1, d), lambda m, n: (0, 0)),
            pl.BlockSpec((d, bn), lambda m, n: (0, n)),
        ],
        out_specs=[
            pl.BlockSpec((1, bm, bn), lambda m, n: (n, m, 0)),
            pl.BlockSpec((1, ts, bn), lambda m, n: (jnp.where(m == 0, n, N_COMP - 1), 0, 0)),
        ],
        out_shape=[
            jax.ShapeDtypeStruct((N_COMP, t, bn), F32),
            jax.ShapeDtypeStruct((N_COMP, ts, bn), F32),
        ],
        scratch_shapes=[pltpu.VMEM((bm, d), BF16)],
        compiler_params=pltpu.CompilerParams(
            dimension_semantics=("arbitrary", "arbitrary"), vmem_limit_bytes=VMEM_LIMIT),
        name="in_proj",
    )(x, xs, norm_w, w)


def _head_norm_gate(o, gain, z):
    ms = jnp.mean(o * o, axis=-1, keepdims=True)
    return o * lax.rsqrt(ms + EPS) * gain * (z * _sigmoid(z))


def _sb_prompt_kernel(bias_ref, q_ref, k_ref, v_ref, z_ref, gain_ref, mu_ref, o_ref, ko_ref, vo_ref,
                      kb_scr, vb_scr, acc_scr, carry_scr):
    tq = SB_TILE
    bk = SB_BLOCK
    t_len = q_ref.shape[1]
    bias2 = bias_ref[0, pl.program_id(1)] * LOG2E
    qscale = HEAD_DIM ** -0.5 * LOG2E
    kb_scr[...] = k_ref[0].astype(BF16)
    vb_scr[...] = v_ref[0].astype(BF16)
    ko_ref[...] = k_ref[0]
    vo_ref[...] = v_ref[0]
    gain = gain_ref[...]

    def tile(q, k0, tk, masked):
        keys = pl.ds(pl.multiple_of(k0, tq), tk)
        z = _dot_nt(q, kb_scr[keys, :]) + bias2
        sp = _softplus2(z)
        if masked:
            causal = (lax.broadcasted_iota(jnp.int32, (tq, tk), 1) - (tk - tq)
                      < lax.broadcasted_iota(jnp.int32, (tq, tk), 0))
            sp = jnp.where(causal, sp, 0.0)
        zs = z - sp
        hi, lo = _split_bf16(sp)
        carry = carry_scr[...]
        ws = [None] * (tk // bk)
        for c in reversed(range(tk // bk)):
            cols = slice(c * bk, (c + 1) * bk)
            r = _dot(jnp.concatenate([hi[:, cols], lo[:, cols]], axis=1), mu_ref[...])
            ws[c] = jnp.exp2(zs[:, cols] - r[:, :bk] - carry)
            carry = carry + r[:, bk:]
        w = jnp.concatenate(ws, axis=1)
        if masked:
            w = jnp.where(causal, w, 0.0)
        carry_scr[...] = carry
        acc_scr[...] += _dot(w.astype(BF16), vb_scr[keys, :])

    def q_tile(qt, _):
        qs = pl.multiple_of(qt * tq, tq)
        q = (q_ref[0, pl.ds(qs, tq), :] * qscale).astype(BF16)
        acc_scr[...] = jnp.zeros_like(acc_scr)
        carry_scr[...] = jnp.zeros_like(carry_scr)
        @pl.when(qt == 0)
        def _():
            tile(q, 0, tq, True)

        @pl.when(qt > 0)
        def _():
            tile(q, qs - tq, 2 * tq, True)

        def k_pair(jj, _):
            tile(q, qs - tq - (jj + 1) * 2 * tq, 2 * tq, False)
            return 0

        lax.fori_loop(0, jnp.maximum(qt - 1, 0) // 2, k_pair, 0)

        @pl.when((qt > 0) & (qt % 2 == 0))
        def _():
            tile(q, 0, tq, False)

        zg = z_ref[0, pl.ds(qs, tq), :]
        o_ref[pl.ds(qs, tq), :] = _head_norm_gate(acc_scr[...], gain, zg).astype(o_ref.dtype)
        return 0

    lax.fori_loop(0, t_len // tq, q_tile, 0)


def _sb_matrix(n, pieces):
    j = jnp.arange(pieces * n)[:, None] % n
    s = jnp.arange(2 * n)[None, :]
    return jnp.where(s < n, (j > s), True).astype(BF16)


def _sb_prompt(p3, sb_bias, gain_a, batch, seq):
    n_heads = p3.shape[2] // HEAD_DIM
    mu = _sb_matrix(SB_BLOCK, 2)
    comp = lambda c: pl.BlockSpec((1, seq, HEAD_DIM), lambda b, h: (c, b, h))
    return pl.pallas_call(
        _sb_prompt_kernel,
        grid=(batch, n_heads),
        in_specs=[
            pl.BlockSpec(memory_space=pltpu.SMEM),
            comp(COMP_QA), comp(COMP_KA), comp(COMP_VA), comp(COMP_ZA),
            pl.BlockSpec((1, HEAD_DIM), lambda b, h: (0, h)),
            pl.BlockSpec(mu.shape, lambda b, h: (0, 0)),
        ],
        out_specs=[pl.BlockSpec((seq, HEAD_DIM), lambda b, h: (b, h))] * 3,
        out_shape=[
            jax.ShapeDtypeStruct((batch * seq, n_heads * HEAD_DIM), BF16),
            jax.ShapeDtypeStruct((batch * seq, n_heads * HEAD_DIM), p3.dtype),
            jax.ShapeDtypeStruct((batch * seq, n_heads * HEAD_DIM), p3.dtype),
        ],
        scratch_shapes=[pltpu.VMEM((seq, HEAD_DIM), BF16), pltpu.VMEM((seq, HEAD_DIM), BF16),
                        pltpu.VMEM((SB_TILE, HEAD_DIM), F32), pltpu.VMEM((SB_TILE, SB_BLOCK), F32)],
        compiler_params=pltpu.CompilerParams(
            dimension_semantics=("arbitrary", "arbitrary"), vmem_limit_bytes=VMEM_LIMIT),
        name="sb_prompt",
    )(sb_bias, p3, p3, p3, p3, gain_a, mu)


def _lower_bound(l):
    e = jnp.exp(l - jnp.max(l, axis=0, keepdims=True))
    return e[0:1, :] / jnp.sum(e, axis=0, keepdims=True)


def _chunk_cumsum(x, c_len):
    rows = x.shape[0]
    group = min(rows, CUMSUM_GROUP)
    ti = lax.broadcasted_iota(jnp.int32, (group, group), 0)
    si = lax.broadcasted_iota(jnp.int32, (group, group), 1)
    tri = jnp.where((si <= ti) & (si // c_len == ti // c_len), 1.0, 0.0).astype(BF16)
    hi = x.astype(BF16)
    rest = x - hi.astype(F32)
    mid = rest.astype(BF16)
    lo = (rest - mid.astype(F32)).astype(BF16)
    pieces = jnp.concatenate([hi, mid, lo], axis=1)
    outs = []
    for g0 in range(0, rows, group):
        r = _dot(tri, pieces[g0:g0 + group])
        outs.append((r[:, :HEAD_DIM] + r[:, HEAD_DIM:2 * HEAD_DIM]) + r[:, 2 * HEAD_DIM:])
    return jnp.concatenate(outs, axis=0)


def _hgrn_chunks(qb, fb, ib, lb, st, c_len, n_valid=None):
    rows = qb.shape[0]
    sub = min(HGRN_SUB, c_len)
    q = qb * _sigmoid(qb)
    g = lb + (1.0 - lb) * _sigmoid(fb)
    kk = 1.0 - g
    lg = jnp.log(g)
    if n_valid is not None:
        live = lax.broadcasted_iota(jnp.int32, (rows, HEAD_DIM), 0) < n_valid
        kk = jnp.where(live, kk, 0.0)
        lg = jnp.where(live, lg, 0.0)
    yield
    b = _chunk_cumsum(lg, c_len)
    vb = ib.astype(BF16)
    chunks = [slice(c * c_len, (c + 1) * c_len) for c in range(rows // c_len)]
    spans, operands = [], []
    for ch in chunks:
        for lo_r in range(ch.start, ch.stop, sub):
            hi_r = lo_r + sub
            m = b[lo_r + sub // 2:lo_r + sub // 2 + 1, :]
            qi = (q[lo_r:hi_r] * jnp.exp(b[lo_r:hi_r] - m)).astype(BF16)
            ki = (kk[ch.start:hi_r] * jnp.exp(m - b[ch.start:hi_r])).astype(BF16)
            spans.append((ch.start, lo_r, hi_r))
            operands.append((qi, ki))
    yield
    atts = [_dot_nt(qi, ki) for qi, ki in operands]
    masked = []
    for att, (c0, lo_r, hi_r) in zip(atts, spans):
        t_pos = lax.broadcasted_iota(jnp.int32, att.shape, 0) + (lo_r - c0)
        s_pos = lax.broadcasted_iota(jnp.int32, att.shape, 1)
        masked.append(jnp.where(s_pos <= t_pos, att, 0.0).astype(BF16))
    yield
    o_intra = jnp.concatenate([_dot(att, vb[c0:hi_r]) for att, (c0, _, hi_r) in zip(masked, spans)], axis=0)
    lasts = [b[ch.stop - 1:ch.stop, :] for ch in chunks]
    decayed = [(kk[ch] * jnp.exp(bl - b[ch])).astype(BF16) for ch, bl in zip(chunks, lasts)]
    yield
    incs = [_dot_tn(vb[ch], kd) for ch, kd in zip(chunks, decayed)]
    states = [st]
    for inc, bl in zip(incs, lasts):
        states.append(states[-1] * jnp.exp(bl) + inc)
    starts = [((q[ch] * jnp.exp(b[ch])).astype(BF16), s.astype(BF16)) for ch, s in zip(chunks, states)]
    yield
    o_inter = [_dot_nt(qe, s) for qe, s in starts]
    return o_intra + jnp.concatenate(o_inter, axis=0), states[-1]


def _in_lockstep(generators):
    values = [None] * len(generators)
    running = dict(enumerate(generators))
    while running:
        for i, gen in list(running.items()):
            try:
                next(gen)
            except StopIteration as done:
                values[i] = done.value
                del running[i]
    return values


def _hgrn_prompt_kernel(lbl_ref, q_ref, f_ref, i_ref, z_ref, gain_ref, o_ref, s_ref):
    r_len = HGRN_CHUNK * HGRN_CHUNKS_PER_STEP
    lb = _lower_bound(lbl_ref[...])
    gain = gain_ref[...]

    def step(ci, st):
        rs = pl.multiple_of(ci * r_len, r_len)
        rows = pl.ds(rs, r_len)
        (o, st), = _in_lockstep(
            [_hgrn_chunks(q_ref[0, rows, :], f_ref[0, rows, :], i_ref[0, rows, :], lb, st, HGRN_CHUNK)])
        o_ref[rows, :] = _head_norm_gate(o, gain, z_ref[0, rows, :]).astype(o_ref.dtype)
        return st

    st = lax.fori_loop(0, q_ref.shape[1] // r_len, step, jnp.zeros((HEAD_DIM, HEAD_DIM), F32))
    s_ref[0, 0] = st.T


def _hgrn_prompt(p3, lb_logits, gain_b, batch, seq):
    n_heads = p3.shape[2] // HEAD_DIM
    comp = lambda c: pl.BlockSpec((1, seq, HEAD_DIM), lambda b, h: (c, b, h))
    return pl.pallas_call(
        _hgrn_prompt_kernel,
        grid=(batch, n_heads),
        in_specs=[
            pl.BlockSpec((lb_logits.shape[0], HEAD_DIM), lambda b, h: (0, h)),
            comp(COMP_QB), comp(COMP_FB), comp(COMP_IB), comp(COMP_ZB),
            pl.BlockSpec((1, HEAD_DIM), lambda b, h: (0, h)),
        ],
        out_specs=[
            pl.BlockSpec((seq, HEAD_DIM), lambda b, h: (b, h)),
            pl.BlockSpec((1, 1, HEAD_DIM, HEAD_DIM), lambda b, h: (b, h, 0, 0)),
        ],
        out_shape=[
            jax.ShapeDtypeStruct((batch * seq, n_heads * HEAD_DIM), BF16),
            jax.ShapeDtypeStruct((batch, n_heads, HEAD_DIM, HEAD_DIM), F32),
        ],
        compiler_params=pltpu.CompilerParams(
            dimension_semantics=("arbitrary", "arbitrary"), vmem_limit_bytes=VMEM_LIMIT),
        name="hgrn_prompt",
    )(lb_logits, p3, p3, p3, p3, gain_b)


def _hgrn_decode_kernel(lbl_ref, q_ref, f_ref, i_ref, z_ref, gain_ref, s0_ref, o_ref, s_ref, *, n_valid):
    n_heads = s0_ref.shape[1]
    head = lambda h: slice(h * HEAD_DIM, (h + 1) * HEAD_DIM)
    lb = _lower_bound(lbl_ref[...])
    results = _in_lockstep([
        _hgrn_chunks(q_ref[0, :, head(h)], f_ref[0, :, head(h)], i_ref[0, :, head(h)], lb[:, head(h)],
                     s0_ref[0, h].T, SAMPLE_PAD, n_valid=n_valid)
        for h in range(n_heads)])
    for h, (o, st) in enumerate(results):
        o_ref[:, head(h)] = _head_norm_gate(o, gain_ref[:, head(h)], z_ref[0, :, head(h)]).astype(o_ref.dtype)
        s_ref[0, h] = st.T


def _hgrn_decode(p3s, lb_logits, gain_b, state, n_valid):
    dec_batch, n_heads = state.shape[:2]
    width = n_heads * HEAD_DIM
    comp = lambda c: pl.BlockSpec((1, SAMPLE_PAD, width), lambda b: (c, b, 0))
    state_spec = pl.BlockSpec((1, n_heads, HEAD_DIM, HEAD_DIM), lambda b: (b, 0, 0, 0))
    return pl.pallas_call(
        functools.partial(_hgrn_decode_kernel, n_valid=n_valid),
        grid=(dec_batch,),
        in_specs=[
            pl.BlockSpec(lb_logits.shape, lambda b: (0, 0)),
            comp(COMP_QB), comp(COMP_FB), comp(COMP_IB), comp(COMP_ZB),
            pl.BlockSpec((1, width), lambda b: (0, 0)),
            state_spec,
        ],
        out_specs=[pl.BlockSpec((SAMPLE_PAD, width), lambda b: (b, 0)), state_spec],
        out_shape=[
            jax.ShapeDtypeStruct((dec_batch * SAMPLE_PAD, width), BF16),
            jax.ShapeDtypeStruct(state.shape, F32),
        ],
        compiler_params=pltpu.CompilerParams(dimension_semantics=("arbitrary",)),
        name="hgrn_decode",
    )(lb_logits, p3s, p3s, p3s, p3s, gain_b, state)


def _sb_decode_kernel(pt_ref, qr_ref, bias_ref, kn_ref, vn_ref, *rest, n_valid, n_pg):
    ck_hbm, cv_hbm, mu_ref, z_ref, gain_ref, o_ref, acc_scr, carry_scr, kbuf, vbuf, sems = rest
    j = pl.program_id(1)
    n_steps = pl.num_programs(1)
    step = pl.program_id(0) * n_steps + j
    n_slots = kbuf.shape[0]
    n_heads, n_rows = acc_scr.shape[:2]
    n_q = n_rows // n_heads
    page = kbuf.shape[2] // n_heads

    def page_copies(t):
        seq, js, slot = t // n_steps, t % n_steps, t % n_slots
        copies = []
        for i in range(n_pg):
            src = pt_ref[seq, n_steps * n_pg - 1 - (js * n_pg + i)]
            copies.append(pltpu.make_async_copy(ck_hbm.at[src], kbuf.at[slot, i], sems.at[slot, 0, i]))
            copies.append(pltpu.make_async_copy(cv_hbm.at[src], vbuf.at[slot, i], sems.at[slot, 1, i]))
        return copies

    @pl.when(step == 0)
    def _():
        for t in range(n_slots - 1):
            for copy in page_copies(t):
                copy.start()

    @pl.when(step + n_slots - 1 < pl.num_programs(0) * n_steps)
    def _():
        for copy in page_copies(step + n_slots - 1):
            copy.start()

    zscale = HEAD_DIM ** -0.5 * LOG2E
    bias2 = bias_ref[...] * LOG2E
    head = lambda h: slice(h * HEAD_DIM, (h + 1) * HEAD_DIM)

    def scores(k_heads):
        z = sum(_dot_nt(qr_ref[0, h], k_heads[h].astype(BF16)) for h in range(n_heads))
        return z * zscale + bias2[:, :z.shape[1]]

    @pl.when(j == 0)
    def _():
        r = kn_ref.shape[1]
        z = scores([kn_ref[0, :, head(h)] for h in range(n_heads)])
        t_row = lax.broadcasted_iota(jnp.int32, (n_rows, r), 0) % n_q
        s_col = lax.broadcasted_iota(jnp.int32, (n_rows, r), 1)
        valid = (s_col < t_row) & (s_col < n_valid)
        sp = jnp.where(valid, _softplus2(z), 0.0)
        later_mat = jnp.where(lax.broadcasted_iota(jnp.int32, (r, r), 0)
                              > lax.broadcasted_iota(jnp.int32, (r, r), 1), 1.0, 0.0)
        later = _dot(sp, later_mat, precision=lax.Precision.HIGHEST)
        wb = jnp.where(valid, jnp.exp2(z - sp - later), 0.0).astype(BF16)
        carry_scr[...] = jnp.broadcast_to(jnp.sum(sp, axis=1, keepdims=True), carry_scr.shape)
        for h in range(n_heads):
            acc_scr[h] = _dot(wb, vn_ref[0, :, head(h)].astype(BF16))

    for copy in page_copies(step):
        copy.wait()
    slot = step % n_slots
    kp_refs = [kbuf.at[slot, i] for i in range(n_pg)]
    vp_refs = [vbuf.at[slot, i] for i in range(n_pg)]
    zs = [scores([kp_ref[pl.ds(h, page, stride=n_heads), :] for h in range(n_heads)]) for kp_ref in kp_refs]
    sps = [_softplus2(z) for z in zs]
    rs = [_dot(jnp.concatenate(_split_bf16(sp), axis=1), mu_ref[...]) for sp in sps]
    carry = carry_scr[...]
    wbs = []
    for z, sp, r in zip(zs, sps, rs):
        wbs.append(jnp.exp2(z - sp - r[:, :page] - carry).astype(BF16))
        carry = carry + r[:, page:]
    carry_scr[...] = carry
    for h in range(n_heads):
        acc_scr[h] += sum(_dot(wb, vp_ref[pl.ds(h, page, stride=n_heads), :].astype(BF16))
                          for wb, vp_ref in zip(wbs, vp_refs))

    @pl.when(j == pl.num_programs(1) - 1)
    def _():
        rh = lax.broadcasted_iota(jnp.int32, (n_rows, HEAD_DIM), 0) // n_q
        o = jnp.zeros((n_rows, HEAD_DIM), F32)
        for h in range(n_heads):
            o = o + jnp.where(rh == h, acc_scr[h], 0.0)
        o_ref[0] = _head_norm_gate(o, gain_ref[...], z_ref[0]).astype(o_ref.dtype)


def _sb_decode(page_table, q_rows, bias_rows, p3s, cache_k, cache_v, z_r, gain_r, n_valid):
    dec_batch, n_pages = page_table.shape
    page_rows = cache_k.shape[1]
    n_heads, n_rows = q_rows.shape[1:3]
    width = n_heads * HEAD_DIM
    page = page_rows // n_heads
    assert page == HEAD_DIM
    mu = _sb_matrix(page, 2)
    new = lambda c: pl.BlockSpec((1, SAMPLE_PAD, width), lambda b, j, pt: (c, b, 0))
    n_pg = DECODE_PAGES_PER_STEP
    assert n_pages % n_pg == 0 and dec_batch * (n_pages // n_pg) >= DECODE_SLOTS - 1
    page_buf = pltpu.VMEM((DECODE_SLOTS, n_pg, page_rows, HEAD_DIM), cache_k.dtype)
    grid_spec = pltpu.PrefetchScalarGridSpec(
        num_scalar_prefetch=1,
        grid=(dec_batch, n_pages // n_pg),
        in_specs=[
            pl.BlockSpec((1, n_heads, n_rows, HEAD_DIM), lambda b, j, pt: (b, 0, 0, 0)),
            pl.BlockSpec((n_rows, HEAD_DIM), lambda b, j, pt: (0, 0)),
            new(COMP_KA), new(COMP_VA),
            pl.BlockSpec(memory_space=pl.ANY), pl.BlockSpec(memory_space=pl.ANY),
            pl.BlockSpec(mu.shape, lambda b, j, pt: (0, 0)),
            pl.BlockSpec((1, n_rows, HEAD_DIM), lambda b, j, pt: (b, 0, 0)),
            pl.BlockSpec((n_rows, HEAD_DIM), lambda b, j, pt: (0, 0)),
        ],
        out_specs=pl.BlockSpec((1, n_rows, HEAD_DIM), lambda b, j, pt: (b, 0, 0)),
        scratch_shapes=[pltpu.VMEM((n_heads, n_rows, HEAD_DIM), F32), pltpu.VMEM((n_rows, HEAD_DIM), F32),
                        page_buf, page_buf, pltpu.SemaphoreType.DMA((DECODE_SLOTS, 2, n_pg))],
    )
    return pl.pallas_call(
        functools.partial(_sb_decode_kernel, n_valid=n_valid, n_pg=n_pg),
        grid_spec=grid_spec,
        out_shape=jax.ShapeDtypeStruct((dec_batch, n_rows, HEAD_DIM), BF16),
        compiler_params=pltpu.CompilerParams(
            dimension_semantics=("arbitrary", "arbitrary"), vmem_limit_bytes=VMEM_LIMIT),
        name="sb_decode",
    )(page_table, q_rows, bias_rows, p3s, p3s, cache_k, cache_v, mu, z_r, gain_r)


def _out_proj_kernel(x_ref, ma_ref, mb_ref, wa_ref, wb_ref, fw_ref, o_ref):
    y = x_ref[...] + _dot(ma_ref[...], wa_ref[...].astype(BF16)) + _dot(mb_ref[...], wb_ref[...].astype(BF16))
    ms = jnp.mean(y * y, axis=-1, keepdims=True)
    o_ref[...] = y * lax.rsqrt(ms + EPS) * fw_ref[...]


def _out_proj(x, m_a, m_b, w, final_w, bm):
    t, d = x.shape
    wa = m_a.shape[1]
    wb = m_b.shape[1]
    assert wa == wb
    return pl.pallas_call(
        _out_proj_kernel,
        grid=(t // bm,),
        in_specs=[
            pl.BlockSpec((bm, d), lambda m: (m, 0)),
            pl.BlockSpec((bm, wa), lambda m: (m, 0)),
            pl.BlockSpec((bm, wb), lambda m: (m, 0)),
            pl.BlockSpec((wa, d), lambda m: (0, 0)),
            pl.BlockSpec((wb, d), lambda m: (1, 0)),
            pl.BlockSpec((1, d), lambda m: (0, 0)),
        ],
        out_specs=pl.BlockSpec((bm, d), lambda m: (m, 0)),
        out_shape=jax.ShapeDtypeStruct((t, d), F32),
        compiler_params=pltpu.CompilerParams(
            dimension_semantics=("arbitrary",), vmem_limit_bytes=VMEM_LIMIT),
        name="out_proj",
    )(x, m_a, m_b, w, w, final_w)


def kernel(x_prompt, x_sample, cache_k, cache_v, state_s, page_table, norm_w, w_in, gain_a, gain_b,
           sb_bias, lb_logits, w_out, final_norm_w):
    depth = norm_w.shape[0]
    assert depth == 1
    batch, seq, d_model = x_prompt.shape
    dec_batch, dec_seq, _ = x_sample.shape
    n_heads_a, head_dim = cache_k.shape[3:]
    w_a = n_heads_a * head_dim
    assert head_dim == HEAD_DIM and state_s.shape[3:] == (HEAD_DIM, HEAD_DIM)
    assert dec_seq <= SAMPLE_PAD and seq % SB_TILE == 0 and seq % (HGRN_CHUNK * HGRN_CHUNKS_PER_STEP) == 0
    assert (batch * seq) % IN_PROJ_ROWS == 0 and (batch * seq) % OUT_PROJ_ROWS == 0

    nw = norm_w[0][None, :]
    fw = final_norm_w[None, :]
    ga = gain_a[0][None, :]
    gb = gain_b[0][None, :]

    xp = x_prompt.reshape(batch * seq, d_model)
    xs = jnp.pad(x_sample, ((0, 0), (0, SAMPLE_PAD - dec_seq), (0, 0))).reshape(dec_batch * SAMPLE_PAD, d_model)
    p3, p3s = _in_proj(xp, xs, nw, w_in[0], bm=IN_PROJ_ROWS)
    m_a, k_p, v_p = _sb_prompt(p3, sb_bias, ga, batch, seq)
    m_b, s_p = _hgrn_prompt(p3, lb_logits, gb, batch, seq)
    y_prompt = _out_proj(xp, m_a, m_b, w_out[0], fw, bm=OUT_PROJ_ROWS).reshape(batch, seq, d_model)
    k_p = k_p.reshape(1, batch, seq, n_heads_a, head_dim)
    v_p = v_p.reshape(1, batch, seq, n_heads_a, head_dim)

    rows = lambda c: p3s[c].reshape(dec_batch, SAMPLE_PAD, -1)[:, :dec_seq]
    q_t = rows(COMP_QA).reshape(dec_batch, dec_seq, n_heads_a, head_dim).transpose(0, 2, 1, 3)
    q_rows = (q_t[:, :, None, :, :] * jnp.eye(n_heads_a, dtype=F32)[None, :, :, None, None]).reshape(
        dec_batch, n_heads_a, n_heads_a * dec_seq, head_dim).astype(BF16)
    bias_rows = jnp.broadcast_to(jnp.repeat(sb_bias[0], dec_seq)[:, None], (n_heads_a * dec_seq, head_dim))
    to_rows = lambda a: a.reshape(dec_batch, dec_seq, n_heads_a, head_dim).transpose(0, 2, 1, 3).reshape(
        dec_batch, n_heads_a * dec_seq, head_dim)
    z_r = to_rows(rows(COMP_ZA))
    gain_r = jnp.repeat(gain_a[0].reshape(n_heads_a, head_dim), dec_seq, axis=0)
    ck = cache_k[0].reshape(cache_k.shape[1], cache_k.shape[2] * n_heads_a, head_dim)
    cv = cache_v[0].reshape(cache_v.shape[1], cache_v.shape[2] * n_heads_a, head_dim)
    o_r = _sb_decode(page_table, q_rows, bias_rows, p3s, ck, cv, z_r, gain_r, dec_seq)
    m_a_s = o_r.reshape(dec_batch, n_heads_a, dec_seq, head_dim).transpose(0, 2, 1, 3).reshape(
        dec_batch, dec_seq, w_a)
    m_a_s = jnp.pad(m_a_s, ((0, 0), (0, SAMPLE_PAD - dec_seq), (0, 0))).reshape(dec_batch * SAMPLE_PAD, w_a)
    m_b_s, s_s = _hgrn_decode(p3s, lb_logits, gb, state_s[0], dec_seq)
    y_s = _out_proj(xs, m_a_s, m_b_s, w_out[0], fw, bm=dec_batch * SAMPLE_PAD)
    y_sample = y_s.reshape(dec_batch, SAMPLE_PAD, d_model)[:, :dec_seq]
    k_s = rows(COMP_KA).reshape(1, dec_batch, dec_seq, n_heads_a, head_dim)
    v_s = rows(COMP_VA).reshape(1, dec_batch, dec_seq, n_heads_a, head_dim)

    return (y_prompt, y_sample, k_p, v_p, s_p[None], k_s, v_s, s_s[None])
```

```python
import functools

import jax
import jax.numpy as jnp
from jax import lax
from jax.experimental import pallas as pl
from jax.experimental.pallas import tpu as pltpu

EPS = 1e-6
HEAD_DIM = 128
N_COMP = 8
COMP_QA, COMP_KA, COMP_VA, COMP_ZA, COMP_QB, COMP_FB, COMP_IB, COMP_ZB = range(N_COMP)
IN_PROJ_ROWS = 1024
OUT_PROJ_ROWS = 512
SB_BLOCK = 128
SB_TILE = 512
LOG2E = 1.4426950408889634
HGRN_CHUNK = 64
HGRN_CHUNKS_PER_STEP = 32
CUMSUM_GROUP = 256
HGRN_SUB = 16
SAMPLE_PAD = 8
DECODE_PAGES_PER_STEP = 8
DECODE_SLOTS = 4
VMEM_LIMIT = 60 * 1024 * 1024

F32 = jnp.float32
BF16 = jnp.bfloat16


def _dot(a, b, **kw):
    return jnp.dot(a, b, preferred_element_type=F32, **kw)


def _dot_nt(a, b):
    return lax.dot_general(a, b, (((1,), (1,)), ((), ())), preferred_element_type=F32)


def _dot_tn(a, b):
    return lax.dot_general(a, b, (((0,), (0,)), ((), ())), preferred_element_type=F32)


def _sigmoid(x):
    return 1.0 / (1.0 + jnp.exp(-x))


def _softplus2(z2):
    return jnp.maximum(z2, 0.0) + jnp.log2(1.0 + jnp.exp2(-jnp.abs(z2)))


def _split_bf16(x):
    hi = x.astype(BF16)
    lo = (x - hi.astype(F32)).astype(BF16)
    return hi, lo


def _in_proj_kernel(x_ref, xs_ref, nw_ref, w_ref, o_ref, os_ref, h_scr):
    m, n = pl.program_id(0), pl.program_id(1)

    def normed(x):
        ms = jnp.mean(x * x, axis=-1, keepdims=True)
        return (x * lax.rsqrt(ms + EPS) * nw_ref[...]).astype(BF16)

    @pl.when(n == 0)
    def _():
        h_scr[...] = normed(x_ref[...])

    @pl.when(m == 0)
    def _():
        os_ref[0] = _dot(normed(xs_ref[...]), w_ref[...].astype(BF16))

    o_ref[0] = _dot(h_scr[...], w_ref[...].astype(BF16))


def _in_proj(x, xs, norm_w, w, bm):
    t, d = x.shape
    ts = xs.shape[0]
    d_in = w.shape[1]
    bn = d_in // N_COMP
    return pl.pallas_call(
        _in_proj_kernel,
        grid=(t // bm, N_COMP),
        in_specs=[
            pl.BlockSpec((bm, d), lambda m, n: (m, 0)),
            pl.BlockSpec((ts, d), lambda m, n: (0, 0)),
            pl.BlockSpec((1, d), lambda m, n: (0, 0)),
            pl.BlockSpec((d, bn), lambda m, n: (0, n)),
        ],
        out_specs=[
            pl.BlockSpec((1, bm, bn), lambda m, n: (n, m, 0)),
            pl.BlockSpec((1, ts, bn), lambda m, n: (jnp.where(m == 0, n, N_COMP - 1), 0, 0)),
        ],
        out_shape=[
            jax.ShapeDtypeStruct((N_COMP, t, bn), F32),
            jax.ShapeDtypeStruct((N_COMP, ts, bn), F32),
        ],
        scratch_shapes=[pltpu.VMEM((bm, d), BF16)],
        compiler_params=pltpu.CompilerParams(
            dimension_semantics=("arbitrary", "arbitrary"), vmem_limit_bytes=VMEM_LIMIT),
        name="in_proj",
    )(x, xs, norm_w, w)


def _head_norm_gate(o, gain, z):
    ms = jnp.mean(o * o, axis=-1, keepdims=True)
    return o * lax.rsqrt(ms + EPS) * gain * (z * _sigmoid(z))


def _sb_prompt_kernel(bias_ref, q_ref, k_ref, v_ref, z_ref, gain_ref, mu_ref, o_ref, ko_ref, vo_ref,
                      kb_scr, vb_scr, acc_scr, carry_scr):
    tq = SB_TILE
    bk = SB_BLOCK
    t_len = q_ref.shape[1]
    bias2 = bias_ref[0, pl.program_id(1)] * LOG2E
    qscale = HEAD_DIM ** -0.5 * LOG2E
    kb_scr[...] = k_ref[0].astype(BF16)
    vb_scr[...] = v_ref[0].astype(BF16)
    ko_ref[...] = k_ref[0]
    vo_ref[...] = v_ref[0]
    gain = gain_ref[...]

    def tile(q, k0, tk, masked):
        keys = pl.ds(pl.multiple_of(k0, tq), tk)
        z = _dot_nt(q, kb_scr[keys, :]) + bias2
        sp = _softplus2(z)
        if masked:
            causal = (lax.broadcasted_iota(jnp.int32, (tq, tk), 1) - (tk - tq)
                      < lax.broadcasted_iota(jnp.int32, (tq, tk), 0))
            sp = jnp.where(causal, sp, 0.0)
        zs = z - sp
        hi, lo = _split_bf16(sp)
        carry = carry_scr[...]
        ws = [None] * (tk // bk)
        for c in reversed(range(tk // bk)):
            cols = slice(c * bk, (c + 1) * bk)
            r = _dot(jnp.concatenate([hi[:, cols], lo[:, cols]], axis=1), mu_ref[...])
            ws[c] = jnp.exp2(zs[:, cols] - r[:, :bk] - carry)
            carry = carry + r[:, bk:]
        w = jnp.concatenate(ws, axis=1)
        if masked:
            w = jnp.where(causal, w, 0.0)
        carry_scr[...] = carry
        acc_scr[...] += _dot(w.astype(BF16), vb_scr[keys, :])

    def q_tile(qt, _):
        qs = pl.multiple_of(qt * tq, tq)
        q = (q_ref[0, pl.ds(qs, tq), :] * qscale).astype(BF16)
        acc_scr[...] = jnp.zeros_like(acc_scr)
        carry_scr[...] = jnp.zeros_like(carry_scr)
        @pl.when(qt == 0)
        def _():
            tile(q, 0, tq, True)

        @pl.when(qt > 0)
        def _():
            tile(q, qs - tq, 2 * tq, True)

        def k_pair(jj, _):
            tile(q, qs - tq - (jj + 1) * 2 * tq, 2 * tq, False)
            return 0

        lax.fori_loop(0, jnp.maximum(qt - 1, 0) // 2, k_pair, 0)

        @pl.when((qt > 0) & (qt % 2 == 0))
        def _():
            tile(q, 0, tq, False)

        zg = z_ref[0, pl.ds(qs, tq), :]
        o_ref[pl.ds(qs, tq), :] = _head_norm_gate(acc_scr[...], gain, zg).astype(o_ref.dtype)
        return 0

    lax.fori_loop(0, t_len // tq, q_tile, 0)


def _sb_matrix(n, pieces):
    j = jnp.arange(pieces * n)[:, None] % n
    s = jnp.arange(2 * n)[None, :]
    return jnp.where(s < n, (j > s), True).astype(BF16)


def _sb_prompt(p3, sb_bias, gain_a, batch, seq):
    n_heads = p3.shape[2] // HEAD_DIM
    mu = _sb_matrix(SB_BLOCK, 2)
    comp = lambda c: pl.BlockSpec((1, seq, HEAD_DIM), lambda b, h: (c, b, h))
    return pl.pallas_call(
        _sb_prompt_kernel,
        grid=(batch, n_heads),
        in_specs=[
            pl.BlockSpec(memory_space=pltpu.SMEM),
            comp(COMP_QA), comp(COMP_KA), comp(COMP_VA), comp(COMP_ZA),
            pl.BlockSpec((1, HEAD_DIM), lambda b, h: (0, h)),
            pl.BlockSpec(mu.shape, lambda b, h: (0, 0)),
        ],
        out_specs=[pl.BlockSpec((seq, HEAD_DIM), lambda b, h: (b, h))] * 3,
        out_shape=[
            jax.ShapeDtypeStruct((batch * seq, n_heads * HEAD_DIM), BF16),
            jax.ShapeDtypeStruct((batch * seq, n_heads * HEAD_DIM), p3.dtype),
            jax.ShapeDtypeStruct((batch * seq, n_heads * HEAD_DIM), p3.dtype),
        ],
        scratch_shapes=[pltpu.VMEM((seq, HEAD_DIM), BF16), pltpu.VMEM((seq, HEAD_DIM), BF16),
                        pltpu.VMEM((SB_TILE, HEAD_DIM), F32), pltpu.VMEM((SB_TILE, SB_BLOCK), F32)],
        compiler_params=pltpu.CompilerParams(
            dimension_semantics=("arbitrary", "arbitrary"), vmem_limit_bytes=VMEM_LIMIT),
        name="sb_prompt",
    )(sb_bias, p3, p3, p3, p3, gain_a, mu)


def _lower_bound(l):
    e = jnp.exp(l - jnp.max(l, axis=0, keepdims=True))
    return e[0:1, :] / jnp.sum(e, axis=0, keepdims=True)


def _chunk_cumsum(x, c_len):
    rows = x.shape[0]
    group = min(rows, CUMSUM_GROUP)
    ti = lax.broadcasted_iota(jnp.int32, (group, group), 0)
    si = lax.broadcasted_iota(jnp.int32, (group, group), 1)
    tri = jnp.where((si <= ti) & (si // c_len == ti // c_len), 1.0, 0.0).astype(BF16)
    hi = x.astype(BF16)
    rest = x - hi.astype(F32)
    mid = rest.astype(BF16)
    lo = (rest - mid.astype(F32)).astype(BF16)
    pieces = jnp.concatenate([hi, mid, lo], axis=1)
    outs = []
    for g0 in range(0, rows, group):
        r = _dot(tri, pieces[g0:g0 + group])
        outs.append((r[:, :HEAD_DIM] + r[:, HEAD_DIM:2 * HEAD_DIM]) + r[:, 2 * HEAD_DIM:])
    return jnp.concatenate(outs, axis=0)


def _hgrn_chunks(qb, fb, ib, lb, st, c_len, n_valid=None):
    rows = qb.shape[0]
    sub = min(HGRN_SUB, c_len)
    q = qb * _sigmoid(qb)
    g = lb + (1.0 - lb) * _sigmoid(fb)
    kk = 1.0 - g
    lg = jnp.log(g)
    if n_valid is not None:
        live = lax.broadcasted_iota(jnp.int32, (rows, HEAD_DIM), 0) < n_valid
        kk = jnp.where(live, kk, 0.0)
        lg = jnp.where(live, lg, 0.0)
    yield
    b = _chunk_cumsum(lg, c_len)
    vb = ib.astype(BF16)
    chunks = [slice(c * c_len, (c + 1) * c_len) for c in range(rows // c_len)]
    spans, operands = [], []
    for ch in chunks:
        for lo_r in range(ch.start, ch.stop, sub):
            hi_r = lo_r + sub
            m = b[lo_r + sub // 2:lo_r + sub // 2 + 1, :]
            qi = (q[lo_r:hi_r] * jnp.exp(b[lo_r:hi_r] - m)).astype(BF16)
            ki = (kk[ch.start:hi_r] * jnp.exp(m - b[ch.start:hi_r])).astype(BF16)
            spans.append((ch.start, lo_r, hi_r))
            operands.append((qi, ki))
    yield
    atts = [_dot_nt(qi, ki) for qi, ki in operands]
    masked = []
    for att, (c0, lo_r, hi_r) in zip(atts, spans):
        t_pos = lax.broadcasted_iota(jnp.int32, att.shape, 0) + (lo_r - c0)
        s_pos = lax.broadcasted_iota(jnp.int32, att.shape, 1)
        masked.append(jnp.where(s_pos <= t_pos, att, 0.0).astype(BF16))
    yield
    o_intra = jnp.concatenate([_dot(att, vb[c0:hi_r]) for att, (c0, _, hi_r) in zip(masked, spans)], axis=0)
    lasts = [b[ch.stop - 1:ch.stop, :] for ch in chunks]
    decayed = [(kk[ch] * jnp.exp(bl - b[ch])).astype(BF16) for ch, bl in zip(chunks, lasts)]
    yield
    incs = [_dot_tn(vb[ch], kd) for ch, kd in zip(chunks, decayed)]
    states = [st]
    for inc, bl in zip(incs, lasts):
        states.append(states[-1] * jnp.exp(bl) + inc)
    starts = [((q[ch] * jnp.exp(b[ch])).astype(BF16), s.astype(BF16)) for ch, s in zip(chunks, states)]
    yield
    o_inter = [_dot_nt(qe, s) for qe, s in starts]
    return o_intra + jnp.concatenate(o_inter, axis=0), states[-1]


def _in_lockstep(generators):
    values = [None] * len(generators)
    running = dict(enumerate(generators))
    while running:
        for i, gen in list(running.items()):
            try:
                next(gen)
            except StopIteration as done:
                values[i] = done.value
                del running[i]
    return values


def _hgrn_prompt_kernel(lbl_ref, q_ref, f_ref, i_ref, z_ref, gain_ref, o_ref, s_ref):
    r_len = HGRN_CHUNK * HGRN_CHUNKS_PER_STEP
    lb = _lower_bound(lbl_ref[...])
    gain = gain_ref[...]

    def step(ci, st):
        rs = pl.multiple_of(ci * r_len, r_len)
        rows = pl.ds(rs, r_len)
        (o, st), = _in_lockstep(
            [_hgrn_chunks(q_ref[0, rows, :], f_ref[0, rows, :], i_ref[0, rows, :], lb, st, HGRN_CHUNK)])
        o_ref[rows, :] = _head_norm_gate(o, gain, z_ref[0, rows, :]).astype(o_ref.dtype)
        return st

    st = lax.fori_loop(0, q_ref.shape[1] // r_len, step, jnp.zeros((HEAD_DIM, HEAD_DIM), F32))
    s_ref[0, 0] = st.T


def _hgrn_prompt(p3, lb_logits, gain_b, batch, seq):
    n_heads = p3.shape[2] // HEAD_DIM
    comp = lambda c: pl.BlockSpec((1, seq, HEAD_DIM), lambda b, h: (c, b, h))
    return pl.pallas_call(
        _hgrn_prompt_kernel,
        grid=(batch, n_heads),
        in_specs=[
            pl.BlockSpec((lb_logits.shape[0], HEAD_DIM), lambda b, h: (0, h)),
            comp(COMP_QB), comp(COMP_FB), comp(COMP_IB), comp(COMP_ZB),
            pl.BlockSpec((1, HEAD_DIM), lambda b, h: (0, h)),
        ],
        out_specs=[
            pl.BlockSpec((seq, HEAD_DIM), lambda b, h: (b, h)),
            pl.BlockSpec((1, 1, HEAD_DIM, HEAD_DIM), lambda b, h: (b, h, 0, 0)),
        ],
        out_shape=[
            jax.ShapeDtypeStruct((batch * seq, n_heads * HEAD_DIM), BF16),
            jax.ShapeDtypeStruct((batch, n_heads, HEAD_DIM, HEAD_DIM), F32),
        ],
        compiler_params=pltpu.CompilerParams(
            dimension_semantics=("arbitrary", "arbitrary"), vmem_limit_bytes=VMEM_LIMIT),
        name="hgrn_prompt",
    )(lb_logits, p3, p3, p3, p3, gain_b)


def _hgrn_decode_kernel(lbl_ref, q_ref, f_ref, i_ref, z_ref, gain_ref, s0_ref, o_ref, s_ref, *, n_valid):
    n_heads = s0_ref.shape[1]
    head = lambda h: slice(h * HEAD_DIM, (h + 1) * HEAD_DIM)
    lb = _lower_bound(lbl_ref[...])
    results = _in_lockstep([
        _hgrn_chunks(q_ref[0, :, head(h)], f_ref[0, :, head(h)], i_ref[0, :, head(h)], lb[:, head(h)],
                     s0_ref[0, h].T, SAMPLE_PAD, n_valid=n_valid)
        for h in range(n_heads)])
    for h, (o, st) in enumerate(results):
        o_ref[:, head(h)] = _head_norm_gate(o, gain_ref[:, head(h)], z_ref[0, :, head(h)]).astype(o_ref.dtype)
        s_ref[0, h] = st.T


def _hgrn_decode(p3s, lb_logits, gain_b, state, n_valid):
    dec_batch, n_heads = state.shape[:2]
    width = n_heads * HEAD_DIM
    comp = lambda c: pl.BlockSpec((1, SAMPLE_PAD, width), lambda b: (c, b, 0))
    state_spec = pl.BlockSpec((1, n_heads, HEAD_DIM, HEAD_DIM), lambda b: (b, 0, 0, 0))
    return pl.pallas_call(
        functools.partial(_hgrn_decode_kernel, n_valid=n_valid),
        grid=(dec_batch,),
        in_specs=[
            pl.BlockSpec(lb_logits.shape, lambda b: (0, 0)),
            comp(COMP_QB), comp(COMP_FB), comp(COMP_IB), comp(COMP_ZB),
            pl.BlockSpec((1, width), lambda b: (0, 0)),
            state_spec,
        ],
        out_specs=[pl.BlockSpec((SAMPLE_PAD, width), lambda b: (b, 0)), state_spec],
        out_shape=[
            jax.ShapeDtypeStruct((dec_batch * SAMPLE_PAD, width), BF16),
            jax.ShapeDtypeStruct(state.shape, F32),
        ],
        compiler_params=pltpu.CompilerParams(dimension_semantics=("arbitrary",)),
        name="hgrn_decode",
    )(lb_logits, p3s, p3s, p3s, p3s, gain_b, state)


def _sb_decode_kernel(pt_ref, qr_ref, bias_ref, kn_ref, vn_ref, *rest, n_valid, n_pg):
    ck_hbm, cv_hbm, mu_ref, z_ref, gain_ref, o_ref, acc_scr, carry_scr, kbuf, vbuf, sems = rest
    j = pl.program_id(1)
    n_steps = pl.num_programs(1)
    step = pl.program_id(0) * n_steps + j
    n_slots = kbuf.shape[0]
    n_heads, n_rows = acc_scr.shape[:2]
    n_q = n_rows // n_heads
    page = kbuf.shape[2] // n_heads

    def page_copies(t):
        seq, js, slot = t // n_steps, t % n_steps, t % n_slots
        copies = []
        for i in range(n_pg):
            src = pt_ref[seq, n_steps * n_pg - 1 - (js * n_pg + i)]
            copies.append(pltpu.make_async_copy(ck_hbm.at[src], kbuf.at[slot, i], sems.at[slot, 0, i]))
            copies.append(pltpu.make_async_copy(cv_hbm.at[src], vbuf.at[slot, i], sems.at[slot, 1, i]))
        return copies

    @pl.when(step == 0)
    def _():
        for t in range(n_slots - 1):
            for copy in page_copies(t):
                copy.start()

    @pl.when(step + n_slots - 1 < pl.num_programs(0) * n_steps)
    def _():
        for copy in page_copies(step + n_slots - 1):
            copy.start()

    zscale = HEAD_DIM ** -0.5 * LOG2E
    bias2 = bias_ref[...] * LOG2E
    head = lambda h: slice(h * HEAD_DIM, (h + 1) * HEAD_DIM)

    def scores(k_heads):
        z = sum(_dot_nt(qr_ref[0, h], k_heads[h].astype(BF16)) for h in range(n_heads))
        return z * zscale + bias2[:, :z.shape[1]]

    @pl.when(j == 0)
    def _():
        r = kn_ref.shape[1]
        z = scores([kn_ref[0, :, head(h)] for h in range(n_heads)])
        t_row = lax.broadcasted_iota(jnp.int32, (n_rows, r), 0) % n_q
        s_col = lax.broadcasted_iota(jnp.int32, (n_rows, r), 1)
        valid = (s_col < t_row) & (s_col < n_valid)
        sp = jnp.where(valid, _softplus2(z), 0.0)
        later_mat = jnp.where(lax.broadcasted_iota(jnp.int32, (r, r), 0)
                              > lax.broadcasted_iota(jnp.int32, (r, r), 1), 1.0, 0.0)
        later = _dot(sp, later_mat, precision=lax.Precision.HIGHEST)
        wb = jnp.where(valid, jnp.exp2(z - sp - later), 0.0).astype(BF16)
        carry_scr[...] = jnp.broadcast_to(jnp.sum(sp, axis=1, keepdims=True), carry_scr.shape)
        for h in range(n_heads):
            acc_scr[h] = _dot(wb, vn_ref[0, :, head(h)].astype(BF16))

    for copy in page_copies(step):
        copy.wait()
    slot = step % n_slots
    kp_refs = [kbuf.at[slot, i] for i in range(n_pg)]
    vp_refs = [vbuf.at[slot, i] for i in range(n_pg)]
    zs = [scores([kp_ref[pl.ds(h, page, stride=n_heads), :] for h in range(n_heads)]) for kp_ref in kp_refs]
    sps = [_softplus2(z) for z in zs]
    rs = [_dot(jnp.concatenate(_split_bf16(sp), axis=1), mu_ref[...]) for sp in sps]
    carry = carry_scr[...]
    wbs = []
    for z, sp, r in zip(zs, sps, rs):
        wbs.append(jnp.exp2(z - sp - r[:, :page] - carry).astype(BF16))
        carry = carry + r[:, page:]
    carry_scr[...] = carry
    for h in range(n_heads):
        acc_scr[h] += sum(_dot(wb, vp_ref[pl.ds(h, page, stride=n_heads), :].astype(BF16))
                          for wb, vp_ref in zip(wbs, vp_refs))

    @pl.when(j == pl.num_programs(1) - 1)
    def _():
        rh = lax.broadcasted_iota(jnp.int32, (n_rows, HEAD_DIM), 0) // n_q
        o = jnp.zeros((n_rows, HEAD_DIM), F32)
        for h in range(n_heads):
            o = o + jnp.where(rh == h, acc_scr[h], 0.0)
        o_ref[0] = _head_norm_gate(o, gain_ref[...], z_ref[0]).astype(o_ref.dtype)


def _sb_decode(page_table, q_rows, bias_rows, p3s, cache_k, cache_v, z_r, gain_r, n_valid):
    dec_batch, n_pages = page_table.shape
    page_rows = cache_k.shape[1]
    n_heads, n_rows = q_rows.shape[1:3]
    width = n_heads * HEAD_DIM
    page = page_rows // n_heads
    assert page == HEAD_DIM
    mu = _sb_matrix(page, 2)
    new = lambda c: pl.BlockSpec((1, SAMPLE_PAD, width), lambda b, j, pt: (c, b, 0))
    n_pg = DECODE_PAGES_PER_STEP
    assert n_pages % n_pg == 0 and dec_batch * (n_pages // n_pg) >= DECODE_SLOTS - 1
    page_buf = pltpu.VMEM((DECODE_SLOTS, n_pg, page_rows, HEAD_DIM), cache_k.dtype)
    grid_spec = pltpu.PrefetchScalarGridSpec(
        num_scalar_prefetch=1,
        grid=(dec_batch, n_pages // n_pg),
        in_specs=[
            pl.BlockSpec((1, n_heads, n_rows, HEAD_DIM), lambda b, j, pt: (b, 0, 0, 0)),
            pl.BlockSpec((n_rows, HEAD_DIM), lambda b, j, pt: (0, 0)),
            new(COMP_KA), new(COMP_VA),
            pl.BlockSpec(memory_space=pl.ANY), pl.BlockSpec(memory_space=pl.ANY),
            pl.BlockSpec(mu.shape, lambda b, j, pt: (0, 0)),
            pl.BlockSpec((1, n_rows, HEAD_DIM), lambda b, j, pt: (b, 0, 0)),
            pl.BlockSpec((n_rows, HEAD_DIM), lambda b, j, pt: (0, 0)),
        ],
        out_specs=pl.BlockSpec((1, n_rows, HEAD_DIM), lambda b, j, pt: (b, 0, 0)),
        scratch_shapes=[pltpu.VMEM((n_heads, n_rows, HEAD_DIM), F32), pltpu.VMEM((n_rows, HEAD_DIM), F32),
                        page_buf, page_buf, pltpu.SemaphoreType.DMA((DECODE_SLOTS, 2, n_pg))],
    )
    return pl.pallas_call(
        functools.partial(_sb_decode_kernel, n_valid=n_valid, n_pg=n_pg),
        grid_spec=grid_spec,
        out_shape=jax.ShapeDtypeStruct((dec_batch, n_rows, HEAD_DIM), BF16),
        compiler_params=pltpu.CompilerParams(
            dimension_semantics=("arbitrary", "arbitrary"), vmem_limit_bytes=VMEM_LIMIT),
        name="sb_decode",
    )(page_table, q_rows, bias_rows, p3s, p3s, cache_k, cache_v, mu, z_r, gain_r)


def _out_proj_kernel(x_ref, ma_ref, mb_ref, wa_ref, wb_ref, fw_ref, o_ref):
    y = x_ref[...] + _dot(ma_ref[...], wa_ref[...].astype(BF16)) + _dot(mb_ref[...], wb_ref[...].astype(BF16))
    ms = jnp.mean(y * y, axis=-1, keepdims=True)
    o_ref[...] = y * lax.rsqrt(ms + EPS) * fw_ref[...]


def _out_proj(x, m_a, m_b, w, final_w, bm):
    t, d = x.shape
    wa = m_a.shape[1]
    wb = m_b.shape[1]
    assert wa == wb
    return pl.pallas_call(
        _out_proj_kernel,
        grid=(t // bm,),
        in_specs=[
            pl.BlockSpec((bm, d), lambda m: (m, 0)),
            pl.BlockSpec((bm, wa), lambda m: (m, 0)),
            pl.BlockSpec((bm, wb), lambda m: (m, 0)),
            pl.BlockSpec((wa, d), lambda m: (0, 0)),
            pl.BlockSpec((wb, d), lambda m: (1, 0)),
            pl.BlockSpec((1, d), lambda m: (0, 0)),
        ],
        out_specs=pl.BlockSpec((bm, d), lambda m: (m, 0)),
        out_shape=jax.ShapeDtypeStruct((t, d), F32),
        compiler_params=pltpu.CompilerParams(
            dimension_semantics=("arbitrary",), vmem_limit_bytes=VMEM_LIMIT),
        name="out_proj",
    )(x, m_a, m_b, w, w, final_w)


def kernel(x_prompt, x_sample, cache_k, cache_v, state_s, page_table, norm_w, w_in, gain_a, gain_b,
           sb_bias, lb_logits, w_out, final_norm_w):
    depth = norm_w.shape[0]
    assert depth == 1
    batch, seq, d_model = x_prompt.shape
    dec_batch, dec_seq, _ = x_sample.shape
    n_heads_a, head_dim = cache_k.shape[3:]
    w_a = n_heads_a * head_dim
    assert head_dim == HEAD_DIM and state_s.shape[3:] == (HEAD_DIM, HEAD_DIM)
    assert dec_seq <= SAMPLE_PAD and seq % SB_TILE == 0 and seq % (HGRN_CHUNK * HGRN_CHUNKS_PER_STEP) == 0
    assert (batch * seq) % IN_PROJ_ROWS == 0 and (batch * seq) % OUT_PROJ_ROWS == 0

    nw = norm_w[0][None, :]
    fw = final_norm_w[None, :]
    ga = gain_a[0][None, :]
    gb = gain_b[0][None, :]

    xp = x_prompt.reshape(batch * seq, d_model)
    xs = jnp.pad(x_sample, ((0, 0), (0, SAMPLE_PAD - dec_seq), (0, 0))).reshape(dec_batch * SAMPLE_PAD, d_model)
    p3, p3s = _in_proj(xp, xs, nw, w_in[0], bm=IN_PROJ_ROWS)
    m_a, k_p, v_p = _sb_prompt(p3, sb_bias, ga, batch, seq)
    m_b, s_p = _hgrn_prompt(p3, lb_logits, gb, batch, seq)
    y_prompt = _out_proj(xp, m_a, m_b, w_out[0], fw, bm=OUT_PROJ_ROWS).reshape(batch, seq, d_model)
    k_p = k_p.reshape(1, batch, seq, n_heads_a, head_dim)
    v_p = v_p.reshape(1, batch, seq, n_heads_a, head_dim)

    rows = lambda c: p3s[c].reshape(dec_batch, SAMPLE_PAD, -1)[:, :dec_seq]
    q_t = rows(COMP_QA).reshape(dec_batch, dec_seq, n_heads_a, head_dim).transpose(0, 2, 1, 3)
    q_rows = (q_t[:, :, None, :, :] * jnp.eye(n_heads_a, dtype=F32)[None, :, :, None, None]).reshape(
        dec_batch, n_heads_a, n_heads_a * dec_seq, head_dim).astype(BF16)
    bias_rows = jnp.broadcast_to(jnp.repeat(sb_bias[0], dec_seq)[:, None], (n_heads_a * dec_seq, head_dim))
    to_rows = lambda a: a.reshape(dec_batch, dec_seq, n_heads_a, head_dim).transpose(0, 2, 1, 3).reshape(
        dec_batch, n_heads_a * dec_seq, head_dim)
    z_r = to_rows(rows(COMP_ZA))
    gain_r = jnp.repeat(gain_a[0].reshape(n_heads_a, head_dim), dec_seq, axis=0)
    ck = cache_k[0].reshape(cache_k.shape[1], cache_k.shape[2] * n_heads_a, head_dim)
    cv = cache_v[0].reshape(cache_v.shape[1], cache_v.shape[2] * n_heads_a, head_dim)
    o_r = _sb_decode(page_table, q_rows, bias_rows, p3s, ck, cv, z_r, gain_r, dec_seq)
    m_a_s = o_r.reshape(dec_batch, n_heads_a, dec_seq, head_dim).transpose(0, 2, 1, 3).reshape(
        dec_batch, dec_seq, w_a)
    m_a_s = jnp.pad(m_a_s, ((0, 0), (0, SAMPLE_PAD - dec_seq), (0, 0))).reshape(dec_batch * SAMPLE_PAD, w_a)
    m_b_s, s_s = _hgrn_decode(p3s, lb_logits, gb, state_s[0], dec_seq)
    y_s = _out_proj(xs, m_a_s, m_b_s, w_out[0], fw, bm=dec_batch * SAMPLE_PAD)
    y_sample = y_s.reshape(dec_batch, SAMPLE_PAD, d_model)[:, :dec_seq]
    k_s = rows(COMP_KA).reshape(1, dec_batch, dec_seq, n_heads_a, head_dim)
    v_s = rows(COMP_VA).reshape(1, dec_batch, dec_seq, n_heads_a, head_dim)

    return (y_prompt, y_sample, k_p, v_p, s_p[None], k_s, v_s, s_s[None])
```

```python
import functools

import jax
import jax.numpy as jnp
from jax import lax
from jax.experimental import pallas as pl
from jax.experimental.pallas import tpu as pltpu

EPS = 1e-6
HEAD_DIM = 128
N_COMP = 8
COMP_QA, COMP_KA, COMP_VA, COMP_ZA, COMP_QB, COMP_FB, COMP_IB, COMP_ZB = range(N_COMP)
IN_PROJ_ROWS = 1024
OUT_PROJ_ROWS = 512
SB_BLOCK = 128
SB_TILE = 512
LOG2E = 1.4426950408889634
HGRN_CHUNK = 64
HGRN_CHUNKS_PER_STEP = 32
CUMSUM_GROUP = 256
HGRN_SUB = 16
SAMPLE_PAD = 8
DECODE_PAGES_PER_STEP = 8
DECODE_SLOTS = 3
VMEM_LIMIT = 60 * 1024 * 1024

F32 = jnp.float32
BF16 = jnp.bfloat16


def _dot(a, b, **kw):
    return jnp.dot(a, b, preferred_element_type=F32, **kw)


def _dot_nt(a, b):
    return lax.dot_general(a, b, (((1,), (1,)), ((), ())), preferred_element_type=F32)


def _dot_tn(a, b):
    return lax.dot_general(a, b, (((0,), (0,)), ((), ())), preferred_element_type=F32)


def _sigmoid(x):
    return 1.0 / (1.0 + jnp.exp(-x))


def _softplus2(z2):
    return jnp.maximum(z2, 0.0) + jnp.log2(1.0 + jnp.exp2(-jnp.abs(z2)))


def _split_bf16(x):
    hi = x.astype(BF16)
    lo = (x - hi.astype(F32)).astype(BF16)
    return hi, lo


def _in_proj_kernel(x_ref, xs_ref, nw_ref, w_ref, o_ref, os_ref, h_scr):
    m, n = pl.program_id(0), pl.program_id(1)

    def normed(x):
        ms = jnp.mean(x * x, axis=-1, keepdims=True)
        return (x * lax.rsqrt(ms + EPS) * nw_ref[...]).astype(BF16)

    @pl.when(n == 0)
    def _():
        h_scr[...] = normed(x_ref[...])

    @pl.when(m == 0)
    def _():
        os_ref[0] = _dot(normed(xs_ref[...]), w_ref[...].astype(BF16))

    o_ref[0] = _dot(h_scr[...], w_ref[...].astype(BF16))


def _in_proj(x, xs, norm_w, w, bm):
    t, d = x.shape
    ts = xs.shape[0]
    d_in = w.shape[1]
    bn = d_in // N_COMP
    return pl.pallas_call(
        _in_proj_kernel,
        grid=(t // bm, N_COMP),
        in_specs=[
            pl.BlockSpec((bm, d), lambda m, n: (m, 0)),
            pl.BlockSpec((ts, d), lambda m, n: (0, 0)),
            pl.BlockSpec((1, d), lambda m, n: (0, 0)),
            pl.BlockSpec((d, bn), lambda m, n: (0, n)),
        ],
        out_specs=[
            pl.BlockSpec((1, bm, bn), lambda m, n: (n, m, 0)),
            pl.BlockSpec((1, ts, bn), lambda m, n: (jnp.where(m == 0, n, N_COMP - 1), 0, 0)),
        ],
        out_shape=[
            jax.ShapeDtypeStruct((N_COMP, t, bn), F32),
            jax.ShapeDtypeStruct((N_COMP, ts, bn), F32),
        ],
        scratch_shapes=[pltpu.VMEM((bm, d), BF16)],
        compiler_params=pltpu.CompilerParams(
            dimension_semantics=("arbitrary", "arbitrary"), vmem_limit_bytes=VMEM_LIMIT),
        name="in_proj",
    )(x, xs, norm_w, w)


def _head_norm_gate(o, gain, z):
    ms = jnp.mean(o * o, axis=-1, keepdims=True)
    return o * lax.rsqrt(ms + EPS) * gain * (z * _sigmoid(z))


def _sb_prompt_kernel(bias_ref, q_ref, k_ref, v_ref, z_ref, gain_ref, mu_ref, o_ref, ko_ref, vo_ref,
                      kb_scr, vb_scr, acc_scr, carry_scr):
    tq = SB_TILE
    bk = SB_BLOCK
    t_len = q_ref.shape[1]
    bias2 = bias_ref[0, pl.program_id(1)] * LOG2E
    qscale = HEAD_DIM ** -0.5 * LOG2E
    kb_scr[...] = k_ref[0].astype(BF16)
    vb_scr[...] = v_ref[0].astype(BF16)
    ko_ref[...] = k_ref[0]
    vo_ref[...] = v_ref[0]
    gain = gain_ref[...]

    def sweep(z, carry, mask):
        sp = _softplus2(z)
        if mask is not None:
            sp = jnp.where(mask, sp, 0.0)
        zs = z - sp
        hi, lo = _split_bf16(sp)
        ws = [None] * (z.shape[1] // bk)
        for c in reversed(range(len(ws))):
            cols = slice(c * bk, (c + 1) * bk)
            r = _dot(jnp.concatenate([hi[:, cols], lo[:, cols]], axis=1), mu_ref[...])
            ws[c] = jnp.exp2(zs[:, cols] - r[:, :bk] - carry)
            carry = carry + r[:, bk:]
        w = jnp.concatenate(ws, axis=1)
        if mask is not None:
            w = jnp.where(mask, w, 0.0)
        return w.astype(BF16), carry

    def tile(q, k0, tk):
        keys = pl.ds(pl.multiple_of(k0, tq), tk)
        w, carry = sweep(_dot_nt(q, kb_scr[keys, :]) + bias2, carry_scr[...], None)
        carry_scr[...] = carry
        acc_scr[...] += _dot(w, vb_scr[keys, :])

    def own_tile(q, qs, with_prev):
        half = tq // 2
        carry = carry_scr[...]
        new_keys = pl.ds(pl.multiple_of(qs + half, half), half)
        row = lax.broadcasted_iota(jnp.int32, (half, half), 0)
        col = lax.broadcasted_iota(jnp.int32, (half, half), 1)
        w_new, carry_late = sweep(_dot_nt(q[half:], kb_scr[new_keys, :]) + bias2, carry[half:], col < row)
        acc_late = _dot(w_new, vb_scr[new_keys, :])
        carry = jnp.concatenate([carry[:half], carry_late], axis=0)

        n_old = tq + half if with_prev else half
        old_keys = pl.ds(pl.multiple_of(qs - (tq if with_prev else 0), half), n_old)
        row = lax.broadcasted_iota(jnp.int32, (tq, n_old), 0)
        col = lax.broadcasted_iota(jnp.int32, (tq, n_old), 1) - (n_old - half)
        w_old, carry = sweep(_dot_nt(q, kb_scr[old_keys, :]) + bias2, carry, (col < row) | (row >= half))
        carry_scr[...] = carry
        acc_scr[...] += (_dot(w_old, vb_scr[old_keys, :])
                         + jnp.concatenate([jnp.zeros((half, HEAD_DIM), F32), acc_late], axis=0))

    def q_tile(qt, _):
        qs = pl.multiple_of(qt * tq, tq)
        q = (q_ref[0, pl.ds(qs, tq), :] * qscale).astype(BF16)
        acc_scr[...] = jnp.zeros_like(acc_scr)
        carry_scr[...] = jnp.zeros_like(carry_scr)
        @pl.when(qt == 0)
        def _():
            own_tile(q, qs, False)

        @pl.when(qt > 0)
        def _():
            own_tile(q, qs, True)

        def k_pair(jj, _):
            tile(q, qs - tq - (jj + 1) * 2 * tq, 2 * tq)
            return 0

        lax.fori_loop(0, jnp.maximum(qt - 1, 0) // 2, k_pair, 0)

        @pl.when((qt > 0) & (qt % 2 == 0))
        def _():
            tile(q, 0, tq)

        zg = z_ref[0, pl.ds(qs, tq), :]
        o_ref[pl.ds(qs, tq), :] = _head_norm_gate(acc_scr[...], gain, zg).astype(o_ref.dtype)
        return 0

    lax.fori_loop(0, t_len // tq, q_tile, 0)


def _sb_matrix(n, pieces):
    j = jnp.arange(pieces * n)[:, None] % n
    s = jnp.arange(2 * n)[None, :]
    return jnp.where(s < n, (j > s), True).astype(BF16)


def _sb_prompt(p3, sb_bias, gain_a, batch, seq):
    n_heads = p3.shape[2] // HEAD_DIM
    mu = _sb_matrix(SB_BLOCK, 2)
    comp = lambda c: pl.BlockSpec((1, seq, HEAD_DIM), lambda b, h: (c, b, h))
    return pl.pallas_call(
        _sb_prompt_kernel,
        grid=(batch, n_heads),
        in_specs=[
            pl.BlockSpec(memory_space=pltpu.SMEM),
            comp(COMP_QA), comp(COMP_KA), comp(COMP_VA), comp(COMP_ZA),
            pl.BlockSpec((1, HEAD_DIM), lambda b, h: (0, h)),
            pl.BlockSpec(mu.shape, lambda b, h: (0, 0)),
        ],
        out_specs=[pl.BlockSpec((seq, HEAD_DIM), lambda b, h: (b, h))] * 3,
        out_shape=[
            jax.ShapeDtypeStruct((batch * seq, n_heads * HEAD_DIM), BF16),
            jax.ShapeDtypeStruct((batch * seq, n_heads * HEAD_DIM), p3.dtype),
            jax.ShapeDtypeStruct((batch * seq, n_heads * HEAD_DIM), p3.dtype),
        ],
        scratch_shapes=[pltpu.VMEM((seq, HEAD_DIM), BF16), pltpu.VMEM((seq, HEAD_DIM), BF16),
                        pltpu.VMEM((SB_TILE, HEAD_DIM), F32), pltpu.VMEM((SB_TILE, SB_BLOCK), F32)],
        compiler_params=pltpu.CompilerParams(
            dimension_semantics=("arbitrary", "arbitrary"), vmem_limit_bytes=VMEM_LIMIT),
        name="sb_prompt",
    )(sb_bias, p3, p3, p3, p3, gain_a, mu)


def _lower_bound(l):
    e = jnp.exp(l - jnp.max(l, axis=0, keepdims=True))
    return e[0:1, :] / jnp.sum(e, axis=0, keepdims=True)


def _chunk_cumsum(x, c_len):
    rows = x.shape[0]
    group = min(rows, CUMSUM_GROUP)
    ti = lax.broadcasted_iota(jnp.int32, (group, group), 0)
    si = lax.broadcasted_iota(jnp.int32, (group, group), 1)
    tri = jnp.where((si <= ti) & (si // c_len == ti // c_len), 1.0, 0.0).astype(BF16)
    hi = x.astype(BF16)
    rest = x - hi.astype(F32)
    mid = rest.astype(BF16)
    lo = (rest - mid.astype(F32)).astype(BF16)
    pieces = jnp.concatenate([hi, mid, lo], axis=1)
    outs = []
    for g0 in range(0, rows, group):
        r = _dot(tri, pieces[g0:g0 + group])
        outs.append((r[:, :HEAD_DIM] + r[:, HEAD_DIM:2 * HEAD_DIM]) + r[:, 2 * HEAD_DIM:])
    return jnp.concatenate(outs, axis=0)


def _hgrn_chunks(qb, fb, ib, lb, st, c_len, n_valid=None):
    rows = qb.shape[0]
    sub = min(HGRN_SUB, c_len)
    q = qb * _sigmoid(qb)
    g = lb + (1.0 - lb) * _sigmoid(fb)
    kk = 1.0 - g
    lg = jnp.log(g)
    if n_valid is not None:
        live = lax.broadcasted_iota(jnp.int32, (rows, HEAD_DIM), 0) < n_valid
        kk = jnp.where(live, kk, 0.0)
        lg = jnp.where(live, lg, 0.0)
    yield
    b = _chunk_cumsum(lg, c_len)
    vb = ib.astype(BF16)
    chunks = [slice(c * c_len, (c + 1) * c_len) for c in range(rows // c_len)]
    spans, operands = [], []
    for ch in chunks:
        for lo_r in range(ch.start, ch.stop, sub):
            hi_r = lo_r + sub
            m = b[lo_r + sub // 2:lo_r + sub // 2 + 1, :]
            qi = (q[lo_r:hi_r] * jnp.exp(b[lo_r:hi_r] - m)).astype(BF16)
            ki = (kk[ch.start:hi_r] * jnp.exp(m - b[ch.start:hi_r])).astype(BF16)
            spans.append((ch.start, lo_r, hi_r))
            operands.append((qi, ki))
    yield
    atts = [_dot_nt(qi, ki) for qi, ki in operands]
    masked = []
    for att, (c0, lo_r, hi_r) in zip(atts, spans):
        t_pos = lax.broadcasted_iota(jnp.int32, att.shape, 0) + (lo_r - c0)
        s_pos = lax.broadcasted_iota(jnp.int32, att.shape, 1)
        masked.append(jnp.where(s_pos <= t_pos, att, 0.0).astype(BF16))
    yield
    o_intra = jnp.concatenate([_dot(att, vb[c0:hi_r]) for att, (c0, _, hi_r) in zip(masked, spans)], axis=0)
    lasts = [b[ch.stop - 1:ch.stop, :] for ch in chunks]
    decayed = [(kk[ch] * jnp.exp(bl - b[ch])).astype(BF16) for ch, bl in zip(chunks, lasts)]
    yield
    incs = [_dot_tn(vb[ch], kd) for ch, kd in zip(chunks, decayed)]
    states = [st]
    for inc, bl in zip(incs, lasts):
        states.append(states[-1] * jnp.exp(bl) + inc)
    starts = [((q[ch] * jnp.exp(b[ch])).astype(BF16), s.astype(BF16)) for ch, s in zip(chunks, states)]
    yield
    o_inter = [_dot_nt(qe, s) for qe, s in starts]
    return o_intra + jnp.concatenate(o_inter, axis=0), states[-1]


def _in_lockstep(generators):
    values = [None] * len(generators)
    running = dict(enumerate(generators))
    while running:
        for i, gen in list(running.items()):
            try:
                next(gen)
            except StopIteration as done:
                values[i] = done.value
                del running[i]
    return values


def _hgrn_prompt_kernel(lbl_ref, q_ref, f_ref, i_ref, z_ref, gain_ref, o_ref, s_ref):
    r_len = HGRN_CHUNK * HGRN_CHUNKS_PER_STEP
    lb = _lower_bound(lbl_ref[...])
    gain = gain_ref[...]

    def step(ci, st):
        rs = pl.multiple_of(ci * r_len, r_len)
        rows = pl.ds(rs, r_len)
        (o, st), = _in_lockstep(
            [_hgrn_chunks(q_ref[0, rows, :], f_ref[0, rows, :], i_ref[0, rows, :], lb, st, HGRN_CHUNK)])
        o_ref[rows, :] = _head_norm_gate(o, gain, z_ref[0, rows, :]).astype(o_ref.dtype)
        return st

    st = lax.fori_loop(0, q_ref.shape[1] // r_len, step, jnp.zeros((HEAD_DIM, HEAD_DIM), F32))
    s_ref[0, 0] = st.T


def _hgrn_prompt(p3, lb_logits, gain_b, batch, seq):
    n_heads = p3.shape[2] // HEAD_DIM
    comp = lambda c: pl.BlockSpec((1, seq, HEAD_DIM), lambda b, h: (c, b, h))
    return pl.pallas_call(
        _hgrn_prompt_kernel,
        grid=(batch, n_heads),
        in_specs=[
            pl.BlockSpec((lb_logits.shape[0], HEAD_DIM), lambda b, h: (0, h)),
            comp(COMP_QB), comp(COMP_FB), comp(COMP_IB), comp(COMP_ZB),
            pl.BlockSpec((1, HEAD_DIM), lambda b, h: (0, h)),
        ],
        out_specs=[
            pl.BlockSpec((seq, HEAD_DIM), lambda b, h: (b, h)),
            pl.BlockSpec((1, 1, HEAD_DIM, HEAD_DIM), lambda b, h: (b, h, 0, 0)),
        ],
        out_shape=[
            jax.ShapeDtypeStruct((batch * seq, n_heads * HEAD_DIM), BF16),
            jax.ShapeDtypeStruct((batch, n_heads, HEAD_DIM, HEAD_DIM), F32),
        ],
        compiler_params=pltpu.CompilerParams(
            dimension_semantics=("arbitrary", "arbitrary"), vmem_limit_bytes=VMEM_LIMIT),
        name="hgrn_prompt",
    )(lb_logits, p3, p3, p3, p3, gain_b)


def _hgrn_decode_kernel(lbl_ref, q_ref, f_ref, i_ref, z_ref, gain_ref, s0_ref, o_ref, s_ref, *, n_valid):
    n_heads = s0_ref.shape[1]
    head = lambda h: slice(h * HEAD_DIM, (h + 1) * HEAD_DIM)
    lb = _lower_bound(lbl_ref[...])
    results = _in_lockstep([
        _hgrn_chunks(q_ref[0, :, head(h)], f_ref[0, :, head(h)], i_ref[0, :, head(h)], lb[:, head(h)],
                     s0_ref[0, h].T, SAMPLE_PAD, n_valid=n_valid)
        for h in range(n_heads)])
    for h, (o, st) in enumerate(results):
        o_ref[:, head(h)] = _head_norm_gate(o, gain_ref[:, head(h)], z_ref[0, :, head(h)]).astype(o_ref.dtype)
        s_ref[0, h] = st.T


def _hgrn_decode(p3s, lb_logits, gain_b, state, n_valid):
    dec_batch, n_heads = state.shape[:2]
    width = n_heads * HEAD_DIM
    comp = lambda c: pl.BlockSpec((1, SAMPLE_PAD, width), lambda b: (c, b, 0))
    state_spec = pl.BlockSpec((1, n_heads, HEAD_DIM, HEAD_DIM), lambda b: (b, 0, 0, 0))
    return pl.pallas_call(
        functools.partial(_hgrn_decode_kernel, n_valid=n_valid),
        grid=(dec_batch,),
        in_specs=[
            pl.BlockSpec(lb_logits.shape, lambda b: (0, 0)),
            comp(COMP_QB), comp(COMP_FB), comp(COMP_IB), comp(COMP_ZB),
            pl.BlockSpec((1, width), lambda b: (0, 0)),
            state_spec,
        ],
        out_specs=[pl.BlockSpec((SAMPLE_PAD, width), lambda b: (b, 0)), state_spec],
        out_shape=[
            jax.ShapeDtypeStruct((dec_batch * SAMPLE_PAD, width), BF16),
            jax.ShapeDtypeStruct(state.shape, F32),
        ],
        compiler_params=pltpu.CompilerParams(dimension_semantics=("arbitrary",)),
        name="hgrn_decode",
    )(lb_logits, p3s, p3s, p3s, p3s, gain_b, state)


def _sb_decode_kernel(pt_ref, qr_ref, bias_ref, kn_ref, vn_ref, *rest, n_valid, n_pg):
    ck_hbm, cv_hbm, mu_ref, z_ref, gain_ref, o_ref, acc_scr, carry_scr, kbuf, vbuf, sems = rest
    j = pl.program_id(1)
    n_steps = pl.num_programs(1)
    step = pl.program_id(0) * n_steps + j
    n_slots = kbuf.shape[0]
    n_heads, n_rows = acc_scr.shape[:2]
    n_q = n_rows // n_heads
    page = kbuf.shape[2] // n_heads

    def page_copies(t):
        seq, js, slot = t // n_steps, t % n_steps, t % n_slots
        copies = []
        for i in range(n_pg):
            src = pt_ref[seq, n_steps * n_pg - 1 - (js * n_pg + i)]
            copies.append(pltpu.make_async_copy(ck_hbm.at[src], kbuf.at[slot, i], sems.at[slot, 0, i]))
            copies.append(pltpu.make_async_copy(cv_hbm.at[src], vbuf.at[slot, i], sems.at[slot, 1, i]))
        return copies

    @pl.when(step == 0)
    def _():
        for t in range(n_slots - 1):
            for copy in page_copies(t):
                copy.start()

    @pl.when(step + n_slots - 1 < pl.num_programs(0) * n_steps)
    def _():
        for copy in page_copies(step + n_slots - 1):
            copy.start()

    zscale = HEAD_DIM ** -0.5 * LOG2E
    bias2 = bias_ref[...] * LOG2E
    head = lambda h: slice(h * HEAD_DIM, (h + 1) * HEAD_DIM)

    def scores(k_heads):
        z = sum(_dot_nt(qr_ref[0, h], k_heads[h].astype(BF16)) for h in range(n_heads))
        return z * zscale + bias2[:, :z.shape[1]]

    @pl.when(j == 0)
    def _():
        r = kn_ref.shape[1]
        z = scores([kn_ref[0, :, head(h)] for h in range(n_heads)])
        t_row = lax.broadcasted_iota(jnp.int32, (n_rows, r), 0) % n_q
        s_col = lax.broadcasted_iota(jnp.int32, (n_rows, r), 1)
        valid = (s_col < t_row) & (s_col < n_valid)
        sp = jnp.where(valid, _softplus2(z), 0.0)
        later_mat = jnp.where(lax.broadcasted_iota(jnp.int32, (r, r), 0)
                              > lax.broadcasted_iota(jnp.int32, (r, r), 1), 1.0, 0.0)
        later = _dot(sp, later_mat, precision=lax.Precision.HIGHEST)
        wb = jnp.where(valid, jnp.exp2(z - sp - later), 0.0).astype(BF16)
        carry_scr[...] = jnp.broadcast_to(jnp.sum(sp, axis=1, keepdims=True), carry_scr.shape)
        for h in range(n_heads):
            acc_scr[h] = _dot(wb, vn_ref[0, :, head(h)].astype(BF16))

    for copy in page_copies(step):
        copy.wait()
    slot = step % n_slots
    kp_refs = [kbuf.at[slot, i] for i in range(n_pg)]
    vp_refs = [vbuf.at[slot, i] for i in range(n_pg)]
    zs = [scores([kp_ref[pl.ds(h, page, stride=n_heads), :] for h in range(n_heads)]) for kp_ref in kp_refs]
    sps = [_softplus2(z) for z in zs]
    rs = [_dot(jnp.concatenate(_split_bf16(sp), axis=1), mu_ref[...]) for sp in sps]
    carry = carry_scr[...]
    wbs = []
    for z, sp, r in zip(zs, sps, rs):
        wbs.append(jnp.exp2(z - sp - r[:, :page] - carry).astype(BF16))
        carry = carry + r[:, page:]
    carry_scr[...] = carry
    for h in range(n_heads):
        acc_scr[h] += sum(_dot(wb, vp_ref[pl.ds(h, page, stride=n_heads), :].astype(BF16))
                          for wb, vp_ref in zip(wbs, vp_refs))

    @pl.when(j == pl.num_programs(1) - 1)
    def _():
        rh = lax.broadcasted_iota(jnp.int32, (n_rows, HEAD_DIM), 0) // n_q
        o = jnp.zeros((n_rows, HEAD_DIM), F32)
        for h in range(n_heads):
            o = o + jnp.where(rh == h, acc_scr[h], 0.0)
        o_ref[0] = _head_norm_gate(o, gain_ref[...], z_ref[0]).astype(o_ref.dtype)


def _sb_decode(page_table, q_rows, bias_rows, p3s, cache_k, cache_v, z_r, gain_r, n_valid):
    dec_batch, n_pages = page_table.shape
    page_rows = cache_k.shape[1]
    n_heads, n_rows = q_rows.shape[1:3]
    width = n_heads * HEAD_DIM
    page = page_rows // n_heads
    assert page == HEAD_DIM
    mu = _sb_matrix(page, 2)
    new = lambda c: pl.BlockSpec((1, SAMPLE_PAD, width), lambda b, j, pt: (c, b, 0))
    n_pg = DECODE_PAGES_PER_STEP
    assert n_pages % n_pg == 0 and dec_batch * (n_pages // n_pg) >= DECODE_SLOTS - 1
    page_buf = pltpu.VMEM((DECODE_SLOTS, n_pg, page_rows, HEAD_DIM), cache_k.dtype)
    grid_spec = pltpu.PrefetchScalarGridSpec(
        num_scalar_prefetch=1,
        grid=(dec_batch, n_pages // n_pg),
        in_specs=[
            pl.BlockSpec((1, n_heads, n_rows, HEAD_DIM), lambda b, j, pt: (b, 0, 0, 0)),
            pl.BlockSpec((n_rows, HEAD_DIM), lambda b, j, pt: (0, 0)),
            new(COMP_KA), new(COMP_VA),
            pl.BlockSpec(memory_space=pl.ANY), pl.BlockSpec(memory_space=pl.ANY),
            pl.BlockSpec(mu.shape, lambda b, j, pt: (0, 0)),
            pl.BlockSpec((1, n_rows, HEAD_DIM), lambda b, j, pt: (b, 0, 0)),
            pl.BlockSpec((n_rows, HEAD_DIM), lambda b, j, pt: (0, 0)),
        ],
        out_specs=pl.BlockSpec((1, n_rows, HEAD_DIM), lambda b, j, pt: (b, 0, 0)),
        scratch_shapes=[pltpu.VMEM((n_heads, n_rows, HEAD_DIM), F32), pltpu.VMEM((n_rows, HEAD_DIM), F32),
                        page_buf, page_buf, pltpu.SemaphoreType.DMA((DECODE_SLOTS, 2, n_pg))],
    )
    return pl.pallas_call(
        functools.partial(_sb_decode_kernel, n_valid=n_valid, n_pg=n_pg),
        grid_spec=grid_spec,
        out_shape=jax.ShapeDtypeStruct((dec_batch, n_rows, HEAD_DIM), BF16),
        compiler_params=pltpu.CompilerParams(
            dimension_semantics=("arbitrary", "arbitrary"), vmem_limit_bytes=VMEM_LIMIT),
        name="sb_decode",
    )(page_table, q_rows, bias_rows, p3s, p3s, cache_k, cache_v, mu, z_r, gain_r)


def _out_proj_kernel(x_ref, ma_ref, mb_ref, xs_ref, mas_ref, mbs_ref, wa_ref, wb_ref, fw_ref, o_ref, os_ref):
    wa = wa_ref[...].astype(BF16)
    wb = wb_ref[...].astype(BF16)

    def project(x, m_a, m_b):
        y = x + _dot(m_a, wa) + _dot(m_b, wb)
        ms = jnp.mean(y * y, axis=-1, keepdims=True)
        return y * lax.rsqrt(ms + EPS) * fw_ref[...]

    o_ref[...] = project(x_ref[...], ma_ref[...], mb_ref[...])

    @pl.when(pl.program_id(0) == 0)
    def _():
        os_ref[...] = project(xs_ref[...], mas_ref[...], mbs_ref[...])


def _out_proj(x, m_a, m_b, xs, m_a_s, m_b_s, w, final_w, bm):
    t, d = x.shape
    ts = xs.shape[0]
    wa = m_a.shape[1]
    wb = m_b.shape[1]
    assert wa == wb
    whole = lambda rows, cols: pl.BlockSpec((rows, cols), lambda m: (0, 0))
    return pl.pallas_call(
        _out_proj_kernel,
        grid=(t // bm,),
        in_specs=[
            pl.BlockSpec((bm, d), lambda m: (m, 0)),
            pl.BlockSpec((bm, wa), lambda m: (m, 0)),
            pl.BlockSpec((bm, wb), lambda m: (m, 0)),
            whole(ts, d), whole(ts, wa), whole(ts, wb),
            pl.BlockSpec((wa, d), lambda m: (0, 0)),
            pl.BlockSpec((wb, d), lambda m: (1, 0)),
            whole(1, d),
        ],
        out_specs=[pl.BlockSpec((bm, d), lambda m: (m, 0)), whole(ts, d)],
        out_shape=[jax.ShapeDtypeStruct((t, d), F32), jax.ShapeDtypeStruct((ts, d), F32)],
        compiler_params=pltpu.CompilerParams(
            dimension_semantics=("arbitrary",), vmem_limit_bytes=VMEM_LIMIT),
        name="out_proj",
    )(x, m_a, m_b, xs, m_a_s, m_b_s, w, w, final_w)


def kernel(x_prompt, x_sample, cache_k, cache_v, state_s, page_table, norm_w, w_in, gain_a, gain_b,
           sb_bias, lb_logits, w_out, final_norm_w):
    depth = norm_w.shape[0]
    assert depth == 1
    batch, seq, d_model = x_prompt.shape
    dec_batch, dec_seq, _ = x_sample.shape
    n_heads_a, head_dim = cache_k.shape[3:]
    w_a = n_heads_a * head_dim
    assert head_dim == HEAD_DIM and state_s.shape[3:] == (HEAD_DIM, HEAD_DIM)
    assert dec_seq <= SAMPLE_PAD and seq % SB_TILE == 0 and seq % (HGRN_CHUNK * HGRN_CHUNKS_PER_STEP) == 0
    assert (batch * seq) % IN_PROJ_ROWS == 0 and (batch * seq) % OUT_PROJ_ROWS == 0

    nw = norm_w[0][None, :]
    fw = final_norm_w[None, :]
    ga = gain_a[0][None, :]
    gb = gain_b[0][None, :]

    xp = x_prompt.reshape(batch * seq, d_model)
    xs = jnp.pad(x_sample, ((0, 0), (0, SAMPLE_PAD - dec_seq), (0, 0))).reshape(dec_batch * SAMPLE_PAD, d_model)
    p3, p3s = _in_proj(xp, xs, nw, w_in[0], bm=IN_PROJ_ROWS)
    m_a, k_p, v_p = _sb_prompt(p3, sb_bias, ga, batch, seq)
    m_b, s_p = _hgrn_prompt(p3, lb_logits, gb, batch, seq)
    k_p = k_p.reshape(1, batch, seq, n_heads_a, head_dim)
    v_p = v_p.reshape(1, batch, seq, n_heads_a, head_dim)

    rows = lambda c: p3s[c].reshape(dec_batch, SAMPLE_PAD, -1)[:, :dec_seq]
    q_t = rows(COMP_QA).reshape(dec_batch, dec_seq, n_heads_a, head_dim).transpose(0, 2, 1, 3)
    q_rows = (q_t[:, :, None, :, :] * jnp.eye(n_heads_a, dtype=F32)[None, :, :, None, None]).reshape(
        dec_batch, n_heads_a, n_heads_a * dec_seq, head_dim).astype(BF16)
    bias_rows = jnp.broadcast_to(jnp.repeat(sb_bias[0], dec_seq)[:, None], (n_heads_a * dec_seq, head_dim))
    to_rows = lambda a: a.reshape(dec_batch, dec_seq, n_heads_a, head_dim).transpose(0, 2, 1, 3).reshape(
        dec_batch, n_heads_a * dec_seq, head_dim)
    z_r = to_rows(rows(COMP_ZA))
    gain_r = jnp.repeat(gain_a[0].reshape(n_heads_a, head_dim), dec_seq, axis=0)
    ck = cache_k[0].reshape(cache_k.shape[1], cache_k.shape[2] * n_heads_a, head_dim)
    cv = cache_v[0].reshape(cache_v.shape[1], cache_v.shape[2] * n_heads_a, head_dim)
    o_r = _sb_decode(page_table, q_rows, bias_rows, p3s, ck, cv, z_r, gain_r, dec_seq)
    m_a_s = o_r.reshape(dec_batch, n_heads_a, dec_seq, head_dim).transpose(0, 2, 1, 3).reshape(
        dec_batch, dec_seq, w_a)
    m_a_s = jnp.pad(m_a_s, ((0, 0), (0, SAMPLE_PAD - dec_seq), (0, 0))).reshape(dec_batch * SAMPLE_PAD, w_a)
    m_b_s, s_s = _hgrn_decode(p3s, lb_logits, gb, state_s[0], dec_seq)
    y_p, y_s = _out_proj(xp, m_a, m_b, xs, m_a_s, m_b_s, w_out[0], fw, bm=OUT_PROJ_ROWS)
    y_prompt = y_p.reshape(batch, seq, d_model)
    y_sample = y_s.reshape(dec_batch, SAMPLE_PAD, d_model)[:, :dec_seq]
    k_s = rows(COMP_KA).reshape(1, dec_batch, dec_seq, n_heads_a, head_dim)
    v_s = rows(COMP_VA).reshape(1, dec_batch, dec_seq, n_heads_a, head_dim)

    return (y_prompt, y_sample, k_p, v_p, s_p[None], k_s, v_s, s_s[None])
```

```python
import functools

import jax
import jax.numpy as jnp
from jax import lax
from jax.experimental import pallas as pl
from jax.experimental.pallas import tpu as pltpu

EPS = 1e-6
HEAD_DIM = 128
N_COMP = 8
COMP_QA, COMP_KA, COMP_VA, COMP_ZA, COMP_QB, COMP_FB, COMP_IB, COMP_ZB = range(N_COMP)
IN_PROJ_ROWS = 1024
OUT_PROJ_ROWS = 512
SB_BLOCK = 128
SB_TILE = 512
LOG2E = 1.4426950408889634
HGRN_CHUNK = 64
CUMSUM_GROUP = 256
HGRN_SUB = 16
SAMPLE_PAD = 8
DECODE_PAGES_PER_STEP = 8
DECODE_SLOTS = 3
VMEM_LIMIT = 60 * 1024 * 1024

F32 = jnp.float32
BF16 = jnp.bfloat16


def _dot(a, b, **kw):
    return jnp.dot(a, b, preferred_element_type=F32, **kw)


def _dot_nt(a, b):
    return lax.dot_general(a, b, (((1,), (1,)), ((), ())), preferred_element_type=F32)


def _dot_tn(a, b):
    return lax.dot_general(a, b, (((0,), (0,)), ((), ())), preferred_element_type=F32)


def _sigmoid(x):
    return 1.0 / (1.0 + jnp.exp(-x))


def _softplus2(z2):
    return jnp.maximum(z2, 0.0) + jnp.log2(1.0 + jnp.exp2(-jnp.abs(z2)))


def _split_bf16(x):
    hi = x.astype(BF16)
    lo = (x - hi.astype(F32)).astype(BF16)
    return hi, lo


def _in_proj_kernel(x_ref, xs_ref, nw_ref, w_ref, o_ref, os_ref, h_scr):
    m, n = pl.program_id(0), pl.program_id(1)

    def normed(x):
        ms = jnp.mean(x * x, axis=-1, keepdims=True)
        return (x * lax.rsqrt(ms + EPS) * nw_ref[...]).astype(BF16)

    @pl.when(n == 0)
    def _():
        h_scr[...] = normed(x_ref[...])

    @pl.when(m == 0)
    def _():
        os_ref[0] = _dot(normed(xs_ref[...]), w_ref[...].astype(BF16))

    o_ref[0] = _dot(h_scr[...], w_ref[...].astype(BF16))


def _in_proj(x, xs, norm_w, w, bm):
    t, d = x.shape
    ts = xs.shape[0]
    d_in = w.shape[1]
    bn = d_in // N_COMP
    return pl.pallas_call(
        _in_proj_kernel,
        grid=(t // bm, N_COMP),
        in_specs=[
            pl.BlockSpec((bm, d), lambda m, n: (m, 0)),
            pl.BlockSpec((ts, d), lambda m, n: (0, 0)),
            pl.BlockSpec((1, d), lambda m, n: (0, 0)),
            pl.BlockSpec((d, bn), lambda m, n: (0, n)),
        ],
        out_specs=[
            pl.BlockSpec((1, bm, bn), lambda m, n: (n, m, 0)),
            pl.BlockSpec((1, ts, bn), lambda m, n: (jnp.where(m == 0, n, N_COMP - 1), 0, 0)),
        ],
        out_shape=[
            jax.ShapeDtypeStruct((N_COMP, t, bn), F32),
            jax.ShapeDtypeStruct((N_COMP, ts, bn), F32),
        ],
        scratch_shapes=[pltpu.VMEM((bm, d), BF16)],
        compiler_params=pltpu.CompilerParams(
            dimension_semantics=("arbitrary", "arbitrary"), vmem_limit_bytes=VMEM_LIMIT),
        name="in_proj",
    )(x, xs, norm_w, w)


def _head_norm_gate(o, gain, z):
    ms = jnp.mean(o * o, axis=-1, keepdims=True)
    return o * lax.rsqrt(ms + EPS) * gain * (z * _sigmoid(z))


def _sb_prompt_kernel(bias_ref, q_ref, k_ref, v_ref, z_ref, gain_ref, mu_ref, o_ref, ko_ref, vo_ref,
                      kb_scr, vb_scr, acc_scr, carry_scr):
    tq = SB_TILE
    bk = SB_BLOCK
    t_len = q_ref.shape[1]
    bias2 = bias_ref[0, pl.program_id(1)] * LOG2E
    qscale = HEAD_DIM ** -0.5 * LOG2E
    kb_scr[...] = k_ref[0].astype(BF16)
    vb_scr[...] = v_ref[0].astype(BF16)
    ko_ref[...] = k_ref[0]
    vo_ref[...] = v_ref[0]
    gain = gain_ref[...]

    def sweep(z, carry, mask):
        sp = _softplus2(z)
        if mask is not None:
            sp = jnp.where(mask, sp, 0.0)
        zs = z - sp
        hi, lo = _split_bf16(sp)
        ws = [None] * (z.shape[1] // bk)
        for c in reversed(range(len(ws))):
            cols = slice(c * bk, (c + 1) * bk)
            r = _dot(jnp.concatenate([hi[:, cols], lo[:, cols]], axis=1), mu_ref[...])
            ws[c] = jnp.exp2(zs[:, cols] - r[:, :bk] - carry)
            carry = carry + r[:, bk:]
        w = jnp.concatenate(ws, axis=1)
        if mask is not None:
            w = jnp.where(mask, w, 0.0)
        return w.astype(BF16), carry

    def tile(q, k0, tk):
        keys = pl.ds(pl.multiple_of(k0, tq), tk)
        w, carry = sweep(_dot_nt(q, kb_scr[keys, :]) + bias2, carry_scr[...], None)
        carry_scr[...] = carry
        acc_scr[...] += _dot(w, vb_scr[keys, :])

    def own_tile(q, qs, with_prev):
        half = tq // 2
        carry = carry_scr[...]
        new_keys = pl.ds(pl.multiple_of(qs + half, half), half)
        row = lax.broadcasted_iota(jnp.int32, (half, half), 0)
        col = lax.broadcasted_iota(jnp.int32, (half, half), 1)
        w_new, carry_late = sweep(_dot_nt(q[half:], kb_scr[new_keys, :]) + bias2, carry[half:], col < row)
        acc_late = _dot(w_new, vb_scr[new_keys, :])
        carry = jnp.concatenate([carry[:half], carry_late], axis=0)

        n_old = tq + half if with_prev else half
        old_keys = pl.ds(pl.multiple_of(qs - (tq if with_prev else 0), half), n_old)
        row = lax.broadcasted_iota(jnp.int32, (tq, n_old), 0)
        col = lax.broadcasted_iota(jnp.int32, (tq, n_old), 1) - (n_old - half)
        w_old, carry = sweep(_dot_nt(q, kb_scr[old_keys, :]) + bias2, carry, (col < row) | (row >= half))
        carry_scr[...] = carry
        acc_scr[...] += (_dot(w_old, vb_scr[old_keys, :])
                         + jnp.concatenate([jnp.zeros((half, HEAD_DIM), F32), acc_late], axis=0))

    def q_tile(qt, _):
        qs = pl.multiple_of(qt * tq, tq)
        q = (q_ref[0, pl.ds(qs, tq), :] * qscale).astype(BF16)
        acc_scr[...] = jnp.zeros_like(acc_scr)
        carry_scr[...] = jnp.zeros_like(carry_scr)
        @pl.when(qt == 0)
        def _():
            own_tile(q, qs, False)

        @pl.when(qt > 0)
        def _():
            own_tile(q, qs, True)

        def k_pair(jj, _):
            tile(q, qs - tq - (jj + 1) * 2 * tq, 2 * tq)
            return 0

        lax.fori_loop(0, jnp.maximum(qt - 1, 0) // 2, k_pair, 0)

        @pl.when((qt > 0) & (qt % 2 == 0))
        def _():
            tile(q, 0, tq)

        zg = z_ref[0, pl.ds(qs, tq), :]
        o_ref[pl.ds(qs, tq), :] = _head_norm_gate(acc_scr[...], gain, zg).astype(o_ref.dtype)
        return 0

    lax.fori_loop(0, t_len // tq, q_tile, 0)


def _sb_matrix(n, pieces):
    j = jnp.arange(pieces * n)[:, None] % n
    s = jnp.arange(2 * n)[None, :]
    return jnp.where(s < n, (j > s), True).astype(BF16)


def _sb_prompt(p3, sb_bias, gain_a, batch, seq):
    n_heads = p3.shape[2] // HEAD_DIM
    mu = _sb_matrix(SB_BLOCK, 2)
    comp = lambda c: pl.BlockSpec((1, seq, HEAD_DIM), lambda b, h: (c, b, h))
    return pl.pallas_call(
        _sb_prompt_kernel,
        grid=(batch, n_heads),
        in_specs=[
            pl.BlockSpec(memory_space=pltpu.SMEM),
            comp(COMP_QA), comp(COMP_KA), comp(COMP_VA), comp(COMP_ZA),
            pl.BlockSpec((1, HEAD_DIM), lambda b, h: (0, h)),
            pl.BlockSpec(mu.shape, lambda b, h: (0, 0)),
        ],
        out_specs=[pl.BlockSpec((seq, HEAD_DIM), lambda b, h: (b, h))] * 3,
        out_shape=[
            jax.ShapeDtypeStruct((batch * seq, n_heads * HEAD_DIM), BF16),
            jax.ShapeDtypeStruct((batch * seq, n_heads * HEAD_DIM), p3.dtype),
            jax.ShapeDtypeStruct((batch * seq, n_heads * HEAD_DIM), p3.dtype),
        ],
        scratch_shapes=[pltpu.VMEM((seq, HEAD_DIM), BF16), pltpu.VMEM((seq, HEAD_DIM), BF16),
                        pltpu.VMEM((SB_TILE, HEAD_DIM), F32), pltpu.VMEM((SB_TILE, SB_BLOCK), F32)],
        compiler_params=pltpu.CompilerParams(
            dimension_semantics=("arbitrary", "arbitrary"), vmem_limit_bytes=VMEM_LIMIT),
        name="sb_prompt",
    )(sb_bias, p3, p3, p3, p3, gain_a, mu)


def _lower_bound(l):
    e = jnp.exp(l - jnp.max(l, axis=0, keepdims=True))
    return e[0:1, :] / jnp.sum(e, axis=0, keepdims=True)


def _chunk_cumsum(x, c_len):
    rows = x.shape[0]
    group = min(rows, CUMSUM_GROUP)
    ti = lax.broadcasted_iota(jnp.int32, (group, group), 0)
    si = lax.broadcasted_iota(jnp.int32, (group, group), 1)
    tri = jnp.where((si <= ti) & (si // c_len == ti // c_len), 1.0, 0.0).astype(BF16)
    hi = x.astype(BF16)
    rest = x - hi.astype(F32)
    mid = rest.astype(BF16)
    lo = (rest - mid.astype(F32)).astype(BF16)
    pieces = jnp.concatenate([hi, mid, lo], axis=1)
    outs = []
    for g0 in range(0, rows, group):
        r = _dot(tri, pieces[g0:g0 + group])
        outs.append((r[:, :HEAD_DIM] + r[:, HEAD_DIM:2 * HEAD_DIM]) + r[:, 2 * HEAD_DIM:])
    return jnp.concatenate(outs, axis=0)


def _hgrn_chunks(qb, fb, ib, lb, st, c_len, n_valid=None):
    rows = qb.shape[0]
    sub = min(HGRN_SUB, c_len)
    q = qb * _sigmoid(qb)
    g = lb + (1.0 - lb) * _sigmoid(fb)
    kk = 1.0 - g
    lg = jnp.log(g)
    if n_valid is not None:
        live = lax.broadcasted_iota(jnp.int32, (rows, HEAD_DIM), 0) < n_valid
        kk = jnp.where(live, kk, 0.0)
        lg = jnp.where(live, lg, 0.0)
    yield
    b = _chunk_cumsum(lg, c_len)
    vb = ib.astype(BF16)
    chunks = [slice(c * c_len, (c + 1) * c_len) for c in range(rows // c_len)]
    spans, operands = [], []
    for ch in chunks:
        for lo_r in range(ch.start, ch.stop, sub):
            hi_r = lo_r + sub
            m = b[lo_r + sub // 2:lo_r + sub // 2 + 1, :]
            qi = (q[lo_r:hi_r] * jnp.exp(b[lo_r:hi_r] - m)).astype(BF16)
            ki = (kk[ch.start:hi_r] * jnp.exp(m - b[ch.start:hi_r])).astype(BF16)
            spans.append((ch.start, lo_r, hi_r))
            operands.append((qi, ki))
    yield
    atts = [_dot_nt(qi, ki) for qi, ki in operands]
    masked = []
    for att, (c0, lo_r, hi_r) in zip(atts, spans):
        t_pos = lax.broadcasted_iota(jnp.int32, att.shape, 0) + (lo_r - c0)
        s_pos = lax.broadcasted_iota(jnp.int32, att.shape, 1)
        masked.append(jnp.where(s_pos <= t_pos, att, 0.0).astype(BF16))
    yield
    o_intra = jnp.concatenate([_dot(att, vb[c0:hi_r]) for att, (c0, _, hi_r) in zip(masked, spans)], axis=0)
    lasts = [b[ch.stop - 1:ch.stop, :] for ch in chunks]
    decayed = [(kk[ch] * jnp.exp(bl - b[ch])).astype(BF16) for ch, bl in zip(chunks, lasts)]
    yield
    incs = [_dot_tn(vb[ch], kd) for ch, kd in zip(chunks, decayed)]
    states = [st]
    for inc, bl in zip(incs, lasts):
        states.append(states[-1] * jnp.exp(bl) + inc)
    starts = [((q[ch] * jnp.exp(b[ch])).astype(BF16), s.astype(BF16)) for ch, s in zip(chunks, states)]
    yield
    o_inter = [_dot_nt(qe, s) for qe, s in starts]
    return o_intra + jnp.concatenate(o_inter, axis=0), states[-1]


def _in_lockstep(generators):
    values = [None] * len(generators)
    running = dict(enumerate(generators))
    while running:
        for i, gen in list(running.items()):
            try:
                next(gen)
            except StopIteration as done:
                values[i] = done.value
                del running[i]
    return values


def _hgrn_decode_kernel(lbl_ref, q_ref, f_ref, i_ref, z_ref, gain_ref, s0_ref, o_ref, s_ref, *, n_valid):
    n_heads = s0_ref.shape[1]
    head = lambda h: slice(h * HEAD_DIM, (h + 1) * HEAD_DIM)
    lb = _lower_bound(lbl_ref[...])
    results = _in_lockstep([
        _hgrn_chunks(q_ref[0, :, head(h)], f_ref[0, :, head(h)], i_ref[0, :, head(h)], lb[:, head(h)],
                     s0_ref[0, h].T, SAMPLE_PAD, n_valid=n_valid)
        for h in range(n_heads)])
    for h, (o, st) in enumerate(results):
        o_ref[:, head(h)] = _head_norm_gate(o, gain_ref[:, head(h)], z_ref[0, :, head(h)]).astype(o_ref.dtype)
        s_ref[0, h] = st.T


def _hgrn_decode(p3s, lb_logits, gain_b, state, n_valid):
    dec_batch, n_heads = state.shape[:2]
    width = n_heads * HEAD_DIM
    comp = lambda c: pl.BlockSpec((1, SAMPLE_PAD, width), lambda b: (c, b, 0))
    state_spec = pl.BlockSpec((1, n_heads, HEAD_DIM, HEAD_DIM), lambda b: (b, 0, 0, 0))
    return pl.pallas_call(
        functools.partial(_hgrn_decode_kernel, n_valid=n_valid),
        grid=(dec_batch,),
        in_specs=[
            pl.BlockSpec(lb_logits.shape, lambda b: (0, 0)),
            comp(COMP_QB), comp(COMP_FB), comp(COMP_IB), comp(COMP_ZB),
            pl.BlockSpec((1, width), lambda b: (0, 0)),
            state_spec,
        ],
        out_specs=[pl.BlockSpec((SAMPLE_PAD, width), lambda b: (b, 0)), state_spec],
        out_shape=[
            jax.ShapeDtypeStruct((dec_batch * SAMPLE_PAD, width), BF16),
            jax.ShapeDtypeStruct(state.shape, F32),
        ],
        compiler_params=pltpu.CompilerParams(dimension_semantics=("arbitrary",)),
        name="hgrn_decode",
    )(lb_logits, p3s, p3s, p3s, p3s, gain_b, state)


def _sb_decode_kernel(pt_ref, qr_ref, bias_ref, kn_ref, vn_ref, ck_hbm, cv_hbm, mu_ref, z_ref, gain_ref,
                      lbl_ref, hq_ref, hf_ref, hi_ref, hz_ref, hgain_ref, o_ref, hb_ref, hs_ref,
                      acc_scr, carry_scr, kbuf, vbuf, sems, st_scr, *, n_valid, n_pg, h_steps):
    j = pl.program_id(1)
    n_steps = pl.num_programs(1)
    step = pl.program_id(0) * n_steps + j
    n_slots = kbuf.shape[0]
    n_heads, n_rows = acc_scr.shape[:2]
    n_q = n_rows // n_heads
    page = kbuf.shape[2] // n_heads

    def page_copies(t):
        seq, js, slot = t // n_steps, t % n_steps, t % n_slots
        copies = []
        for i in range(n_pg):
            src = pt_ref[seq, n_steps * n_pg - 1 - (js * n_pg + i)]
            copies.append(pltpu.make_async_copy(ck_hbm.at[src], kbuf.at[slot, i], sems.at[slot, 0, i]))
            copies.append(pltpu.make_async_copy(cv_hbm.at[src], vbuf.at[slot, i], sems.at[slot, 1, i]))
        return copies

    @pl.when(step == 0)
    def _():
        for t in range(n_slots - 1):
            for copy in page_copies(t):
                copy.start()

    @pl.when(step + n_slots - 1 < pl.num_programs(0) * n_steps)
    def _():
        for copy in page_copies(step + n_slots - 1):
            copy.start()

    zscale = HEAD_DIM ** -0.5 * LOG2E
    bias2 = bias_ref[...] * LOG2E
    head = lambda h: slice(h * HEAD_DIM, (h + 1) * HEAD_DIM)

    def scores(k_heads):
        z = sum(_dot_nt(qr_ref[0, h], k_heads[h].astype(BF16)) for h in range(n_heads))
        return z * zscale + bias2[:, :z.shape[1]]

    @pl.when(j == 0)
    def _():
        r = kn_ref.shape[1]
        z = scores([kn_ref[0, :, head(h)] for h in range(n_heads)])
        t_row = lax.broadcasted_iota(jnp.int32, (n_rows, r), 0) % n_q
        s_col = lax.broadcasted_iota(jnp.int32, (n_rows, r), 1)
        valid = (s_col < t_row) & (s_col < n_valid)
        sp = jnp.where(valid, _softplus2(z), 0.0)
        later_mat = jnp.where(lax.broadcasted_iota(jnp.int32, (r, r), 0)
                              > lax.broadcasted_iota(jnp.int32, (r, r), 1), 1.0, 0.0)
        later = _dot(sp, later_mat, precision=lax.Precision.HIGHEST)
        wb = jnp.where(valid, jnp.exp2(z - sp - later), 0.0).astype(BF16)
        carry_scr[...] = jnp.broadcast_to(jnp.sum(sp, axis=1, keepdims=True), carry_scr.shape)
        for h in range(n_heads):
            acc_scr[h] = _dot(wb, vn_ref[0, :, head(h)].astype(BF16))

    h_step = step % h_steps

    @pl.when(h_step == 0)
    def _():
        st_scr[...] = jnp.zeros_like(st_scr)

    for copy in page_copies(step):
        copy.wait()
    slot = step % n_slots
    kp_refs = [kbuf.at[slot, i] for i in range(n_pg)]
    vp_refs = [vbuf.at[slot, i] for i in range(n_pg)]
    def sweep_pages():
        k_pages = [[kp_ref[pl.ds(h, page, stride=n_heads), :] for h in range(n_heads)] for kp_ref in kp_refs]
        yield
        zs = [scores(k_heads) for k_heads in k_pages]
        sps = [_softplus2(z) for z in zs]
        pieces = [jnp.concatenate(_split_bf16(sp), axis=1) for sp in sps]
        yield
        rs = [_dot(p, mu_ref[...]) for p in pieces]
        carry = carry_scr[...]
        wbs = []
        for z, sp, r in zip(zs, sps, rs):
            wbs.append(jnp.exp2(z - sp - r[:, :page] - carry).astype(BF16))
            carry = carry + r[:, page:]
        carry_scr[...] = carry
        yield
        for h in range(n_heads):
            acc_scr[h] += sum(_dot(wb, vp_ref[pl.ds(h, page, stride=n_heads), :].astype(BF16))
                              for wb, vp_ref in zip(wbs, vp_refs))

    _, (hb, st) = _in_lockstep(
        [sweep_pages(),
         _hgrn_chunks(hq_ref[0], hf_ref[0], hi_ref[0], _lower_bound(lbl_ref[...]), st_scr[...], HGRN_CHUNK)])
    hb_ref[...] = _head_norm_gate(hb, hgain_ref[...], hz_ref[0]).astype(hb_ref.dtype)
    st_scr[...] = st

    @pl.when(h_step == h_steps - 1)
    def _():
        hs_ref[0, 0] = st_scr[...].T

    @pl.when(j == pl.num_programs(1) - 1)
    def _():
        rh = lax.broadcasted_iota(jnp.int32, (n_rows, HEAD_DIM), 0) // n_q
        o = jnp.zeros((n_rows, HEAD_DIM), F32)
        for h in range(n_heads):
            o = o + jnp.where(rh == h, acc_scr[h], 0.0)
        o_ref[0] = _head_norm_gate(o, gain_ref[...], z_ref[0]).astype(o_ref.dtype)


def _sb_decode_hgrn_prompt(page_table, q_rows, bias_rows, p3s, cache_k, cache_v, z_r, gain_r, n_valid,
                           p3, lb_logits, gain_b, batch, seq):
    dec_batch, n_pages = page_table.shape
    page_rows = cache_k.shape[1]
    n_heads, n_rows = q_rows.shape[1:3]
    width = n_heads * HEAD_DIM
    page = page_rows // n_heads
    assert page == HEAD_DIM
    mu = _sb_matrix(page, 2)
    new = lambda c: pl.BlockSpec((1, SAMPLE_PAD, width), lambda b, j, pt: (c, b, 0))
    n_pg = DECODE_PAGES_PER_STEP
    assert n_pages % n_pg == 0 and dec_batch * (n_pages // n_pg) >= DECODE_SLOTS - 1
    page_buf = pltpu.VMEM((DECODE_SLOTS, n_pg, page_rows, HEAD_DIM), cache_k.dtype)
    n_steps = n_pages // n_pg
    n_heads_b = p3.shape[2] // HEAD_DIM
    r_len = batch * n_heads_b * seq // (dec_batch * n_steps)
    h_steps = seq // r_len
    assert r_len * dec_batch * n_steps == batch * n_heads_b * seq and seq % r_len == 0 and r_len % HGRN_CHUNK == 0
    stream = lambda b, j: (b * n_steps + j) // h_steps
    h_rows = lambda b, j: (stream(b, j) // n_heads_b) * h_steps + (b * n_steps + j) % h_steps
    h_head = lambda b, j: stream(b, j) % n_heads_b
    h_comp = lambda c: pl.BlockSpec((1, r_len, HEAD_DIM), lambda b, j, pt: (c, h_rows(b, j), h_head(b, j)))
    grid_spec = pltpu.PrefetchScalarGridSpec(
        num_scalar_prefetch=1,
        grid=(dec_batch, n_pages // n_pg),
        in_specs=[
            pl.BlockSpec((1, n_heads, n_rows, HEAD_DIM), lambda b, j, pt: (b, 0, 0, 0)),
            pl.BlockSpec((n_rows, HEAD_DIM), lambda b, j, pt: (0, 0)),
            new(COMP_KA), new(COMP_VA),
            pl.BlockSpec(memory_space=pl.ANY), pl.BlockSpec(memory_space=pl.ANY),
            pl.BlockSpec(mu.shape, lambda b, j, pt: (0, 0)),
            pl.BlockSpec((1, n_rows, HEAD_DIM), lambda b, j, pt: (b, 0, 0)),
            pl.BlockSpec((n_rows, HEAD_DIM), lambda b, j, pt: (0, 0)),
            pl.BlockSpec((lb_logits.shape[0], HEAD_DIM), lambda b, j, pt: (0, h_head(b, j))),
            h_comp(COMP_QB), h_comp(COMP_FB), h_comp(COMP_IB), h_comp(COMP_ZB),
            pl.BlockSpec((1, HEAD_DIM), lambda b, j, pt: (0, h_head(b, j))),
        ],
        out_specs=[
            pl.BlockSpec((1, n_rows, HEAD_DIM), lambda b, j, pt: (b, 0, 0)),
            pl.BlockSpec((r_len, HEAD_DIM), lambda b, j, pt: (h_rows(b, j), h_head(b, j))),
            pl.BlockSpec((1, 1, HEAD_DIM, HEAD_DIM),
                         lambda b, j, pt: (stream(b, j) // n_heads_b, h_head(b, j), 0, 0)),
        ],
        scratch_shapes=[pltpu.VMEM((n_heads, n_rows, HEAD_DIM), F32), pltpu.VMEM((n_rows, HEAD_DIM), F32),
                        page_buf, page_buf, pltpu.SemaphoreType.DMA((DECODE_SLOTS, 2, n_pg)),
                        pltpu.VMEM((HEAD_DIM, HEAD_DIM), F32)],
    )
    return pl.pallas_call(
        functools.partial(_sb_decode_kernel, n_valid=n_valid, n_pg=n_pg, h_steps=h_steps),
        grid_spec=grid_spec,
        out_shape=[
            jax.ShapeDtypeStruct((dec_batch, n_rows, HEAD_DIM), BF16),
            jax.ShapeDtypeStruct((batch * seq, n_heads_b * HEAD_DIM), BF16),
            jax.ShapeDtypeStruct((batch, n_heads_b, HEAD_DIM, HEAD_DIM), F32),
        ],
        compiler_params=pltpu.CompilerParams(
            dimension_semantics=("arbitrary", "arbitrary"), vmem_limit_bytes=VMEM_LIMIT),
        name="sb_decode_hgrn_prompt",
    )(page_table, q_rows, bias_rows, p3s, p3s, cache_k, cache_v, mu, z_r, gain_r,
      lb_logits, p3, p3, p3, p3, gain_b)


def _out_proj_kernel(x_ref, ma_ref, mb_ref, xs_ref, mas_ref, mbs_ref, wa_ref, wb_ref, fw_ref, o_ref, os_ref):
    wa = wa_ref[...].astype(BF16)
    wb = wb_ref[...].astype(BF16)

    def project(x, m_a, m_b):
        y = x + _dot(m_a, wa) + _dot(m_b, wb)
        ms = jnp.mean(y * y, axis=-1, keepdims=True)
        return y * lax.rsqrt(ms + EPS) * fw_ref[...]

    o_ref[...] = project(x_ref[...], ma_ref[...], mb_ref[...])

    @pl.when(pl.program_id(0) == 0)
    def _():
        os_ref[...] = project(xs_ref[...], mas_ref[...], mbs_ref[...])


def _out_proj(x, m_a, m_b, xs, m_a_s, m_b_s, w, final_w, bm):
    t, d = x.shape
    ts = xs.shape[0]
    wa = m_a.shape[1]
    wb = m_b.shape[1]
    assert wa == wb
    whole = lambda rows, cols: pl.BlockSpec((rows, cols), lambda m: (0, 0))
    return pl.pallas_call(
        _out_proj_kernel,
        grid=(t // bm,),
        in_specs=[
            pl.BlockSpec((bm, d), lambda m: (m, 0)),
            pl.BlockSpec((bm, wa), lambda m: (m, 0)),
            pl.BlockSpec((bm, wb), lambda m: (m, 0)),
            whole(ts, d), whole(ts, wa), whole(ts, wb),
            pl.BlockSpec((wa, d), lambda m: (0, 0)),
            pl.BlockSpec((wb, d), lambda m: (1, 0)),
            whole(1, d),
        ],
        out_specs=[pl.BlockSpec((bm, d), lambda m: (m, 0)), whole(ts, d)],
        out_shape=[jax.ShapeDtypeStruct((t, d), F32), jax.ShapeDtypeStruct((ts, d), F32)],
        compiler_params=pltpu.CompilerParams(
            dimension_semantics=("arbitrary",), vmem_limit_bytes=VMEM_LIMIT),
        name="out_proj",
    )(x, m_a, m_b, xs, m_a_s, m_b_s, w, w, final_w)


def kernel(x_prompt, x_sample, cache_k, cache_v, state_s, page_table, norm_w, w_in, gain_a, gain_b,
           sb_bias, lb_logits, w_out, final_norm_w):
    depth = norm_w.shape[0]
    assert depth == 1
    batch, seq, d_model = x_prompt.shape
    dec_batch, dec_seq, _ = x_sample.shape
    n_heads_a, head_dim = cache_k.shape[3:]
    w_a = n_heads_a * head_dim
    assert head_dim == HEAD_DIM and state_s.shape[3:] == (HEAD_DIM, HEAD_DIM)
    assert dec_seq <= SAMPLE_PAD and seq % SB_TILE == 0
    assert (batch * seq) % IN_PROJ_ROWS == 0 and (batch * seq) % OUT_PROJ_ROWS == 0

    nw = norm_w[0][None, :]
    fw = final_norm_w[None, :]
    ga = gain_a[0][None, :]
    gb = gain_b[0][None, :]

    xp = x_prompt.reshape(batch * seq, d_model)
    xs = jnp.pad(x_sample, ((0, 0), (0, SAMPLE_PAD - dec_seq), (0, 0))).reshape(dec_batch * SAMPLE_PAD, d_model)
    p3, p3s = _in_proj(xp, xs, nw, w_in[0], bm=IN_PROJ_ROWS)
    m_a, k_p, v_p = _sb_prompt(p3, sb_bias, ga, batch, seq)
    k_p = k_p.reshape(1, batch, seq, n_heads_a, head_dim)
    v_p = v_p.reshape(1, batch, seq, n_heads_a, head_dim)

    rows = lambda c: p3s[c].reshape(dec_batch, SAMPLE_PAD, -1)[:, :dec_seq]
    q_t = rows(COMP_QA).reshape(dec_batch, dec_seq, n_heads_a, head_dim).transpose(0, 2, 1, 3)
    q_rows = (q_t[:, :, None, :, :] * jnp.eye(n_heads_a, dtype=F32)[None, :, :, None, None]).reshape(
        dec_batch, n_heads_a, n_heads_a * dec_seq, head_dim).astype(BF16)
    bias_rows = jnp.broadcast_to(jnp.repeat(sb_bias[0], dec_seq)[:, None], (n_heads_a * dec_seq, head_dim))
    to_rows = lambda a: a.reshape(dec_batch, dec_seq, n_heads_a, head_dim).transpose(0, 2, 1, 3).reshape(
        dec_batch, n_heads_a * dec_seq, head_dim)
    z_r = to_rows(rows(COMP_ZA))
    gain_r = jnp.repeat(gain_a[0].reshape(n_heads_a, head_dim), dec_seq, axis=0)
    ck = cache_k[0].reshape(cache_k.shape[1], cache_k.shape[2] * n_heads_a, head_dim)
    cv = cache_v[0].reshape(cache_v.shape[1], cache_v.shape[2] * n_heads_a, head_dim)
    o_r, m_b, s_p = _sb_decode_hgrn_prompt(page_table, q_rows, bias_rows, p3s, ck, cv, z_r, gain_r, dec_seq,
                                           p3, lb_logits, gb, batch, seq)
    m_a_s = o_r.reshape(dec_batch, n_heads_a, dec_seq, head_dim).transpose(0, 2, 1, 3).reshape(
        dec_batch, dec_seq, w_a)
    m_a_s = jnp.pad(m_a_s, ((0, 0), (0, SAMPLE_PAD - dec_seq), (0, 0))).reshape(dec_batch * SAMPLE_PAD, w_a)
    m_b_s, s_s = _hgrn_decode(p3s, lb_logits, gb, state_s[0], dec_seq)
    y_p, y_s = _out_proj(xp, m_a, m_b, xs, m_a_s, m_b_s, w_out[0], fw, bm=OUT_PROJ_ROWS)
    y_prompt = y_p.reshape(batch, seq, d_model)
    y_sample = y_s.reshape(dec_batch, SAMPLE_PAD, d_model)[:, :dec_seq]
    k_s = rows(COMP_KA).reshape(1, dec_batch, dec_seq, n_heads_a, head_dim)
    v_s = rows(COMP_VA).reshape(1, dec_batch, dec_seq, n_heads_a, head_dim)

    return (y_prompt, y_sample, k_p, v_p, s_p[None], k_s, v_s, s_s[None])
```

```python
import functools

import jax
import jax.numpy as jnp
from jax import lax
from jax.experimental import pallas as pl
from jax.experimental.pallas import tpu as pltpu

EPS = 1e-6
HEAD_DIM = 128
N_COMP = 8
COMP_QA, COMP_KA, COMP_VA, COMP_ZA, COMP_QB, COMP_FB, COMP_IB, COMP_ZB = range(N_COMP)
IN_PROJ_ROWS = 1024
OUT_PROJ_ROWS = 512
SB_BLOCK = 128
SB_TILE = 512
LOG2E = 1.4426950408889634
HGRN_CHUNK = 64
CUMSUM_GROUP = 256
HGRN_SUB = 16
SAMPLE_PAD = 8
DECODE_PAGES_PER_STEP = 16
DECODE_SLOTS = 3
PAGE_DMA_PRIORITY = 1
VMEM_LIMIT = 60 * 1024 * 1024

F32 = jnp.float32
BF16 = jnp.bfloat16


def _dot(a, b, **kw):
    return jnp.dot(a, b, preferred_element_type=F32, **kw)


def _dot_nt(a, b):
    return lax.dot_general(a, b, (((1,), (1,)), ((), ())), preferred_element_type=F32)


def _dot_tn(a, b):
    return lax.dot_general(a, b, (((0,), (0,)), ((), ())), preferred_element_type=F32)


def _sigmoid(x):
    return 1.0 / (1.0 + jnp.exp(-x))


def _softplus2(z2):
    return jnp.maximum(z2, 0.0) + jnp.log2(1.0 + jnp.exp2(-jnp.abs(z2)))


def _split_bf16(x):
    hi = x.astype(BF16)
    lo = (x - hi.astype(F32)).astype(BF16)
    return hi, lo


def _in_proj_kernel(x_ref, xs_ref, nw_ref, w_ref, o_ref, os_ref, h_scr):
    m, n = pl.program_id(0), pl.program_id(1)

    def normed(x):
        ms = jnp.mean(x * x, axis=-1, keepdims=True)
        return (x * lax.rsqrt(ms + EPS) * nw_ref[...]).astype(BF16)

    @pl.when(n == 0)
    def _():
        h_scr[...] = normed(x_ref[...])

    @pl.when(m == 0)
    def _():
        os_ref[0] = _dot(normed(xs_ref[...]), w_ref[...].astype(BF16))

    o_ref[0] = _dot(h_scr[...], w_ref[...].astype(BF16))


def _in_proj(x, xs, norm_w, w, bm):
    t, d = x.shape
    ts = xs.shape[0]
    d_in = w.shape[1]
    bn = d_in // N_COMP
    return pl.pallas_call(
        _in_proj_kernel,
        grid=(t // bm, N_COMP),
        in_specs=[
            pl.BlockSpec((bm, d), lambda m, n: (m, 0)),
            pl.BlockSpec((ts, d), lambda m, n: (0, 0)),
            pl.BlockSpec((1, d), lambda m, n: (0, 0)),
            pl.BlockSpec((d, bn), lambda m, n: (0, n)),
        ],
        out_specs=[
            pl.BlockSpec((1, bm, bn), lambda m, n: (n, m, 0)),
            pl.BlockSpec((1, ts, bn), lambda m, n: (jnp.where(m == 0, n, N_COMP - 1), 0, 0)),
        ],
        out_shape=[
            jax.ShapeDtypeStruct((N_COMP, t, bn), F32),
            jax.ShapeDtypeStruct((N_COMP, ts, bn), F32),
        ],
        scratch_shapes=[pltpu.VMEM((bm, d), BF16)],
        compiler_params=pltpu.CompilerParams(
            dimension_semantics=("arbitrary", "arbitrary"), vmem_limit_bytes=VMEM_LIMIT),
        name="in_proj",
    )(x, xs, norm_w, w)


def _head_norm_gate(o, gain, z):
    ms = jnp.mean(o * o, axis=-1, keepdims=True)
    return o * lax.rsqrt(ms + EPS) * gain * (z * _sigmoid(z))


def _sb_prompt_kernel(bias_ref, q_ref, k_ref, v_ref, z_ref, gain_ref, mu_ref, o_ref, ko_ref, vo_ref,
                      kb_scr, vb_scr, acc_scr, carry_scr):
    tq = SB_TILE
    bk = SB_BLOCK
    t_len = q_ref.shape[1]
    bias2 = bias_ref[0, pl.program_id(1)] * LOG2E
    qscale = HEAD_DIM ** -0.5 * LOG2E
    kb_scr[...] = k_ref[0].astype(BF16)
    vb_scr[...] = v_ref[0].astype(BF16)
    ko_ref[...] = k_ref[0]
    vo_ref[...] = v_ref[0]
    gain = gain_ref[...]

    def sweep(z, carry, mask):
        sp = _softplus2(z)
        if mask is not None:
            sp = jnp.where(mask, sp, 0.0)
        zs = z - sp
        hi, lo = _split_bf16(sp)
        ws = [None] * (z.shape[1] // bk)
        for c in reversed(range(len(ws))):
            cols = slice(c * bk, (c + 1) * bk)
            r = _dot(jnp.concatenate([hi[:, cols], lo[:, cols]], axis=1), mu_ref[...])
            ws[c] = jnp.exp2(zs[:, cols] - r[:, :bk] - carry)
            carry = carry + r[:, bk:]
        w = jnp.concatenate(ws, axis=1)
        if mask is not None:
            w = jnp.where(mask, w, 0.0)
        return w.astype(BF16), carry

    def tile(q, k0, tk):
        keys = pl.ds(pl.multiple_of(k0, tq), tk)
        w, carry = sweep(_dot_nt(q, kb_scr[keys, :]) + bias2, carry_scr[...], None)
        carry_scr[...] = carry
        acc_scr[...] += _dot(w, vb_scr[keys, :])

    def own_tile(q, qs, with_prev):
        half = tq // 2
        carry = carry_scr[...]
        new_keys = pl.ds(pl.multiple_of(qs + half, half), half)
        row = lax.broadcasted_iota(jnp.int32, (half, half), 0)
        col = lax.broadcasted_iota(jnp.int32, (half, half), 1)
        w_new, carry_late = sweep(_dot_nt(q[half:], kb_scr[new_keys, :]) + bias2, carry[half:], col < row)
        acc_late = _dot(w_new, vb_scr[new_keys, :])
        carry = jnp.concatenate([carry[:half], carry_late], axis=0)

        n_old = tq + half if with_prev else half
        old_keys = pl.ds(pl.multiple_of(qs - (tq if with_prev else 0), half), n_old)
        row = lax.broadcasted_iota(jnp.int32, (tq, n_old), 0)
        col = lax.broadcasted_iota(jnp.int32, (tq, n_old), 1) - (n_old - half)
        w_old, carry = sweep(_dot_nt(q, kb_scr[old_keys, :]) + bias2, carry, (col < row) | (row >= half))
        carry_scr[...] = carry
        acc_scr[...] += (_dot(w_old, vb_scr[old_keys, :])
                         + jnp.concatenate([jnp.zeros((half, HEAD_DIM), F32), acc_late], axis=0))

    def q_tile(qt, _):
        qs = pl.multiple_of(qt * tq, tq)
        q = (q_ref[0, pl.ds(qs, tq), :] * qscale).astype(BF16)
        acc_scr[...] = jnp.zeros_like(acc_scr)
        carry_scr[...] = jnp.zeros_like(carry_scr)
        @pl.when(qt == 0)
        def _():
            own_tile(q, qs, False)

        @pl.when(qt > 0)
        def _():
            own_tile(q, qs, True)

        def k_pair(jj, _):
            tile(q, qs - tq - (jj + 1) * 2 * tq, 2 * tq)
            return 0

        lax.fori_loop(0, jnp.maximum(qt - 1, 0) // 2, k_pair, 0)

        @pl.when((qt > 0) & (qt % 2 == 0))
        def _():
            tile(q, 0, tq)

        zg = z_ref[0, pl.ds(qs, tq), :]
        o_ref[pl.ds(qs, tq), :] = _head_norm_gate(acc_scr[...], gain, zg).astype(o_ref.dtype)
        return 0

    lax.fori_loop(0, t_len // tq, q_tile, 0)


def _sb_matrix(n, pieces):
    j = jnp.arange(pieces * n)[:, None] % n
    s = jnp.arange(2 * n)[None, :]
    return jnp.where(s < n, (j > s), True).astype(BF16)


def _sb_prompt(p3, sb_bias, gain_a, batch, seq):
    n_heads = p3.shape[2] // HEAD_DIM
    mu = _sb_matrix(SB_BLOCK, 2)
    comp = lambda c: pl.BlockSpec((1, seq, HEAD_DIM), lambda b, h: (c, b, h))
    return pl.pallas_call(
        _sb_prompt_kernel,
        grid=(batch, n_heads),
        in_specs=[
            pl.BlockSpec(memory_space=pltpu.SMEM),
            comp(COMP_QA), comp(COMP_KA), comp(COMP_VA), comp(COMP_ZA),
            pl.BlockSpec((1, HEAD_DIM), lambda b, h: (0, h)),
            pl.BlockSpec(mu.shape, lambda b, h: (0, 0)),
        ],
        out_specs=[pl.BlockSpec((seq, HEAD_DIM), lambda b, h: (b, h))] * 3,
        out_shape=[
            jax.ShapeDtypeStruct((batch * seq, n_heads * HEAD_DIM), BF16),
            jax.ShapeDtypeStruct((batch * seq, n_heads * HEAD_DIM), p3.dtype),
            jax.ShapeDtypeStruct((batch * seq, n_heads * HEAD_DIM), p3.dtype),
        ],
        scratch_shapes=[pltpu.VMEM((seq, HEAD_DIM), BF16), pltpu.VMEM((seq, HEAD_DIM), BF16),
                        pltpu.VMEM((SB_TILE, HEAD_DIM), F32), pltpu.VMEM((SB_TILE, SB_BLOCK), F32)],
        compiler_params=pltpu.CompilerParams(
            dimension_semantics=("arbitrary", "arbitrary"), vmem_limit_bytes=VMEM_LIMIT),
        name="sb_prompt",
    )(sb_bias, p3, p3, p3, p3, gain_a, mu)


def _lower_bound(l):
    e = jnp.exp(l - jnp.max(l, axis=0, keepdims=True))
    return e[0:1, :] / jnp.sum(e, axis=0, keepdims=True)


def _chunk_cumsum(x, c_len):
    rows = x.shape[0]
    group = min(rows, CUMSUM_GROUP)
    ti = lax.broadcasted_iota(jnp.int32, (group, group), 0)
    si = lax.broadcasted_iota(jnp.int32, (group, group), 1)
    tri = jnp.where((si <= ti) & (si // c_len == ti // c_len), 1.0, 0.0).astype(BF16)
    hi = x.astype(BF16)
    rest = x - hi.astype(F32)
    mid = rest.astype(BF16)
    lo = (rest - mid.astype(F32)).astype(BF16)
    pieces = jnp.concatenate([hi, mid, lo], axis=1)
    outs = []
    for g0 in range(0, rows, group):
        r = _dot(tri, pieces[g0:g0 + group])
        outs.append((r[:, :HEAD_DIM] + r[:, HEAD_DIM:2 * HEAD_DIM]) + r[:, 2 * HEAD_DIM:])
    return jnp.concatenate(outs, axis=0)


def _hgrn_chunks(qb, fb, ib, lb, st, c_len, n_valid=None):
    rows = qb.shape[0]
    sub = min(HGRN_SUB, c_len)
    q = qb * _sigmoid(qb)
    g = lb + (1.0 - lb) * _sigmoid(fb)
    kk = 1.0 - g
    lg = jnp.log(g)
    if n_valid is not None:
        live = lax.broadcasted_iota(jnp.int32, (rows, HEAD_DIM), 0) < n_valid
        kk = jnp.where(live, kk, 0.0)
        lg = jnp.where(live, lg, 0.0)
    yield
    b = _chunk_cumsum(lg, c_len)
    vb = ib.astype(BF16)
    chunks = [slice(c * c_len, (c + 1) * c_len) for c in range(rows // c_len)]
    spans, operands = [], []
    for ch in chunks:
        for lo_r in range(ch.start, ch.stop, sub):
            hi_r = lo_r + sub
            m = b[lo_r + sub // 2:lo_r + sub // 2 + 1, :]
            qi = (q[lo_r:hi_r] * jnp.exp(b[lo_r:hi_r] - m)).astype(BF16)
            ki = (kk[ch.start:hi_r] * jnp.exp(m - b[ch.start:hi_r])).astype(BF16)
            spans.append((ch.start, lo_r, hi_r))
            operands.append((qi, ki))
    yield
    atts = [_dot_nt(qi, ki) for qi, ki in operands]
    masked = []
    for att, (c0, lo_r, hi_r) in zip(atts, spans):
        t_pos = lax.broadcasted_iota(jnp.int32, att.shape, 0) + (lo_r - c0)
        s_pos = lax.broadcasted_iota(jnp.int32, att.shape, 1)
        masked.append(jnp.where(s_pos <= t_pos, att, 0.0).astype(BF16))
    yield
    o_intra = jnp.concatenate([_dot(att, vb[c0:hi_r]) for att, (c0, _, hi_r) in zip(masked, spans)], axis=0)
    lasts = [b[ch.stop - 1:ch.stop, :] for ch in chunks]
    decayed = [(kk[ch] * jnp.exp(bl - b[ch])).astype(BF16) for ch, bl in zip(chunks, lasts)]
    yield
    incs = [_dot_tn(vb[ch], kd) for ch, kd in zip(chunks, decayed)]
    states = [st]
    for inc, bl in zip(incs, lasts):
        states.append(states[-1] * jnp.exp(bl) + inc)
    starts = [((q[ch] * jnp.exp(b[ch])).astype(BF16), s.astype(BF16)) for ch, s in zip(chunks, states)]
    yield
    o_inter = [_dot_nt(qe, s) for qe, s in starts]
    return o_intra + jnp.concatenate(o_inter, axis=0), states[-1]


def _in_lockstep(generators):
    values = [None] * len(generators)
    running = dict(enumerate(generators))
    while running:
        for i, gen in list(running.items()):
            try:
                next(gen)
            except StopIteration as done:
                values[i] = done.value
                del running[i]
    return values


def _hgrn_decode_kernel(lbl_ref, q_ref, f_ref, i_ref, z_ref, gain_ref, s0_ref, o_ref, s_ref, *, n_valid):
    n_heads = s0_ref.shape[1]
    head = lambda h: slice(h * HEAD_DIM, (h + 1) * HEAD_DIM)
    lb = _lower_bound(lbl_ref[...])
    results = _in_lockstep([
        _hgrn_chunks(q_ref[0, :, head(h)], f_ref[0, :, head(h)], i_ref[0, :, head(h)], lb[:, head(h)],
                     s0_ref[0, h].T, SAMPLE_PAD, n_valid=n_valid)
        for h in range(n_heads)])
    for h, (o, st) in enumerate(results):
        o_ref[:, head(h)] = _head_norm_gate(o, gain_ref[:, head(h)], z_ref[0, :, head(h)]).astype(o_ref.dtype)
        s_ref[0, h] = st.T


def _sb_decode_kernel(pt_ref, qr_ref, bias_ref, kn_ref, vn_ref, ck_hbm, cv_hbm, mu_ref, z_ref, gain_ref,
                      lbl_ref, hq_ref, hf_ref, hi_ref, hz_ref, hgain_ref,
                      lbls_ref, sq_ref, sf_ref, si_ref, sz_ref, sgain_ref, s0_ref,
                      o_ref, hb_ref, hs_ref, so_ref, ss_ref,
                      acc_scr, carry_scr, kbuf, vbuf, sems, st_scr, *, n_valid, n_pg, h_steps):
    j = pl.program_id(1)
    n_steps = pl.num_programs(1)
    step = pl.program_id(0) * n_steps + j
    n_slots = kbuf.shape[0]
    n_heads, n_rows = qr_ref.shape[1:3]
    n_q = n_rows // n_heads
    page = kbuf.shape[2] // n_heads

    def page_copies(t):
        seq, js, slot = t // n_steps, t % n_steps, t % n_slots
        copies = []
        for i in range(n_pg):
            src = pt_ref[seq, n_steps * n_pg - 1 - (js * n_pg + i)]
            copies.append(pltpu.make_async_copy(ck_hbm.at[src], kbuf.at[slot, i], sems.at[slot, 0, i]))
            copies.append(pltpu.make_async_copy(cv_hbm.at[src], vbuf.at[slot, i], sems.at[slot, 1, i]))
        return copies

    @pl.when(step == 0)
    def _():
        for t in range(n_slots - 1):
            for copy in page_copies(t):
                copy.start(priority=PAGE_DMA_PRIORITY)

    @pl.when(step + n_slots - 1 < pl.num_programs(0) * n_steps)
    def _():
        for copy in page_copies(step + n_slots - 1):
            copy.start(priority=PAGE_DMA_PRIORITY)

    zscale = HEAD_DIM ** -0.5 * LOG2E
    bias2 = bias_ref[...] * LOG2E
    head = lambda h: slice(h * HEAD_DIM, (h + 1) * HEAD_DIM)

    pairs = [(h, h + 1) for h in range(0, n_heads, 2)]
    q_pairs = [jnp.concatenate([qr_ref[0, a], qr_ref[0, b]], axis=1) for a, b in pairs]

    def scores(k_heads):
        z = sum(_dot_nt(qp, jnp.concatenate([k_heads[a].astype(BF16), k_heads[b].astype(BF16)], axis=1))
                for qp, (a, b) in zip(q_pairs, pairs))
        return z * zscale + bias2[:, :z.shape[1]]

    def weighted_values(w, v_heads):
        row_head = lax.broadcasted_iota(jnp.int32, w.shape, 0) // n_q
        own = lambda h: jnp.where(row_head == h, w, 0.0).astype(BF16)
        if w.shape[1] % HEAD_DIM:
            return sum(_dot(own(h), v_heads[h].astype(BF16)) for h in range(n_heads))
        return sum(_dot(jnp.concatenate([own(a), own(b)], axis=1),
                        jnp.concatenate([v_heads[a].astype(BF16), v_heads[b].astype(BF16)], axis=0))
                   for a, b in pairs)

    @pl.when(j == 0)
    def _():
        r = kn_ref.shape[1]
        z = scores([kn_ref[0, :, head(h)] for h in range(n_heads)])
        t_row = lax.broadcasted_iota(jnp.int32, (n_rows, r), 0) % n_q
        s_col = lax.broadcasted_iota(jnp.int32, (n_rows, r), 1)
        valid = (s_col < t_row) & (s_col < n_valid)
        sp = jnp.where(valid, _softplus2(z), 0.0)
        later_mat = jnp.where(lax.broadcasted_iota(jnp.int32, (r, r), 0)
                              > lax.broadcasted_iota(jnp.int32, (r, r), 1), 1.0, 0.0)
        later = _dot(sp, later_mat, precision=lax.Precision.HIGHEST)
        w = jnp.where(valid, jnp.exp2(z - sp - later), 0.0)
        carry_scr[...] = jnp.broadcast_to(jnp.sum(sp, axis=1, keepdims=True), carry_scr.shape)
        acc_scr[...] = weighted_values(w, [vn_ref[0, :, head(h)] for h in range(n_heads)])
        _hgrn_decode_kernel(lbls_ref, sq_ref, sf_ref, si_ref, sz_ref, sgain_ref, s0_ref, so_ref, ss_ref,
                            n_valid=n_valid)

    h_step = step % h_steps

    @pl.when(h_step == 0)
    def _():
        st_scr[...] = jnp.zeros_like(st_scr)

    for copy in page_copies(step):
        copy.wait()
    slot = step % n_slots
    kp_refs = [kbuf.at[slot, i] for i in range(n_pg)]
    vp_refs = [vbuf.at[slot, i] for i in range(n_pg)]
    def sweep_pages():
        k_pages = [[kp_ref[pl.ds(h, page, stride=n_heads), :] for h in range(n_heads)] for kp_ref in kp_refs]
        yield
        zs = [scores(k_heads) for k_heads in k_pages]
        sps = [_softplus2(z) for z in zs]
        pieces = [jnp.concatenate(_split_bf16(sp), axis=1) for sp in sps]
        yield
        rs = [_dot(p, mu_ref[...]) for p in pieces]
        carry = carry_scr[...]
        ws = []
        for z, sp, r in zip(zs, sps, rs):
            ws.append(jnp.exp2(z - sp - r[:, :page] - carry))
            carry = carry + r[:, page:]
        carry_scr[...] = carry
        v_pages = [[vp_ref[pl.ds(h, page, stride=n_heads), :] for h in range(n_heads)] for vp_ref in vp_refs]
        yield
        acc_scr[...] += sum(weighted_values(w, v_heads) for w, v_heads in zip(ws, v_pages))

    _, (hb, st) = _in_lockstep(
        [sweep_pages(),
         _hgrn_chunks(hq_ref[0], hf_ref[0], hi_ref[0], _lower_bound(lbl_ref[...]), st_scr[...], HGRN_CHUNK)])
    hb_ref[...] = _head_norm_gate(hb, hgain_ref[...], hz_ref[0]).astype(hb_ref.dtype)
    st_scr[...] = st

    @pl.when(h_step == h_steps - 1)
    def _():
        hs_ref[0, 0] = st_scr[...].T

    @pl.when(j == pl.num_programs(1) - 1)
    def _():
        o_ref[0] = _head_norm_gate(acc_scr[...], gain_ref[...], z_ref[0]).astype(o_ref.dtype)


def _sb_decode_hgrn_prompt(page_table, q_rows, bias_rows, p3s, cache_k, cache_v, z_r, gain_r, n_valid,
                           p3, lb_logits, gain_b, batch, seq, state):
    dec_batch, n_pages = page_table.shape
    page_rows = cache_k.shape[1]
    n_heads, n_rows = q_rows.shape[1:3]
    width = n_heads * HEAD_DIM
    page = page_rows // n_heads
    assert page == HEAD_DIM
    mu = _sb_matrix(page, 2)
    new = lambda c: pl.BlockSpec((1, SAMPLE_PAD, width), lambda b, j, pt: (c, b, 0))
    n_pg = DECODE_PAGES_PER_STEP
    assert n_pages % n_pg == 0 and dec_batch * (n_pages // n_pg) >= DECODE_SLOTS - 1
    page_buf = pltpu.VMEM((DECODE_SLOTS, n_pg, page_rows, HEAD_DIM), cache_k.dtype)
    n_steps = n_pages // n_pg
    n_heads_b = p3.shape[2] // HEAD_DIM
    r_len = batch * n_heads_b * seq // (dec_batch * n_steps)
    h_steps = seq // r_len
    assert r_len * dec_batch * n_steps == batch * n_heads_b * seq and seq % r_len == 0 and r_len % HGRN_CHUNK == 0
    stream = lambda b, j: (b * n_steps + j) // h_steps
    h_rows = lambda b, j: (stream(b, j) // n_heads_b) * h_steps + (b * n_steps + j) % h_steps
    h_head = lambda b, j: stream(b, j) % n_heads_b
    h_comp = lambda c: pl.BlockSpec((1, r_len, HEAD_DIM), lambda b, j, pt: (c, h_rows(b, j), h_head(b, j)))
    width_b = n_heads_b * HEAD_DIM
    state_spec = pl.BlockSpec((1, n_heads_b, HEAD_DIM, HEAD_DIM), lambda b, j, pt: (b, 0, 0, 0))
    s_comp = lambda c: pl.BlockSpec((1, SAMPLE_PAD, width_b), lambda b, j, pt: (c, b, 0))
    grid_spec = pltpu.PrefetchScalarGridSpec(
        num_scalar_prefetch=1,
        grid=(dec_batch, n_pages // n_pg),
        in_specs=[
            pl.BlockSpec((1, n_heads, n_rows, HEAD_DIM), lambda b, j, pt: (b, 0, 0, 0)),
            pl.BlockSpec((n_rows, HEAD_DIM), lambda b, j, pt: (0, 0)),
            new(COMP_KA), new(COMP_VA),
            pl.BlockSpec(memory_space=pl.ANY), pl.BlockSpec(memory_space=pl.ANY),
            pl.BlockSpec(mu.shape, lambda b, j, pt: (0, 0)),
            pl.BlockSpec((1, n_rows, HEAD_DIM), lambda b, j, pt: (b, 0, 0)),
            pl.BlockSpec((n_rows, HEAD_DIM), lambda b, j, pt: (0, 0)),
            pl.BlockSpec((lb_logits.shape[0], HEAD_DIM), lambda b, j, pt: (0, h_head(b, j))),
            h_comp(COMP_QB), h_comp(COMP_FB), h_comp(COMP_IB), h_comp(COMP_ZB),
            pl.BlockSpec((1, HEAD_DIM), lambda b, j, pt: (0, h_head(b, j))),
            pl.BlockSpec(lb_logits.shape, lambda b, j, pt: (0, 0)),
            s_comp(COMP_QB), s_comp(COMP_FB), s_comp(COMP_IB), s_comp(COMP_ZB),
            pl.BlockSpec((1, width_b), lambda b, j, pt: (0, 0)),
            state_spec,
        ],
        out_specs=[
            pl.BlockSpec((1, n_rows, HEAD_DIM), lambda b, j, pt: (b, 0, 0)),
            pl.BlockSpec((r_len, HEAD_DIM), lambda b, j, pt: (h_rows(b, j), h_head(b, j))),
            pl.BlockSpec((1, 1, HEAD_DIM, HEAD_DIM),
                         lambda b, j, pt: (stream(b, j) // n_heads_b, h_head(b, j), 0, 0)),
            pl.BlockSpec((SAMPLE_PAD, width_b), lambda b, j, pt: (b, 0)),
            state_spec,
        ],
        scratch_shapes=[pltpu.VMEM((n_rows, HEAD_DIM), F32), pltpu.VMEM((n_rows, HEAD_DIM), F32),
                        page_buf, page_buf, pltpu.SemaphoreType.DMA((DECODE_SLOTS, 2, n_pg)),
                        pltpu.VMEM((HEAD_DIM, HEAD_DIM), F32)],
    )
    return pl.pallas_call(
        functools.partial(_sb_decode_kernel, n_valid=n_valid, n_pg=n_pg, h_steps=h_steps),
        grid_spec=grid_spec,
        out_shape=[
            jax.ShapeDtypeStruct((dec_batch, n_rows, HEAD_DIM), BF16),
            jax.ShapeDtypeStruct((batch * seq, n_heads_b * HEAD_DIM), BF16),
            jax.ShapeDtypeStruct((batch, n_heads_b, HEAD_DIM, HEAD_DIM), F32),
            jax.ShapeDtypeStruct((dec_batch * SAMPLE_PAD, width_b), BF16),
            jax.ShapeDtypeStruct(state.shape, F32),
        ],
        compiler_params=pltpu.CompilerParams(
            dimension_semantics=("arbitrary", "arbitrary"), vmem_limit_bytes=VMEM_LIMIT),
        name="sb_decode_hgrn_prompt",
    )(page_table, q_rows, bias_rows, p3s, p3s, cache_k, cache_v, mu, z_r, gain_r,
      lb_logits, p3, p3, p3, p3, gain_b,
      lb_logits, p3s, p3s, p3s, p3s, gain_b, state)


def _out_proj_kernel(x_ref, ma_ref, mb_ref, xs_ref, mas_ref, mbs_ref, wa_ref, wb_ref, fw_ref, o_ref, os_ref):
    wa = wa_ref[...].astype(BF16)
    wb = wb_ref[...].astype(BF16)

    def project(x, m_a, m_b):
        y = x + _dot(m_a, wa) + _dot(m_b, wb)
        ms = jnp.mean(y * y, axis=-1, keepdims=True)
        return y * lax.rsqrt(ms + EPS) * fw_ref[...]

    o_ref[...] = project(x_ref[...], ma_ref[...], mb_ref[...])

    @pl.when(pl.program_id(0) == 0)
    def _():
        os_ref[...] = project(xs_ref[...], mas_ref[...], mbs_ref[...])


def _out_proj(x, m_a, m_b, xs, m_a_s, m_b_s, w, final_w, bm):
    t, d = x.shape
    ts = xs.shape[0]
    wa = m_a.shape[1]
    wb = m_b.shape[1]
    assert wa == wb
    whole = lambda rows, cols: pl.BlockSpec((rows, cols), lambda m: (0, 0))
    return pl.pallas_call(
        _out_proj_kernel,
        grid=(t // bm,),
        in_specs=[
            pl.BlockSpec((bm, d), lambda m: (m, 0)),
            pl.BlockSpec((bm, wa), lambda m: (m, 0)),
            pl.BlockSpec((bm, wb), lambda m: (m, 0)),
            whole(ts, d), whole(ts, wa), whole(ts, wb),
            pl.BlockSpec((wa, d), lambda m: (0, 0)),
            pl.BlockSpec((wb, d), lambda m: (1, 0)),
            whole(1, d),
        ],
        out_specs=[pl.BlockSpec((bm, d), lambda m: (m, 0)), whole(ts, d)],
        out_shape=[jax.ShapeDtypeStruct((t, d), F32), jax.ShapeDtypeStruct((ts, d), F32)],
        compiler_params=pltpu.CompilerParams(
            dimension_semantics=("arbitrary",), vmem_limit_bytes=VMEM_LIMIT),
        name="out_proj",
    )(x, m_a, m_b, xs, m_a_s, m_b_s, w, w, final_w)


def kernel(x_prompt, x_sample, cache_k, cache_v, state_s, page_table, norm_w, w_in, gain_a, gain_b,
           sb_bias, lb_logits, w_out, final_norm_w):
    depth = norm_w.shape[0]
    assert depth == 1
    batch, seq, d_model = x_prompt.shape
    dec_batch, dec_seq, _ = x_sample.shape
    n_heads_a, head_dim = cache_k.shape[3:]
    w_a = n_heads_a * head_dim
    assert head_dim == HEAD_DIM and state_s.shape[3:] == (HEAD_DIM, HEAD_DIM)
    assert dec_seq <= SAMPLE_PAD and seq % SB_TILE == 0
    assert (batch * seq) % IN_PROJ_ROWS == 0 and (batch * seq) % OUT_PROJ_ROWS == 0

    nw = norm_w[0][None, :]
    fw = final_norm_w[None, :]
    ga = gain_a[0][None, :]
    gb = gain_b[0][None, :]

    xp = x_prompt.reshape(batch * seq, d_model)
    xs = jnp.pad(x_sample, ((0, 0), (0, SAMPLE_PAD - dec_seq), (0, 0))).reshape(dec_batch * SAMPLE_PAD, d_model)
    p3, p3s = _in_proj(xp, xs, nw, w_in[0], bm=IN_PROJ_ROWS)
    m_a, k_p, v_p = _sb_prompt(p3, sb_bias, ga, batch, seq)
    k_p = k_p.reshape(1, batch, seq, n_heads_a, head_dim)
    v_p = v_p.reshape(1, batch, seq, n_heads_a, head_dim)

    rows = lambda c: p3s[c].reshape(dec_batch, SAMPLE_PAD, -1)[:, :dec_seq]
    q_t = rows(COMP_QA).reshape(dec_batch, dec_seq, n_heads_a, head_dim).transpose(0, 2, 1, 3)
    q_rows = (q_t[:, :, None, :, :] * jnp.eye(n_heads_a, dtype=F32)[None, :, :, None, None]).reshape(
        dec_batch, n_heads_a, n_heads_a * dec_seq, head_dim).astype(BF16)
    bias_rows = jnp.broadcast_to(jnp.repeat(sb_bias[0], dec_seq)[:, None], (n_heads_a * dec_seq, head_dim))
    to_rows = lambda a: a.reshape(dec_batch, dec_seq, n_heads_a, head_dim).transpose(0, 2, 1, 3).reshape(
        dec_batch, n_heads_a * dec_seq, head_dim)
    z_r = to_rows(rows(COMP_ZA))
    gain_r = jnp.repeat(gain_a[0].reshape(n_heads_a, head_dim), dec_seq, axis=0)
    ck = cache_k[0].reshape(cache_k.shape[1], cache_k.shape[2] * n_heads_a, head_dim)
    cv = cache_v[0].reshape(cache_v.shape[1], cache_v.shape[2] * n_heads_a, head_dim)
    o_r, m_b, s_p, m_b_s, s_s = _sb_decode_hgrn_prompt(
        page_table, q_rows, bias_rows, p3s, ck, cv, z_r, gain_r, dec_seq,
        p3, lb_logits, gb, batch, seq, state_s[0])
    m_a_s = o_r.reshape(dec_batch, n_heads_a, dec_seq, head_dim).transpose(0, 2, 1, 3).reshape(
        dec_batch, dec_seq, w_a)
    m_a_s = jnp.pad(m_a_s, ((0, 0), (0, SAMPLE_PAD - dec_seq), (0, 0))).reshape(dec_batch * SAMPLE_PAD, w_a)
    y_p, y_s = _out_proj(xp, m_a, m_b, xs, m_a_s, m_b_s, w_out[0], fw, bm=OUT_PROJ_ROWS)
    y_prompt = y_p.reshape(batch, seq, d_model)
    y_sample = y_s.reshape(dec_batch, SAMPLE_PAD, d_model)[:, :dec_seq]
    k_s = rows(COMP_KA).reshape(1, dec_batch, dec_seq, n_heads_a, head_dim)
    v_s = rows(COMP_VA).reshape(1, dec_batch, dec_seq, n_heads_a, head_dim)

    return (y_prompt, y_sample, k_p, v_p, s_p[None], k_s, v_s, s_s[None])
```

```python
import functools

import jax
import jax.numpy as jnp
from jax import lax
from jax.experimental import pallas as pl
from jax.experimental.pallas import tpu as pltpu

EPS = 1e-6
HEAD_DIM = 128
N_COMP = 8
COMP_QA, COMP_KA, COMP_VA, COMP_ZA, COMP_QB, COMP_FB, COMP_IB, COMP_ZB = range(N_COMP)
IN_PROJ_ROWS = 1024
OUT_PROJ_ROWS = 512
SB_BLOCK = 128
SB_TILE = 512
LOG2E = 1.4426950408889634
HGRN_CHUNK = 64
CUMSUM_GROUP = 256
HGRN_HEADS_PER_STEP = 2
HGRN_SUB = 16
SAMPLE_PAD = 8
DECODE_PAGES_PER_STEP = 16
DECODE_SLOTS = 3
PAGE_DMA_PRIORITY = 1
VMEM_LIMIT = 60 * 1024 * 1024

F32 = jnp.float32
BF16 = jnp.bfloat16


def _dot(a, b, **kw):
    return jnp.dot(a, b, preferred_element_type=F32, **kw)


def _dot_nt(a, b):
    return lax.dot_general(a, b, (((1,), (1,)), ((), ())), preferred_element_type=F32)


def _dot_tn(a, b):
    return lax.dot_general(a, b, (((0,), (0,)), ((), ())), preferred_element_type=F32)


def _sigmoid(x):
    return 1.0 / (1.0 + jnp.exp(-x))


def _softplus2(z2):
    return jnp.maximum(z2, 0.0) + jnp.log2(1.0 + jnp.exp2(-jnp.abs(z2)))


def _split_bf16(x):
    hi = x.astype(BF16)
    lo = (x - hi.astype(F32)).astype(BF16)
    return hi, lo


def _in_proj_kernel(x_ref, xs_ref, nw_ref, w_ref, o_ref, os_ref, h_scr):
    m, n = pl.program_id(0), pl.program_id(1)

    def normed(x):
        ms = jnp.mean(x * x, axis=-1, keepdims=True)
        return (x * lax.rsqrt(ms + EPS) * nw_ref[...]).astype(BF16)

    @pl.when(n == 0)
    def _():
        h_scr[...] = normed(x_ref[...])

    @pl.when(m == 0)
    def _():
        os_ref[0] = _dot(normed(xs_ref[...]), w_ref[...].astype(BF16))

    o_ref[0] = _dot(h_scr[...], w_ref[...].astype(BF16))


def _in_proj(x, xs, norm_w, w, bm):
    t, d = x.shape
    ts = xs.shape[0]
    d_in = w.shape[1]
    bn = d_in // N_COMP
    return pl.pallas_call(
        _in_proj_kernel,
        grid=(t // bm, N_COMP),
        in_specs=[
            pl.BlockSpec((bm, d), lambda m, n: (m, 0)),
            pl.BlockSpec((ts, d), lambda m, n: (0, 0)),
            pl.BlockSpec((1, d), lambda m, n: (0, 0)),
            pl.BlockSpec((d, bn), lambda m, n: (0, n)),
        ],
        out_specs=[
            pl.BlockSpec((1, bm, bn), lambda m, n: (n, m, 0)),
            pl.BlockSpec((1, ts, bn), lambda m, n: (jnp.where(m == 0, n, N_COMP - 1), 0, 0)),
        ],
        out_shape=[
            jax.ShapeDtypeStruct((N_COMP, t, bn), F32),
            jax.ShapeDtypeStruct((N_COMP, ts, bn), F32),
        ],
        scratch_shapes=[pltpu.VMEM((bm, d), BF16)],
        compiler_params=pltpu.CompilerParams(
            dimension_semantics=("arbitrary", "arbitrary"), vmem_limit_bytes=VMEM_LIMIT),
        name="in_proj",
    )(x, xs, norm_w, w)


def _head_norm_gate(o, gain, z):
    ms = jnp.mean(o * o, axis=-1, keepdims=True)
    return o * lax.rsqrt(ms + EPS) * gain * (z * _sigmoid(z))


def _sb_prompt_kernel(bias_ref, q_ref, k_ref, v_ref, z_ref, gain_ref, mu_ref, o_ref, ko_ref, vo_ref,
                      kb_scr, vb_scr, acc_scr, carry_scr):
    tq = SB_TILE
    bk = SB_BLOCK
    t_len = q_ref.shape[1]
    bias2 = bias_ref[0, pl.program_id(1)] * LOG2E
    qscale = HEAD_DIM ** -0.5 * LOG2E
    kb_scr[...] = k_ref[0].astype(BF16)
    vb_scr[...] = v_ref[0].astype(BF16)
    ko_ref[...] = k_ref[0]
    vo_ref[...] = v_ref[0]
    gain = gain_ref[...]

    def sweep(z, carry, mask):
        sp = _softplus2(z)
        if mask is not None:
            sp = jnp.where(mask, sp, 0.0)
        zs = z - sp
        hi, lo = _split_bf16(sp)
        ws = [None] * (z.shape[1] // bk)
        for c in reversed(range(len(ws))):
            cols = slice(c * bk, (c + 1) * bk)
            r = _dot(jnp.concatenate([hi[:, cols], lo[:, cols]], axis=1), mu_ref[...])
            ws[c] = jnp.exp2(zs[:, cols] - r[:, :bk] - carry)
            carry = carry + r[:, bk:]
        w = jnp.concatenate(ws, axis=1)
        if mask is not None:
            w = jnp.where(mask, w, 0.0)
        return w.astype(BF16), carry

    def tile(q, k0, tk):
        keys = pl.ds(pl.multiple_of(k0, tq), tk)
        w, carry = sweep(_dot_nt(q, kb_scr[keys, :]) + bias2, carry_scr[...], None)
        carry_scr[...] = carry
        acc_scr[...] += _dot(w, vb_scr[keys, :])

    def own_tile(q, qs, with_prev):
        half = tq // 2
        carry = carry_scr[...]
        new_keys = pl.ds(pl.multiple_of(qs + half, half), half)
        row = lax.broadcasted_iota(jnp.int32, (half, half), 0)
        col = lax.broadcasted_iota(jnp.int32, (half, half), 1)
        w_new, carry_late = sweep(_dot_nt(q[half:], kb_scr[new_keys, :]) + bias2, carry[half:], col < row)
        acc_late = _dot(w_new, vb_scr[new_keys, :])
        carry = jnp.concatenate([carry[:half], carry_late], axis=0)

        n_old = tq + half if with_prev else half
        old_keys = pl.ds(pl.multiple_of(qs - (tq if with_prev else 0), half), n_old)
        row = lax.broadcasted_iota(jnp.int32, (tq, n_old), 0)
        col = lax.broadcasted_iota(jnp.int32, (tq, n_old), 1) - (n_old - half)
        w_old, carry = sweep(_dot_nt(q, kb_scr[old_keys, :]) + bias2, carry, (col < row) | (row >= half))
        carry_scr[...] = carry
        acc_scr[...] += (_dot(w_old, vb_scr[old_keys, :])
                         + jnp.concatenate([jnp.zeros((half, HEAD_DIM), F32), acc_late], axis=0))

    def q_tile(qt, _):
        qs = pl.multiple_of(qt * tq, tq)
        q = (q_ref[0, pl.ds(qs, tq), :] * qscale).astype(BF16)
        acc_scr[...] = jnp.zeros_like(acc_scr)
        carry_scr[...] = jnp.zeros_like(carry_scr)
        @pl.when(qt == 0)
        def _():
            own_tile(q, qs, False)

        @pl.when(qt > 0)
        def _():
            own_tile(q, qs, True)

        def k_pair(jj, _):
            tile(q, qs - tq - (jj + 1) * 2 * tq, 2 * tq)
            return 0

        lax.fori_loop(0, jnp.maximum(qt - 1, 0) // 2, k_pair, 0)

        @pl.when((qt > 0) & (qt % 2 == 0))
        def _():
            tile(q, 0, tq)

        zg = z_ref[0, pl.ds(qs, tq), :]
        o_ref[pl.ds(qs, tq), :] = _head_norm_gate(acc_scr[...], gain, zg).astype(o_ref.dtype)
        return 0

    lax.fori_loop(0, t_len // tq, q_tile, 0)


def _sb_matrix(n, pieces):
    j = jnp.arange(pieces * n)[:, None] % n
    s = jnp.arange(2 * n)[None, :]
    return jnp.where(s < n, (j > s), True).astype(BF16)


def _sb_prompt(p3, sb_bias, gain_a, batch, seq):
    n_heads = p3.shape[2] // HEAD_DIM
    mu = _sb_matrix(SB_BLOCK, 2)
    comp = lambda c: pl.BlockSpec((1, seq, HEAD_DIM), lambda b, h: (c, b, h))
    return pl.pallas_call(
        _sb_prompt_kernel,
        grid=(batch, n_heads),
        in_specs=[
            pl.BlockSpec(memory_space=pltpu.SMEM),
            comp(COMP_QA), comp(COMP_KA), comp(COMP_VA), comp(COMP_ZA),
            pl.BlockSpec((1, HEAD_DIM), lambda b, h: (0, h)),
            pl.BlockSpec(mu.shape, lambda b, h: (0, 0)),
        ],
        out_specs=[pl.BlockSpec((seq, HEAD_DIM), lambda b, h: (b, h))] * 3,
        out_shape=[
            jax.ShapeDtypeStruct((batch * seq, n_heads * HEAD_DIM), BF16),
            jax.ShapeDtypeStruct((batch * seq, n_heads * HEAD_DIM), p3.dtype),
            jax.ShapeDtypeStruct((batch * seq, n_heads * HEAD_DIM), p3.dtype),
        ],
        scratch_shapes=[pltpu.VMEM((seq, HEAD_DIM), BF16), pltpu.VMEM((seq, HEAD_DIM), BF16),
                        pltpu.VMEM((SB_TILE, HEAD_DIM), F32), pltpu.VMEM((SB_TILE, SB_BLOCK), F32)],
        compiler_params=pltpu.CompilerParams(
            dimension_semantics=("arbitrary", "arbitrary"), vmem_limit_bytes=VMEM_LIMIT),
        name="sb_prompt",
    )(sb_bias, p3, p3, p3, p3, gain_a, mu)


def _lower_bound(l):
    e = jnp.exp(l - jnp.max(l, axis=0, keepdims=True))
    return e[0:1, :] / jnp.sum(e, axis=0, keepdims=True)


def _chunk_cumsum(x, c_len):
    rows = x.shape[0]
    group = min(rows, CUMSUM_GROUP)
    ti = lax.broadcasted_iota(jnp.int32, (group, group), 0)
    si = lax.broadcasted_iota(jnp.int32, (group, group), 1)
    tri = jnp.where((si <= ti) & (si // c_len == ti // c_len), 1.0, 0.0).astype(BF16)
    hi = x.astype(BF16)
    rest = x - hi.astype(F32)
    mid = rest.astype(BF16)
    lo = (rest - mid.astype(F32)).astype(BF16)
    pieces = jnp.concatenate([hi, mid, lo], axis=1)
    outs = []
    for g0 in range(0, rows, group):
        r = _dot(tri, pieces[g0:g0 + group])
        outs.append((r[:, :HEAD_DIM] + r[:, HEAD_DIM:2 * HEAD_DIM]) + r[:, 2 * HEAD_DIM:])
    return jnp.concatenate(outs, axis=0)


def _hgrn_chunks(qb, fb, ib, lb, st, c_len, n_valid=None):
    rows = qb.shape[0]
    sub = min(HGRN_SUB, c_len)
    q = qb * _sigmoid(qb)
    g = lb + (1.0 - lb) * _sigmoid(fb)
    kk = 1.0 - g
    lg = jnp.log(g)
    if n_valid is not None:
        live = lax.broadcasted_iota(jnp.int32, (rows, HEAD_DIM), 0) < n_valid
        kk = jnp.where(live, kk, 0.0)
        lg = jnp.where(live, lg, 0.0)
    yield
    b = _chunk_cumsum(lg, c_len)
    vb = ib.astype(BF16)
    chunks = [slice(c * c_len, (c + 1) * c_len) for c in range(rows // c_len)]
    spans, operands = [], []
    for ch in chunks:
        for lo_r in range(ch.start, ch.stop, sub):
            hi_r = lo_r + sub
            m = b[lo_r + sub // 2:lo_r + sub // 2 + 1, :]
            qi = (q[lo_r:hi_r] * jnp.exp(b[lo_r:hi_r] - m)).astype(BF16)
            ki = (kk[ch.start:hi_r] * jnp.exp(m - b[ch.start:hi_r])).astype(BF16)
            spans.append((ch.start, lo_r, hi_r))
            operands.append((qi, ki))
    yield
    atts = [_dot_nt(qi, ki) for qi, ki in operands]
    masked = []
    for att, (c0, lo_r, hi_r) in zip(atts, spans):
        t_pos = lax.broadcasted_iota(jnp.int32, att.shape, 0) + (lo_r - c0)
        s_pos = lax.broadcasted_iota(jnp.int32, att.shape, 1)
        masked.append(jnp.where(s_pos <= t_pos, att, 0.0).astype(BF16))
    yield
    o_intra = jnp.concatenate([_dot(att, vb[c0:hi_r]) for att, (c0, _, hi_r) in zip(masked, spans)], axis=0)
    lasts = [b[ch.stop - 1:ch.stop, :] for ch in chunks]
    decayed = [(kk[ch] * jnp.exp(bl - b[ch])).astype(BF16) for ch, bl in zip(chunks, lasts)]
    yield
    incs = [_dot_tn(vb[ch], kd) for ch, kd in zip(chunks, decayed)]
    states = [st]
    for inc, bl in zip(incs, lasts):
        states.append(states[-1] * jnp.exp(bl) + inc)
    starts = [((q[ch] * jnp.exp(b[ch])).astype(BF16), s.astype(BF16)) for ch, s in zip(chunks, states)]
    yield
    o_inter = [_dot_nt(qe, s) for qe, s in starts]
    return o_intra + jnp.concatenate(o_inter, axis=0), states[-1]


def _in_lockstep(generators):
    values = [None] * len(generators)
    running = dict(enumerate(generators))
    while running:
        for i, gen in list(running.items()):
            try:
                next(gen)
            except StopIteration as done:
                values[i] = done.value
                del running[i]
    return values


def _hgrn_decode_kernel(lbl_ref, q_ref, f_ref, i_ref, z_ref, gain_ref, s0_ref, o_ref, s_ref, *, n_valid):
    n_heads = s0_ref.shape[1]
    head = lambda h: slice(h * HEAD_DIM, (h + 1) * HEAD_DIM)
    lb = _lower_bound(lbl_ref[...])
    results = _in_lockstep([
        _hgrn_chunks(q_ref[0, :, head(h)], f_ref[0, :, head(h)], i_ref[0, :, head(h)], lb[:, head(h)],
                     s0_ref[0, h].T, SAMPLE_PAD, n_valid=n_valid)
        for h in range(n_heads)])
    for h, (o, st) in enumerate(results):
        o_ref[:, head(h)] = _head_norm_gate(o, gain_ref[:, head(h)], z_ref[0, :, head(h)]).astype(o_ref.dtype)
        s_ref[0, h] = st.T


def _sb_decode_kernel(pt_ref, qr_ref, bias_ref, kn_ref, vn_ref, ck_hbm, cv_hbm, mu_ref, z_ref, gain_ref,
                      lbl_ref, hq_ref, hf_ref, hi_ref, hz_ref, hgain_ref,
                      lbls_ref, sq_ref, sf_ref, si_ref, sz_ref, sgain_ref, s0_ref,
                      o_ref, hb_ref, hs_ref, so_ref, ss_ref,
                      acc_scr, carry_scr, kbuf, vbuf, sems, st_scr, *, n_valid, n_pg, h_steps):
    j = pl.program_id(1)
    n_steps = pl.num_programs(1)
    step = pl.program_id(0) * n_steps + j
    n_slots = kbuf.shape[0]
    n_heads, n_rows = qr_ref.shape[1:3]
    n_q = n_rows // n_heads
    page = kbuf.shape[2] // n_heads

    def page_copies(t):
        seq, js, slot = t // n_steps, t % n_steps, t % n_slots
        copies = []
        for i in range(n_pg):
            src = pt_ref[seq, n_steps * n_pg - 1 - (js * n_pg + i)]
            copies.append(pltpu.make_async_copy(ck_hbm.at[src], kbuf.at[slot, i], sems.at[slot, 0, i]))
            copies.append(pltpu.make_async_copy(cv_hbm.at[src], vbuf.at[slot, i], sems.at[slot, 1, i]))
        return copies

    @pl.when(step == 0)
    def _():
        for t in range(n_slots - 1):
            for copy in page_copies(t):
                copy.start(priority=PAGE_DMA_PRIORITY)

    @pl.when(step + n_slots - 1 < pl.num_programs(0) * n_steps)
    def _():
        for copy in page_copies(step + n_slots - 1):
            copy.start(priority=PAGE_DMA_PRIORITY)

    zscale = HEAD_DIM ** -0.5 * LOG2E
    bias2 = bias_ref[...] * LOG2E
    head = lambda h: slice(h * HEAD_DIM, (h + 1) * HEAD_DIM)

    pairs = [(h, h + 1) for h in range(0, n_heads, 2)]
    q_pairs = [jnp.concatenate([qr_ref[0, a], qr_ref[0, b]], axis=1) for a, b in pairs]

    def scores(k_heads):
        z = sum(_dot_nt(qp, jnp.concatenate([k_heads[a].astype(BF16), k_heads[b].astype(BF16)], axis=1))
                for qp, (a, b) in zip(q_pairs, pairs))
        return z * zscale + bias2[:, :z.shape[1]]

    def weighted_values(w, v_heads):
        row_head = lax.broadcasted_iota(jnp.int32, w.shape, 0) // n_q
        own = lambda h: jnp.where(row_head == h, w, 0.0).astype(BF16)
        if w.shape[1] % HEAD_DIM:
            return sum(_dot(own(h), v_heads[h].astype(BF16)) for h in range(n_heads))
        return sum(_dot(jnp.concatenate([own(a), own(b)], axis=1),
                        jnp.concatenate([v_heads[a].astype(BF16), v_heads[b].astype(BF16)], axis=0))
                   for a, b in pairs)

    @pl.when(j == 0)
    def _():
        r = kn_ref.shape[1]
        z = scores([kn_ref[0, :, head(h)] for h in range(n_heads)])
        t_row = lax.broadcasted_iota(jnp.int32, (n_rows, r), 0) % n_q
        s_col = lax.broadcasted_iota(jnp.int32, (n_rows, r), 1)
        valid = (s_col < t_row) & (s_col < n_valid)
        sp = jnp.where(valid, _softplus2(z), 0.0)
        later_mat = jnp.where(lax.broadcasted_iota(jnp.int32, (r, r), 0)
                              > lax.broadcasted_iota(jnp.int32, (r, r), 1), 1.0, 0.0)
        later = _dot(sp, later_mat, precision=lax.Precision.HIGHEST)
        w = jnp.where(valid, jnp.exp2(z - sp - later), 0.0)
        carry_scr[...] = jnp.broadcast_to(jnp.sum(sp, axis=1, keepdims=True), carry_scr.shape)
        acc_scr[...] = weighted_values(w, [vn_ref[0, :, head(h)] for h in range(n_heads)])
        _hgrn_decode_kernel(lbls_ref, sq_ref, sf_ref, si_ref, sz_ref, sgain_ref, s0_ref, so_ref, ss_ref,
                            n_valid=n_valid)

    h_step = step % h_steps

    @pl.when(h_step == 0)
    def _():
        st_scr[...] = jnp.zeros_like(st_scr)

    for copy in page_copies(step):
        copy.wait()
    slot = step % n_slots
    kp_refs = [kbuf.at[slot, i] for i in range(n_pg)]
    vp_refs = [vbuf.at[slot, i] for i in range(n_pg)]
    def sweep_pages():
        k_pages = [[kp_ref[pl.ds(h, page, stride=n_heads), :] for h in range(n_heads)] for kp_ref in kp_refs]
        yield
        zs = [scores(k_heads) for k_heads in k_pages]
        sps = [_softplus2(z) for z in zs]
        pieces = [jnp.concatenate(_split_bf16(sp), axis=1) for sp in sps]
        yield
        rs = [_dot(p, mu_ref[...]) for p in pieces]
        carry = carry_scr[...]
        ws = []
        for z, sp, r in zip(zs, sps, rs):
            ws.append(jnp.exp2(z - sp - r[:, :page] - carry))
            carry = carry + r[:, page:]
        carry_scr[...] = carry
        v_pages = [[vp_ref[pl.ds(h, page, stride=n_heads), :] for h in range(n_heads)] for vp_ref in vp_refs]
        yield
        acc_scr[...] += sum(weighted_values(w, v_heads) for w, v_heads in zip(ws, v_pages))

    lb = _lower_bound(lbl_ref[...])
    n_hb = st_scr.shape[0]
    results = _in_lockstep(
        [sweep_pages()] +
        [_hgrn_chunks(hq_ref[0, :, head(h)], hf_ref[0, :, head(h)], hi_ref[0, :, head(h)], lb[:, head(h)],
                      st_scr[h], HGRN_CHUNK) for h in range(n_hb)])
    for h, (hb, st) in enumerate(results[1:]):
        hb_ref[:, head(h)] = _head_norm_gate(hb, hgain_ref[:, head(h)], hz_ref[0, :, head(h)]).astype(hb_ref.dtype)
        st_scr[h] = st

    @pl.when(h_step == h_steps - 1)
    def _():
        for h in range(n_hb):
            hs_ref[0, h] = st_scr[h].T

    @pl.when(j == pl.num_programs(1) - 1)
    def _():
        o_ref[0] = _head_norm_gate(acc_scr[...], gain_ref[...], z_ref[0]).astype(o_ref.dtype)


def _sb_decode_hgrn_prompt(page_table, q_rows, bias_rows, p3s, cache_k, cache_v, z_r, gain_r, n_valid,
                           p3, lb_logits, gain_b, batch, seq, state):
    dec_batch, n_pages = page_table.shape
    page_rows = cache_k.shape[1]
    n_heads, n_rows = q_rows.shape[1:3]
    width = n_heads * HEAD_DIM
    page = page_rows // n_heads
    assert page == HEAD_DIM
    mu = _sb_matrix(page, 2)
    new = lambda c: pl.BlockSpec((1, SAMPLE_PAD, width), lambda b, j, pt: (c, b, 0))
    n_pg = DECODE_PAGES_PER_STEP
    assert n_pages % n_pg == 0 and dec_batch * (n_pages // n_pg) >= DECODE_SLOTS - 1
    page_buf = pltpu.VMEM((DECODE_SLOTS, n_pg, page_rows, HEAD_DIM), cache_k.dtype)
    n_steps = n_pages // n_pg
    n_heads_b = p3.shape[2] // HEAD_DIM
    n_hb = HGRN_HEADS_PER_STEP if n_heads_b % HGRN_HEADS_PER_STEP == 0 else 1
    n_groups = n_heads_b // n_hb
    r_len = batch * n_groups * seq // (dec_batch * n_steps)
    h_steps = seq // r_len
    assert r_len * dec_batch * n_steps == batch * n_groups * seq and seq % r_len == 0 and r_len % HGRN_CHUNK == 0
    stream = lambda b, j: (b * n_steps + j) // h_steps
    h_rows = lambda b, j: (stream(b, j) // n_groups) * h_steps + (b * n_steps + j) % h_steps
    h_head = lambda b, j: stream(b, j) % n_groups
    h_width = n_hb * HEAD_DIM
    h_comp = lambda c: pl.BlockSpec((1, r_len, h_width), lambda b, j, pt: (c, h_rows(b, j), h_head(b, j)))
    width_b = n_heads_b * HEAD_DIM
    state_spec = pl.BlockSpec((1, n_heads_b, HEAD_DIM, HEAD_DIM), lambda b, j, pt: (b, 0, 0, 0))
    s_comp = lambda c: pl.BlockSpec((1, SAMPLE_PAD, width_b), lambda b, j, pt: (c, b, 0))
    grid_spec = pltpu.PrefetchScalarGridSpec(
        num_scalar_prefetch=1,
        grid=(dec_batch, n_pages // n_pg),
        in_specs=[
            pl.BlockSpec((1, n_heads, n_rows, HEAD_DIM), lambda b, j, pt: (b, 0, 0, 0)),
            pl.BlockSpec((n_rows, HEAD_DIM), lambda b, j, pt: (0, 0)),
            new(COMP_KA), new(COMP_VA),
            pl.BlockSpec(memory_space=pl.ANY), pl.BlockSpec(memory_space=pl.ANY),
            pl.BlockSpec(mu.shape, lambda b, j, pt: (0, 0)),
            pl.BlockSpec((1, n_rows, HEAD_DIM), lambda b, j, pt: (b, 0, 0)),
            pl.BlockSpec((n_rows, HEAD_DIM), lambda b, j, pt: (0, 0)),
            pl.BlockSpec((lb_logits.shape[0], h_width), lambda b, j, pt: (0, h_head(b, j))),
            h_comp(COMP_QB), h_comp(COMP_FB), h_comp(COMP_IB), h_comp(COMP_ZB),
            pl.BlockSpec((1, h_width), lambda b, j, pt: (0, h_head(b, j))),
            pl.BlockSpec(lb_logits.shape, lambda b, j, pt: (0, 0)),
            s_comp(COMP_QB), s_comp(COMP_FB), s_comp(COMP_IB), s_comp(COMP_ZB),
            pl.BlockSpec((1, width_b), lambda b, j, pt: (0, 0)),
            state_spec,
        ],
        out_specs=[
            pl.BlockSpec((1, n_rows, HEAD_DIM), lambda b, j, pt: (b, 0, 0)),
            pl.BlockSpec((r_len, h_width), lambda b, j, pt: (h_rows(b, j), h_head(b, j))),
            pl.BlockSpec((1, n_hb, HEAD_DIM, HEAD_DIM),
                         lambda b, j, pt: (stream(b, j) // n_groups, h_head(b, j), 0, 0)),
            pl.BlockSpec((SAMPLE_PAD, width_b), lambda b, j, pt: (b, 0)),
            state_spec,
        ],
        scratch_shapes=[pltpu.VMEM((n_rows, HEAD_DIM), F32), pltpu.VMEM((n_rows, HEAD_DIM), F32),
                        page_buf, page_buf, pltpu.SemaphoreType.DMA((DECODE_SLOTS, 2, n_pg)),
                        pltpu.VMEM((n_hb, HEAD_DIM, HEAD_DIM), F32)],
    )
    return pl.pallas_call(
        functools.partial(_sb_decode_kernel, n_valid=n_valid, n_pg=n_pg, h_steps=h_steps),
        grid_spec=grid_spec,
        out_shape=[
            jax.ShapeDtypeStruct((dec_batch, n_rows, HEAD_DIM), BF16),
            jax.ShapeDtypeStruct((batch * seq, n_heads_b * HEAD_DIM), BF16),
            jax.ShapeDtypeStruct((batch, n_heads_b, HEAD_DIM, HEAD_DIM), F32),
            jax.ShapeDtypeStruct((dec_batch * SAMPLE_PAD, width_b), BF16),
            jax.ShapeDtypeStruct(state.shape, F32),
        ],
        compiler_params=pltpu.CompilerParams(
            dimension_semantics=("arbitrary", "arbitrary"), vmem_limit_bytes=VMEM_LIMIT),
        name="sb_decode_hgrn_prompt",
    )(page_table, q_rows, bias_rows, p3s, p3s, cache_k, cache_v, mu, z_r, gain_r,
      lb_logits, p3, p3, p3, p3, gain_b,
      lb_logits, p3s, p3s, p3s, p3s, gain_b, state)


def _out_proj_kernel(x_ref, ma_ref, mb_ref, xs_ref, mas_ref, mbs_ref, wa_ref, wb_ref, fw_ref, o_ref, os_ref):
    wa = wa_ref[...].astype(BF16)
    wb = wb_ref[...].astype(BF16)

    def project(x, m_a, m_b):
        y = x + _dot(m_a, wa) + _dot(m_b, wb)
        ms = jnp.mean(y * y, axis=-1, keepdims=True)
        return y * lax.rsqrt(ms + EPS) * fw_ref[...]

    o_ref[...] = project(x_ref[...], ma_ref[...], mb_ref[...])

    @pl.when(pl.program_id(0) == 0)
    def _():
        os_ref[...] = project(xs_ref[...], mas_ref[...], mbs_ref[...])


def _out_proj(x, m_a, m_b, xs, m_a_s, m_b_s, w, final_w, bm):
    t, d = x.shape
    ts = xs.shape[0]
    wa = m_a.shape[1]
    wb = m_b.shape[1]
    assert wa == wb
    whole = lambda rows, cols: pl.BlockSpec((rows, cols), lambda m: (0, 0))
    return pl.pallas_call(
        _out_proj_kernel,
        grid=(t // bm,),
        in_specs=[
            pl.BlockSpec((bm, d), lambda m: (m, 0)),
            pl.BlockSpec((bm, wa), lambda m: (m, 0)),
            pl.BlockSpec((bm, wb), lambda m: (m, 0)),
            whole(ts, d), whole(ts, wa), whole(ts, wb),
            pl.BlockSpec((wa, d), lambda m: (0, 0)),
            pl.BlockSpec((wb, d), lambda m: (1, 0)),
            whole(1, d),
        ],
        out_specs=[pl.BlockSpec((bm, d), lambda m: (m, 0)), whole(ts, d)],
        out_shape=[jax.ShapeDtypeStruct((t, d), F32), jax.ShapeDtypeStruct((ts, d), F32)],
        compiler_params=pltpu.CompilerParams(
            dimension_semantics=("arbitrary",), vmem_limit_bytes=VMEM_LIMIT),
        name="out_proj",
    )(x, m_a, m_b, xs, m_a_s, m_b_s, w, w, final_w)


def kernel(x_prompt, x_sample, cache_k, cache_v, state_s, page_table, norm_w, w_in, gain_a, gain_b,
           sb_bias, lb_logits, w_out, final_norm_w):
    depth = norm_w.shape[0]
    assert depth == 1
    batch, seq, d_model = x_prompt.shape
    dec_batch, dec_seq, _ = x_sample.shape
    n_heads_a, head_dim = cache_k.shape[3:]
    w_a = n_heads_a * head_dim
    assert head_dim == HEAD_DIM and state_s.shape[3:] == (HEAD_DIM, HEAD_DIM)
    assert dec_seq <= SAMPLE_PAD and seq % SB_TILE == 0
    assert (batch * seq) % IN_PROJ_ROWS == 0 and (batch * seq) % OUT_PROJ_ROWS == 0

    nw = norm_w[0][None, :]
    fw = final_norm_w[None, :]
    ga = gain_a[0][None, :]
    gb = gain_b[0][None, :]

    xp = x_prompt.reshape(batch * seq, d_model)
    xs = jnp.pad(x_sample, ((0, 0), (0, SAMPLE_PAD - dec_seq), (0, 0))).reshape(dec_batch * SAMPLE_PAD, d_model)
    p3, p3s = _in_proj(xp, xs, nw, w_in[0], bm=IN_PROJ_ROWS)
    m_a, k_p, v_p = _sb_prompt(p3, sb_bias, ga, batch, seq)
    k_p = k_p.reshape(1, batch, seq, n_heads_a, head_dim)
    v_p = v_p.reshape(1, batch, seq, n_heads_a, head_dim)

    rows = lambda c: p3s[c].reshape(dec_batch, SAMPLE_PAD, -1)[:, :dec_seq]
    q_t = rows(COMP_QA).reshape(dec_batch, dec_seq, n_heads_a, head_dim).transpose(0, 2, 1, 3)
    q_rows = (q_t[:, :, None, :, :] * jnp.eye(n_heads_a, dtype=F32)[None, :, :, None, None]).reshape(
        dec_batch, n_heads_a, n_heads_a * dec_seq, head_dim).astype(BF16)
    bias_rows = jnp.broadcast_to(jnp.repeat(sb_bias[0], dec_seq)[:, None], (n_heads_a * dec_seq, head_dim))
    to_rows = lambda a: a.reshape(dec_batch, dec_seq, n_heads_a, head_dim).transpose(0, 2, 1, 3).reshape(
        dec_batch, n_heads_a * dec_seq, head_dim)
    z_r = to_rows(rows(COMP_ZA))
    gain_r = jnp.repeat(gain_a[0].reshape(n_heads_a, head_dim), dec_seq, axis=0)
    ck = cache_k[0].reshape(cache_k.shape[1], cache_k.shape[2] * n_heads_a, head_dim)
    cv = cache_v[0].reshape(cache_v.shape[1], cache_v.shape[2] * n_heads_a, head_dim)
    o_r, m_b, s_p, m_b_s, s_s = _sb_decode_hgrn_prompt(
        page_table, q_rows, bias_rows, p3s, ck, cv, z_r, gain_r, dec_seq,
        p3, lb_logits, gb, batch, seq, state_s[0])
    m_a_s = o_r.reshape(dec_batch, n_heads_a, dec_seq, head_dim).transpose(0, 2, 1, 3).reshape(
        dec_batch, dec_seq, w_a)
    m_a_s = jnp.pad(m_a_s, ((0, 0), (0, SAMPLE_PAD - dec_seq), (0, 0))).reshape(dec_batch * SAMPLE_PAD, w_a)
    y_p, y_s = _out_proj(xp, m_a, m_b, xs, m_a_s, m_b_s, w_out[0], fw, bm=OUT_PROJ_ROWS)
    y_prompt = y_p.reshape(batch, seq, d_model)
    y_sample = y_s.reshape(dec_batch, SAMPLE_PAD, d_model)[:, :dec_seq]
    k_s = rows(COMP_KA).reshape(1, dec_batch, dec_seq, n_heads_a, head_dim)
    v_s = rows(COMP_VA).reshape(1, dec_batch, dec_seq, n_heads_a, head_dim)

    return (y_prompt, y_sample, k_p, v_p, s_p[None], k_s, v_s, s_s[None])
```
